```python
import math
import jax
import jax.numpy as jnp
from jax import lax
import numpy as np

D_MODEL = 1024
BATCH = 16
SEQ = 2048
DEPTH = 2

GRID_W = 64
CTX_LEN = 256
EPS = 1e-6

W_MIX = 512
N_BRANCH = 3
CHUNK = 128
SGU_HEADS = 4
CONV_WIDTH = 3
S5_GROUP_CH = 16
S5_GROUPS = W_MIX // S5_GROUP_CH
S5_STATE = 64
DT_MIN = 0.01
DT_MAX = 0.1
A_U0 = 0
A_V0 = A_U0 + W_MIX
B_B0 = A_V0 + W_MIX
B_C0 = B_B0 + W_MIX
B_H0 = B_C0 + W_MIX
S5_COL0 = B_H0 + W_MIX
GATE0 = S5_COL0 + W_MIX
D_IN = GATE0 + N_BRANCH * D_MODEL
N_EXPERTS = 16
N_EXPERT_GROUPS = 4
EXPERTS_PER_GROUP = N_EXPERTS // N_EXPERT_GROUPS
TOPK_GROUPS = 1
GROUP_SCORE_TOPK = 2
TOP_K = 2
D_FF_EXPERT = 512

kernel_name = "hybrid_sgu_conv_s5_moe_dit"


def _rmsnorm(x, g):
    xf = x.astype(jnp.float32)
    y = xf * lax.rsqrt(jnp.mean(xf * xf, axis=-1, keepdims=True) + EPS)
    return (y * g.astype(jnp.float32)).astype(x.dtype)


def _layernorm(x, g):
    xf = x.astype(jnp.float32)
    xc = xf - jnp.mean(xf, axis=-1, keepdims=True)
    y = xc * lax.rsqrt(jnp.mean(xc * xc, axis=-1, keepdims=True) + EPS)
    return (y * g.astype(jnp.float32)).astype(x.dtype)


def _adaln(cvec, w, b):
    m = jax.nn.silu(cvec) @ w + b
    return jnp.split(m[..., None, :], 6, axis=-1)


def _modulate(h, shift, scale):
    return h * (1.0 + scale) + shift


def _chunk_sgu(u, v, norm_g, w_s, b_s):
    u = jax.nn.gelu(u)
    v = _layernorm(jax.nn.gelu(v), norm_g)
    bsz, n, w = v.shape
    vc = v.reshape(bsz, n // CHUNK, CHUNK, SGU_HEADS, w // SGU_HEADS)
    mixed = jnp.einsum("hqk,bnkhc->bnqhc", w_s, vc) + b_s.T[:, :, None]
    return u * mixed.reshape(bsz, n, w)


def _conv3_axis(h, w, axis):
    n = h.shape[axis]
    pad = [(0, 0)] * h.ndim
    pad[axis] = (1, 1)
    hp = jnp.pad(h, pad)
    prev = lax.slice_in_dim(hp, 0, n, axis=axis)
    nxt = lax.slice_in_dim(hp, 2, n + 2, axis=axis)
    return w[0] * prev + w[1] * h + w[2] * nxt


def _short_conv(h, w, grid):
    if grid:
        bsz, n, ch = h.shape
        rows = n // GRID_W
        hg = h.reshape(bsz, rows, GRID_W, ch)
        return _conv3_axis(hg, w, 2).reshape(bsz, n, ch)
    return _conv3_axis(h, w, 1)


def _cmul(ar, ai, br, bi):
    return ar * br - ai * bi, ar * bi + ai * br


def _ssm_combine(e1, e2):
    a1r, a1i, b1r, b1i = e1
    a2r, a2i, b2r, b2i = e2
    ar, ai = _cmul(a2r, a2i, a1r, a1i)
    br, bi = _cmul(a2r, a2i, b1r, b1i)
    return ar, ai, br + b2r, bi + b2i


def _s5_discretise(lam_re, lam_im, log_dt, b_re, b_im):
    lam_re = lam_re.astype(jnp.float32)
    lam_im = lam_im.astype(jnp.float32)
    dt = jnp.exp(log_dt.astype(jnp.float32))[:, None]
    mag = jnp.exp(lam_re * dt)
    ab_re = mag * jnp.cos(lam_im * dt)
    ab_im = mag * jnp.sin(lam_im * dt)
    den = lam_re * lam_re + lam_im * lam_im
    nr = ab_re - 1.0
    f_re = (nr * lam_re + ab_im * lam_im) / den
    f_im = (ab_im * lam_re - nr * lam_im) / den
    bb_re, bb_im = _cmul(f_re[..., None], f_im[..., None], b_re.astype(jnp.float32), b_im.astype(jnp.float32))
    return ab_re, ab_im, bb_re, bb_im


def _s5_states(u, p, h0s):
    bsz, n, _ = u.shape
    ug = u.reshape(bsz, n, S5_GROUPS, S5_GROUP_CH).astype(jnp.float32)
    states = []
    for d in range(2):
        reverse = d == 1
        ab_re, ab_im, bb_re, bb_im = _s5_discretise(
            p["s5_lam_re"][d], p["s5_lam_im"][d], p["s5_log_dt"][d], p["s5_b_re"][d], p["s5_b_im"][d])
        bu_re = jnp.einsum("gph,blgh->blgp", bb_re, ug)
        bu_im = jnp.einsum("gph,blgh->blgp", bb_im, ug)
        if h0s is not None:
            i0 = n - 1 if reverse else 0
            c_re, c_im = _cmul(ab_re, ab_im, h0s[d][0], h0s[d][1])
            bu_re = bu_re.at[:, i0].add(c_re)
            bu_im = bu_im.at[:, i0].add(c_im)
        a_re = jnp.broadcast_to(ab_re, (1, n) + ab_re.shape)
        a_im = jnp.broadcast_to(ab_im, (1, n) + ab_im.shape)
        _, _, s_re, s_im = lax.associative_scan(
            _ssm_combine, (a_re, a_im, bu_re, bu_im), reverse=reverse, axis=1)
        states.append((s_re, s_im))
    return states


def _final_states(states):
    (f_re, f_im), (b_re, b_im) = states
    return ((f_re[:, -1], f_im[:, -1]), (b_re[:, 0], b_im[:, 0]))


def _s5_readout(u, states, p):
    bsz, n, w = u.shape
    ug = u.reshape(bsz, n, S5_GROUPS, S5_GROUP_CH).astype(jnp.float32)
    y = ug * p["s5_d"].astype(jnp.float32).reshape(S5_GROUPS, S5_GROUP_CH)
    for d in range(2):
        s_re, s_im = states[d]
        y = (y + jnp.einsum("ghp,blgp->blgh", p["s5_c_re"][d].astype(jnp.float32), s_re)
             - jnp.einsum("ghp,blgp->blgh", p["s5_c_im"][d].astype(jnp.float32), s_im))
    z = jax.nn.gelu(y.reshape(bsz, n, w))
    out = z * jax.nn.sigmoid(z @ p["glu_w"].astype(jnp.float32) + p["glu_b"].astype(jnp.float32))
    return out.astype(u.dtype)


def _token_mix(hn, p, grid, s5_h0):
    proj = hn @ p["w_in"]
    a_u, a_v, b_b, b_c, b_h, c_u, gates = jnp.split(
        proj, [A_V0, B_B0, B_C0, B_H0, S5_COL0, GATE0], axis=-1)
    y_a = _chunk_sgu(a_u, a_v, p["sgu_norm_g"], p["sgu_w"], p["sgu_b"])
    y_b = b_b * _short_conv(b_c * b_h, p["conv_w"], grid)
    states = _s5_states(c_u, p, s5_h0)
    y_c = _s5_readout(c_u, states, p)
    g = jax.nn.sigmoid(gates.astype(jnp.float32)).astype(hn.dtype)
    merged = None
    for i, y in enumerate((y_a, y_b, y_c)):
        term = g[..., i * D_MODEL:(i + 1) * D_MODEL] * (y @ p["w_branch"][i])
        merged = term if merged is None else merged + term
    return merged @ p["w_out"], states


def _moe(h, router_w, router_b, w_gate, w_up, w_down):
    bsz, n, d = h.shape
    t = h.reshape(bsz * n, d)
    scores = jax.nn.sigmoid((t @ router_w).astype(jnp.float32))
    biased = (scores + router_b.astype(jnp.float32)).reshape(-1, N_EXPERT_GROUPS, EXPERTS_PER_GROUP)
    group_score = jnp.sum(lax.top_k(biased, GROUP_SCORE_TOPK)[0], axis=-1)
    _, g_idx = lax.top_k(group_score, TOPK_GROUPS)
    g_mask = jnp.sum(jax.nn.one_hot(g_idx, N_EXPERT_GROUPS, dtype=jnp.float32), axis=1) > 0
    masked = jnp.where(g_mask[:, :, None], biased, -jnp.inf).reshape(-1, N_EXPERTS)
    _, e_idx = lax.top_k(masked, TOP_K)
    w_sel = jnp.take_along_axis(scores, e_idx, axis=1)
    w_sel = w_sel / jnp.sum(w_sel, axis=-1, keepdims=True)
    gates = jnp.sum(jax.nn.one_hot(e_idx, N_EXPERTS, dtype=jnp.float32) * w_sel[..., None], axis=1)
    gates = gates.astype(h.dtype)
    out = jnp.zeros_like(t)
    for e in range(N_EXPERTS):
        he = jax.nn.silu(t @ w_gate[e]) * (t @ w_up[e])
        out = out + gates[:, e:e + 1] * (he @ w_down[e])
    return out.reshape(bsz, n, d)


def setup_inputs(seed: int = 0) -> dict:
    key = jax.random.key(seed)
    ks = jax.random.split(key, 32)
    f32 = jnp.float32

    def nrm(k, shape, scale):
        return jax.random.normal(k, shape, f32) * scale

    G, P, H = S5_GROUPS, S5_STATE, S5_GROUP_CH
    lam_im_base = jnp.pi * jnp.arange(P, dtype=f32)
    return {
        "x": nrm(ks[0], (BATCH, SEQ, D_MODEL), 1.0),
        "c": nrm(ks[1], (BATCH, D_MODEL), 1.0),
        "ctx": nrm(ks[2], (BATCH, CTX_LEN, D_MODEL), 1.0),
        "c_ctx": nrm(ks[3], (D_MODEL,), 1.0),
        "w_mod": nrm(ks[4], (DEPTH, D_MODEL, 6 * D_MODEL), 0.5 * D_MODEL ** -0.5),
        "b_mod": nrm(ks[5], (DEPTH, 6 * D_MODEL), 0.02),
        "norm1_g": 1.0 + nrm(ks[6], (DEPTH, D_MODEL), 0.02),
        "norm2_g": 1.0 + nrm(ks[7], (DEPTH, D_MODEL), 0.02),
        "w_in": nrm(ks[8], (DEPTH, D_MODEL, D_IN), D_MODEL ** -0.5),
        "sgu_norm_g": 1.0 + nrm(ks[9], (DEPTH, W_MIX), 0.02),
        "sgu_w": nrm(ks[10], (DEPTH, SGU_HEADS, CHUNK, CHUNK), CHUNK ** -0.5),
        "sgu_b": nrm(ks[11], (DEPTH, SGU_HEADS, CHUNK), 0.02),
        "conv_w": nrm(ks[12], (DEPTH, CONV_WIDTH, W_MIX), CONV_WIDTH ** -0.5),
        "s5_lam_re": -0.5 + nrm(ks[13], (DEPTH, 2, G, P), 0.01),
        "s5_lam_im": lam_im_base + nrm(ks[14], (DEPTH, 2, G, P), 0.01),
        "s5_log_dt": jax.random.uniform(ks[15], (DEPTH, 2, G), f32, math.log(DT_MIN), math.log(DT_MAX)),
        "s5_b_re": nrm(ks[16], (DEPTH, 2, G, P, H), (2 * H) ** -0.5),
        "s5_b_im": nrm(ks[17], (DEPTH, 2, G, P, H), (2 * H) ** -0.5),
        "s5_c_re": nrm(ks[18], (DEPTH, 2, G, H, P), P ** -0.5),
        "s5_c_im": nrm(ks[19], (DEPTH, 2, G, H, P), P ** -0.5),
        "s5_d": nrm(ks[20], (DEPTH, W_MIX), 1.0),
        "glu_w": nrm(ks[21], (DEPTH, W_MIX, W_MIX), W_MIX ** -0.5),
        "glu_b": nrm(ks[22], (DEPTH, W_MIX), 0.02),
        "w_branch": nrm(ks[23], (DEPTH, N_BRANCH, W_MIX, D_MODEL), W_MIX ** -0.5),
        "w_out": nrm(ks[24], (DEPTH, D_MODEL, D_MODEL), D_MODEL ** -0.5),
        "router_w": nrm(ks[25], (D_MODEL, N_EXPERTS), D_MODEL ** -0.5),
        "router_b": nrm(ks[26], (N_EXPERTS,), 0.01),
        "exp_w_gate": nrm(ks[27], (DEPTH, N_EXPERTS, D_MODEL, D_FF_EXPERT), D_MODEL ** -0.5),
        "exp_w_up": nrm(ks[28], (DEPTH, N_EXPERTS, D_MODEL, D_FF_EXPERT), D_MODEL ** -0.5),
        "exp_w_down": nrm(ks[29], (DEPTH, N_EXPERTS, D_FF_EXPERT, D_MODEL), D_FF_EXPERT ** -0.5),
        "final_norm_g": 1.0 + nrm(ks[30], (D_MODEL,), 0.02),
    }


def reference(x, c, ctx, c_ctx, w_mod, b_mod, norm1_g, norm2_g, w_in, sgu_norm_g, sgu_w, sgu_b,
              conv_w, s5_lam_re, s5_lam_im, s5_log_dt, s5_b_re, s5_b_im, s5_c_re, s5_c_im, s5_d,
              glu_w, glu_b, w_branch, w_out, router_w, router_b, exp_w_gate, exp_w_up, exp_w_down,
              final_norm_g):
    h, hc = x, ctx
    for l in range(DEPTH):
        last = l == DEPTH - 1
        p = {
            "w_in": w_in[l], "sgu_norm_g": sgu_norm_g[l], "sgu_w": sgu_w[l], "sgu_b": sgu_b[l],
            "conv_w": conv_w[l], "s5_lam_re": s5_lam_re[l], "s5_lam_im": s5_lam_im[l],
            "s5_log_dt": s5_log_dt[l], "s5_b_re": s5_b_re[l], "s5_b_im": s5_b_im[l],
            "s5_c_re": s5_c_re[l], "s5_c_im": s5_c_im[l], "s5_d": s5_d[l],
            "glu_w": glu_w[l], "glu_b": glu_b[l], "w_branch": w_branch[l], "w_out": w_out[l],
        }
        sh1, sc1, g1, sh2, sc2, g2 = _adaln(c, w_mod[l], b_mod[l])
        csh1, csc1, cg1, csh2, csc2, cg2 = _adaln(c_ctx, w_mod[l], b_mod[l])
        cn = _modulate(_rmsnorm(hc, norm1_g[l]), csh1, csc1)
        if last:
            ctx_states = _s5_states(cn @ w_in[l][:, S5_COL0:S5_COL0 + W_MIX], p, None)
        else:
            ctx_mix, ctx_states = _token_mix(cn, p, False, None)
        h0 = _final_states(ctx_states)
        xn = _modulate(_rmsnorm(h, norm1_g[l]), sh1, sc1)
        lat_mix, _ = _token_mix(xn, p, True, h0)
        h = h + g1 * lat_mix
        h = h + g2 * _moe(_modulate(_rmsnorm(h, norm2_g[l]), sh2, sc2),
                          router_w, router_b, exp_w_gate[l], exp_w_up[l], exp_w_down[l])
        if not last:
            hc = hc + cg1 * ctx_mix
            hc = hc + cg2 * _moe(_modulate(_rmsnorm(hc, norm2_g[l]), csh2, csc2),
                                 router_w, router_b, exp_w_gate[l], exp_w_up[l], exp_w_down[l])
    return _rmsnorm(h, final_norm_g)
```

```python
import functools

import jax
import jax.numpy as jnp
from jax import lax
from jax.experimental import pallas as pl
from jax.experimental.pallas import tpu as pltpu

F32 = jnp.float32
BF16 = jnp.bfloat16
HIGHEST = lax.Precision.HIGHEST

D_MODEL = 1024
DEPTH = 2
EPS = 1e-6
W_MIX = 512
N_BRANCH = 3
CHUNK = 128
SGU_HEADS = 4
SGU_HEAD_W = W_MIX // SGU_HEADS
GRID_W = 64
S5_H = 16
S5_G = W_MIX // S5_H
S5_P = 64
S5_LC = 16
S5_W = S5_LC * S5_H
S5_COL0 = 5 * W_MIX
GATE0 = 6 * W_MIX
N_EXPERTS = 16
EXPERTS_PER_GROUP = 4
N_GROUPS = 4
D_FF = 512
N_PAIRS = 6
N_CLASSES = N_GROUPS * N_PAIRS
N_MOD = 6

TM_TOK = 256
TM_MOE = 256
VMEM_LIMIT = 56 * 1024 * 1024


def _rms(x, g):
    return x * lax.rsqrt(jnp.mean(x * x, axis=-1, keepdims=True) + EPS) * g


def _bdot(a, b):
    return jnp.dot(a.astype(BF16), b, preferred_element_type=F32)


def _const_spec(shape):
    nd = len(shape)
    return pl.BlockSpec(shape, lambda *_: (0,) * nd, pipeline_mode=pl.Buffered(1))


def _adaln_kernel(c_ref, w_ref, b_ref, o_ref):
    c = c_ref[...]
    s = c * jax.nn.sigmoid(c)
    o_ref[0] = jnp.dot(s, w_ref[0], precision=HIGHEST, preferred_element_type=F32) + b_ref[0]


def _adaln(cvec, w_mod, b_mod):
    rows = cvec.shape[0]
    tn = 1536
    n_mod = N_MOD * D_MODEL
    return pl.pallas_call(
        _adaln_kernel,
        grid=(DEPTH, n_mod // tn),
        in_specs=[
            pl.BlockSpec((rows, D_MODEL), lambda l, j: (0, 0)),
            pl.BlockSpec((1, D_MODEL, tn), lambda l, j: (l, 0, j)),
            pl.BlockSpec((1, 1, tn), lambda l, j: (l, 0, j)),
        ],
        out_specs=pl.BlockSpec((1, rows, tn), lambda l, j: (l, 0, j)),
        out_shape=jax.ShapeDtypeStruct((DEPTH, rows, n_mod), F32),
        compiler_params=pltpu.CompilerParams(
            dimension_semantics=("arbitrary", "arbitrary"), vmem_limit_bytes=VMEM_LIMIT),
        name="adaln",
    )(cvec, w_mod, b_mod.reshape(DEPTH, 1, n_mod))


def _pre_kernel(*refs, residual):
    if residual:
        h_ref, moe_ref, pmod_ref, mod_ref, g_ref, w_ref, hnew_ref, cu_ref = refs
        x = h_ref[...] + pmod_ref[0, 5:6, :] * moe_ref[...]
        hnew_ref[...] = x
    else:
        h_ref, mod_ref, g_ref, w_ref, cu_ref = refs
        x = h_ref[...]
    xn = _rms(x, g_ref[...]) * (1.0 + mod_ref[0, 1:2, :]) + mod_ref[0, 0:1, :]
    cu_ref[...] = _bdot(xn, w_ref[...]).astype(BF16)


def _pre(h, mod, norm_g, w_s5, seq, moe=None, pmod=None):
    t = h.shape[0]
    tm = TM_TOK
    per_seq = seq // tm
    tok = pl.BlockSpec((tm, D_MODEL), lambda i: (i, 0))
    modspec = pl.BlockSpec((1, N_MOD, D_MODEL), lambda i: (i // per_seq, 0, 0))
    residual = moe is not None
    in_specs = [tok]
    args = [h]
    if residual:
        in_specs += [tok, modspec]
        args += [moe, pmod]
    in_specs += [modspec, _const_spec((1, D_MODEL)), _const_spec((D_MODEL, W_MIX))]
    args += [mod, norm_g.reshape(1, D_MODEL), w_s5]
    cu_spec = pl.BlockSpec((tm, W_MIX), lambda i: (i, 0))
    cu_shape = jax.ShapeDtypeStruct((t, W_MIX), BF16)
    if residual:
        out_specs = [tok, cu_spec]
        out_shape = [jax.ShapeDtypeStruct((t, D_MODEL), F32), cu_shape]
    else:
        out_specs = cu_spec
        out_shape = cu_shape
    out = pl.pallas_call(
        functools.partial(_pre_kernel, residual=residual),
        grid=(t // tm,),
        in_specs=in_specs,
        out_specs=out_specs,
        out_shape=out_shape,
        compiler_params=pltpu.CompilerParams(
            dimension_semantics=("arbitrary",), vmem_limit_bytes=VMEM_LIMIT),
        name="pre_res" if residual else "pre",
    )(*args)
    if residual:
        return out[0], out[1]
    return h, out


def _s5_mats(lam_re, lam_im, log_dt, b_re, b_im, c_re, c_im, dvec):
    dt = jnp.exp(log_dt)[..., None]
    mag = jnp.exp(lam_re * dt)
    ab_re = mag * jnp.cos(lam_im * dt)
    ab_im = mag * jnp.sin(lam_im * dt)
    den = lam_re * lam_re + lam_im * lam_im
    nr = ab_re - 1.0
    f_re = (nr * lam_re + ab_im * lam_im) / den
    f_im = (ab_im * lam_re - nr * lam_im) / den
    bb_re = f_re[..., None] * b_re - f_im[..., None] * b_im
    bb_im = f_re[..., None] * b_im + f_im[..., None] * b_re
    k = jnp.arange(S5_LC + 1, dtype=F32)[:, None, None, None]
    pmag = jnp.exp(lam_re * dt * k)
    pw_re = pmag * jnp.cos(lam_im * dt * k)
    pw_im = pmag * jnp.sin(lam_im * dt * k)
    cp_re = c_re[None] * pw_re[:, :, :, None, :] - c_im[None] * pw_im[:, :, :, None, :]
    cp_im = c_re[None] * pw_im[:, :, :, None, :] + c_im[None] * pw_re[:, :, :, None, :]
    kk = (jnp.einsum("kdghp,dgpj->kdghj", cp_re[:S5_LC], bb_re, precision=HIGHEST)
          - jnp.einsum("kdghp,dgpj->kdghj", cp_im[:S5_LC], bb_im, precision=HIGHEST))
    ii = jnp.arange(S5_LC)[:, None]
    jj = jnp.arange(S5_LC)[None, :]
    lag_f = jj - ii
    lag_b = ii - jj
    kf = kk[:, 0][jnp.clip(lag_f, 0, S5_LC - 1)]
    kb = kk[:, 1][jnp.clip(lag_b, 0, S5_LC - 1)]
    tf = jnp.where((lag_f >= 0)[:, :, None, None, None], kf, 0.0)
    tb = jnp.where((lag_b >= 0)[:, :, None, None, None], kb, 0.0)
    dmat = (jnp.eye(S5_LC, dtype=F32)[:, :, None, None, None]
            * jnp.eye(S5_H, dtype=F32)[None, None, None]
            * dvec.reshape(S5_G, S5_H)[None, None, :, :, None])
    tmat = jnp.transpose(tf + tb + dmat, (2, 0, 4, 1, 3)).reshape(S5_G, S5_W, S5_W)

    def cmul_pb(pr, pi, br, bi):
        re = pr[..., None] * br[None] - pi[..., None] * bi[None]
        im = pr[..., None] * bi[None] + pi[..., None] * br[None]
        return jnp.transpose(re, (1, 0, 3, 2)), jnp.transpose(im, (1, 0, 3, 2))

    rev = jnp.arange(S5_LC - 1, -1, -1)
    fwd = jnp.arange(S5_LC)
    if_re, if_im = cmul_pb(pw_re[rev, 0], pw_im[rev, 0], bb_re[0], bb_im[0])
    ib_re, ib_im = cmul_pb(pw_re[fwd, 1], pw_im[fwd, 1], bb_re[1], bb_im[1])
    inj = jnp.concatenate([if_re, if_im, ib_re, ib_im], axis=-1).reshape(S5_G, S5_W, 4 * S5_P)

    def readout(cr, ci):
        return jnp.transpose(cr, (1, 3, 0, 2)), jnp.transpose(ci, (1, 3, 0, 2))

    rf_re, rf_im = readout(cp_re[fwd + 1, 0], cp_im[fwd + 1, 0])
    rb_re, rb_im = readout(cp_re[S5_LC - fwd, 1], cp_im[S5_LC - fwd, 1])
    rmat = jnp.concatenate([rf_re, -rf_im, rb_re, -rb_im], axis=1).reshape(S5_G, 4 * S5_P, S5_W)
    a_re, a_im = pw_re[S5_LC], pw_im[S5_LC]
    pvec = jnp.concatenate([a_re[0], a_re[0], a_re[1], a_re[1]], axis=-1)
    qvec = jnp.concatenate([-a_im[0], a_im[0], -a_im[1], a_im[1]], axis=-1)
    pq = jnp.stack([pvec, qvec], axis=1)
    return tmat.astype(BF16), inj.astype(BF16), rmat.astype(BF16), pq


def _s5_kernel(*refs, n_chunks, n_batch, want_y):
    if want_y:
        u_ref, t_ref, inj_ref, r_ref, pq_ref, h0_ref, y_ref, fin_ref, i_scr, s_scr = refs
    else:
        u_ref, inj_ref, pq_ref, h0_ref, fin_ref, i_scr, s_scr = refs
    half = 2 * S5_P
    u = u_ref[0]
    i_scr[...] = jnp.dot(u, inj_ref[0], preferred_element_type=F32)
    pf = pq_ref[0, 0:1, 0:half]
    qf = pq_ref[0, 1:2, 0:half]
    pb = pq_ref[0, 0:1, half:]
    qb = pq_ref[0, 1:2, half:]

    def step(k, carry):
        sf, sb = carry
        rf = pl.multiple_of(k * n_batch, n_batch)
        rb = pl.multiple_of((n_chunks - 1 - k) * n_batch, n_batch)
        s_scr[pl.ds(rf, n_batch), 0:half] = sf
        s_scr[pl.ds(rb, n_batch), half:] = sb
        sf = pf * sf + qf * pltpu.roll(sf, S5_P, axis=1) + i_scr[pl.ds(rf, n_batch), 0:half]
        sb = pb * sb + qb * pltpu.roll(sb, S5_P, axis=1) + i_scr[pl.ds(rb, n_batch), half:]
        return sf, sb

    h0 = h0_ref[0]
    sf, sb = lax.fori_loop(0, n_chunks, step, (h0[:, 0:half], h0[:, half:]))
    fin_ref[0] = jnp.concatenate([sf, sb], axis=1)
    if want_y:
        y = jnp.dot(u, t_ref[0], preferred_element_type=F32)
        y = y + _bdot(s_scr[...], r_ref[0])
        y_ref[0] = y


def _s5(cu, mats, h0, seq, want_y):
    tmat, inj, rmat, pq = mats
    t = cu.shape[0]
    nb = t // seq
    nc = seq // S5_LC
    rows = nc * nb
    u = cu.reshape(nb, nc, S5_LC, S5_G, S5_H).transpose(3, 1, 0, 2, 4).reshape(S5_G, rows, S5_W)
    if h0 is None:
        h0 = jnp.zeros((S5_G, nb, S5_W), F32)
    grp = lambda shape: pl.BlockSpec((1,) + shape, lambda g: (g, 0, 0))
    in_specs = [grp((rows, S5_W))]
    args = [u]
    if want_y:
        in_specs.append(grp((S5_W, S5_W)))
        args.append(tmat)
    in_specs.append(grp((S5_W, S5_W)))
    args.append(inj)
    if want_y:
        in_specs.append(grp((S5_W, S5_W)))
        args.append(rmat)
    in_specs += [grp((2, S5_W)), grp((nb, S5_W))]
    args += [pq, h0]
    fin_spec = grp((nb, S5_W))
    fin_shape = jax.ShapeDtypeStruct((S5_G, nb, S5_W), F32)
    if want_y:
        out_specs = [grp((rows, S5_W)), fin_spec]
        out_shape = [jax.ShapeDtypeStruct((S5_G, rows, S5_W), F32), fin_shape]
    else:
        out_specs = fin_spec
        out_shape = fin_shape
    out = pl.pallas_call(
        functools.partial(_s5_kernel, n_chunks=nc, n_batch=nb, want_y=want_y),
        grid=(S5_G,),
        in_specs=in_specs,
        out_specs=out_specs,
        out_shape=out_shape,
        scratch_shapes=[pltpu.VMEM((rows, S5_W), F32), pltpu.VMEM((rows, S5_W), F32)],
        compiler_params=pltpu.CompilerParams(
            dimension_semantics=("arbitrary",), vmem_limit_bytes=VMEM_LIMIT),
        name="s5" if want_y else "s5_states",
    )(*args)
    if not want_y:
        return None, out
    y, fin = out
    y = y.reshape(S5_G, nc, nb, S5_LC, S5_H).transpose(2, 1, 3, 0, 4).reshape(t, W_MIX)
    return y, fin


def _route(logits_t, rb_ref):
    s = [jax.nn.sigmoid(logits_t[e:e + 1, :]) for e in range(N_EXPERTS)]
    bz = [s[e] + rb_ref[e] for e in range(N_EXPERTS)]
    gs = []
    for g in range(N_GROUPS):
        v = bz[g * EXPERTS_PER_GROUP:(g + 1) * EXPERTS_PER_GROUP]
        best = None
        for i in range(EXPERTS_PER_GROUP):
            for j in range(i + 1, EXPERTS_PER_GROUP):
                p = v[i] + v[j]
                best = p if best is None else jnp.maximum(best, p)
        gs.append(best)
    gbest = gs[0]
    gi = jnp.zeros_like(gbest, dtype=jnp.int32)
    for g in range(1, N_GROUPS):
        upd = gs[g] > gbest
        gi = jnp.where(upd, g, gi)
        gbest = jnp.where(upd, gs[g], gbest)

    def pick(vals, k):
        out = vals[k]
        for g in range(1, N_GROUPS):
            out = jnp.where(gi == g, vals[g * EXPERTS_PER_GROUP + k], out)
        return out

    v = [pick(bz, k) for k in range(EXPERTS_PER_GROUP)]
    sc = [pick(s, k) for k in range(EXPERTS_PER_GROUP)]

    def argmax4(vals):
        best, idx = vals[0], jnp.zeros_like(gi)
        for k in range(1, EXPERTS_PER_GROUP):
            upd = vals[k] > best
            idx = jnp.where(upd, k, idx)
            best = jnp.where(upd, vals[k], best)
        return idx

    i1 = argmax4(v)
    v2 = [jnp.where(i1 == k, -jnp.inf, v[k]) for k in range(EXPERTS_PER_GROUP)]
    i2 = argmax4(v2)

    def sel(vals, idx):
        out = vals[0]
        for k in range(1, EXPERTS_PER_GROUP):
            out = jnp.where(idx == k, vals[k], out)
        return out

    w1 = sel(sc, i1)
    w2 = sel(sc, i2)
    tot = w1 + w2
    w1 = w1 / tot
    w2 = w2 / tot
    first_low = i1 < i2
    lo = jnp.where(first_low, i1, i2)
    hi = jnp.where(first_low, i2, i1)
    wa = jnp.where(first_low, w1, w2)
    wb = jnp.where(first_low, w2, w1)
    base = gi * EXPERTS_PER_GROUP
    return (base + lo).astype(F32), (base + hi).astype(F32), wa, wb


def _mix_kernel(h_ref, yc_ref, mod_ref, n1_ref, n2_ref, win_ref, sgug_ref, sguw_ref, sgub_ref,
                convw_ref, gluw_ref, glub_ref, wbr_ref, wout_ref, rwt_ref, rb_ref,
                h2_ref, hn2_ref, route_ref, ya_scr, *, conv_w):
    tm = h_ref.shape[0]
    x = h_ref[...]
    xn = _rms(x, n1_ref[...]) * (1.0 + mod_ref[0, 1:2, :]) + mod_ref[0, 0:1, :]
    xb = xn.astype(BF16)

    def proj(c0, width):
        return jnp.dot(xb, win_ref[:, c0:c0 + width], preferred_element_type=F32)

    u = jax.nn.gelu(proj(0, W_MIX))
    v = jax.nn.gelu(proj(W_MIX, W_MIX))
    vc = v - jnp.mean(v, axis=-1, keepdims=True)
    v = vc * lax.rsqrt(jnp.mean(vc * vc, axis=-1, keepdims=True) + EPS) * sgug_ref[...]
    vb = v.astype(BF16)
    for ck in range(tm // CHUNK):
        r0 = ck * CHUNK
        for hh in range(SGU_HEADS):
            c0 = hh * SGU_HEAD_W
            mixed = jnp.dot(sguw_ref[hh], vb[r0:r0 + CHUNK, c0:c0 + SGU_HEAD_W],
                            preferred_element_type=F32) + sgub_ref[:, hh:hh + 1]
            ya_scr[r0:r0 + CHUNK, c0:c0 + SGU_HEAD_W] = (
                u[r0:r0 + CHUNK, c0:c0 + SGU_HEAD_W] * mixed).astype(BF16)
    merged = jax.nn.sigmoid(proj(GATE0 - W_MIX, D_MODEL)) * jnp.dot(
        ya_scr[...], wbr_ref[0], preferred_element_type=F32)

    m = proj(3 * W_MIX, W_MIX) * proj(4 * W_MIX, W_MIX)
    pos = lax.broadcasted_iota(jnp.int32, (tm, 1), 0) % conv_w
    prev = jnp.where(pos == 0, 0.0, pltpu.roll(m, 1, axis=0))
    nxt = jnp.where(pos == conv_w - 1, 0.0, pltpu.roll(m, tm - 1, axis=0))
    yb = proj(2 * W_MIX, W_MIX) * (
        convw_ref[0:1, :] * prev + convw_ref[1:2, :] * m + convw_ref[2:3, :] * nxt)
    merged = merged + jax.nn.sigmoid(proj(GATE0 - W_MIX + D_MODEL, D_MODEL)) * _bdot(yb, wbr_ref[1])

    z = jax.nn.gelu(yc_ref[...])
    yc = z * jax.nn.sigmoid(_bdot(z, gluw_ref[...]) + glub_ref[...])
    merged = merged + jax.nn.sigmoid(proj(GATE0 - W_MIX + 2 * D_MODEL, D_MODEL)) * _bdot(yc, wbr_ref[2])

    h2 = x + mod_ref[0, 2:3, :] * _bdot(merged, wout_ref[...])
    h2_ref[...] = h2
    hn2 = _rms(h2, n2_ref[...]) * (1.0 + mod_ref[0, 4:5, :]) + mod_ref[0, 3:4, :]
    hn2_ref[...] = hn2.astype(BF16)
    logits_t = lax.dot_general(rwt_ref[...], hn2, (((1,), (1,)), ((), ())),
                               precision=HIGHEST, preferred_element_type=F32)
    ea, eb, wa, wb = _route(logits_t, rb_ref)
    zero = jnp.zeros_like(wa)
    route_ref[...] = jnp.concatenate([ea, eb, wa, wb, zero, zero, zero, zero], axis=0)


def _mix(h, yc, mod, lw, seq, conv_w, router_wt, router_b):
    t = h.shape[0]
    tm = TM_TOK
    per_seq = seq // tm
    tok = pl.BlockSpec((tm, D_MODEL), lambda i: (i, 0))
    n_rest = lw["w_rest"].shape[1]
    in_specs = [
        tok,
        pl.BlockSpec((tm, W_MIX), lambda i: (i, 0)),
        pl.BlockSpec((1, N_MOD, D_MODEL), lambda i: (i // per_seq, 0, 0)),
        _const_spec((1, D_MODEL)),
        _const_spec((1, D_MODEL)),
        _const_spec((D_MODEL, n_rest)),
        _const_spec((1, W_MIX)),
        _const_spec((SGU_HEADS, CHUNK, CHUNK)),
        _const_spec((CHUNK, SGU_HEADS)),
        _const_spec((3, W_MIX)),
        _const_spec((W_MIX, W_MIX)),
        _const_spec((1, W_MIX)),
        _const_spec((N_BRANCH, W_MIX, D_MODEL)),
        _const_spec((D_MODEL, D_MODEL)),
        _const_spec((N_EXPERTS, D_MODEL)),
        pl.BlockSpec(memory_space=pltpu.SMEM),
    ]
    out_specs = [tok, tok, pl.BlockSpec((8, tm), lambda i: (0, i))]
    out_shape = [
        jax.ShapeDtypeStruct((t, D_MODEL), F32),
        jax.ShapeDtypeStruct((t, D_MODEL), BF16),
        jax.ShapeDtypeStruct((8, t), F32),
    ]
    return pl.pallas_call(
        functools.partial(_mix_kernel, conv_w=conv_w),
        grid=(t // tm,),
        in_specs=in_specs,
        out_specs=out_specs,
        out_shape=out_shape,
        scratch_shapes=[pltpu.VMEM((tm, W_MIX), BF16)],
        compiler_params=pltpu.CompilerParams(
            dimension_semantics=("arbitrary",), vmem_limit_bytes=VMEM_LIMIT),
        name="mix",
    )(h, yc, mod, lw["norm1_g"], lw["norm2_g"], lw["w_rest"], lw["sgu_norm_g"], lw["sgu_w"],
      lw["sgu_b_t"], lw["conv_w"], lw["glu_w"], lw["glu_b"], lw["w_branch"], lw["w_out"],
      router_wt, router_b)


def _moe_kernel(ea_ref, eb_ref, valid_ref, x_ref, w_ref, gua_ref, da_ref, gub_ref, db_ref, o_ref):
    i = pl.program_id(0)

    @pl.when(valid_ref[i] > 0)
    def _():
        x = x_ref[...]

        def expert(gu_ref, d_ref):
            hh = jnp.dot(x, gu_ref[0], preferred_element_type=F32)
            g = hh[:, :D_FF]
            act = g * jax.nn.sigmoid(g) * hh[:, D_FF:]
            return _bdot(act, d_ref[0])

        o_ref[...] = w_ref[:, 0:1] * expert(gua_ref, da_ref) + w_ref[:, 1:2] * expert(gub_ref, db_ref)

    @pl.when(valid_ref[i] == 0)
    def _():
        o_ref[...] = jnp.zeros_like(o_ref)


def _moe(hn2, route, w_gu, w_d):
    t = hn2.shape[0]
    tm = TM_MOE
    nt = t // tm + N_CLASSES
    tp = nt * tm
    ea = route[0].astype(jnp.int32)
    eb = route[1].astype(jnp.int32)
    lo = ea % EXPERTS_PER_GROUP
    hi = eb % EXPERTS_PER_GROUP
    cls = (ea // EXPERTS_PER_GROUP) * N_PAIRS + (lo * (7 - lo)) // 2 + (hi - lo - 1)
    onehot = (cls[:, None] == jnp.arange(N_CLASSES, dtype=jnp.int32)[None, :]).astype(jnp.int32)
    csum = jnp.cumsum(onehot, axis=0)
    rank = jnp.sum(csum * onehot, axis=1) - 1
    counts = csum[-1]
    padded = ((counts + tm - 1) // tm) * tm
    ends = jnp.cumsum(padded)
    offs = ends - padded
    pos = offs[cls] + rank
    starts = jnp.arange(nt, dtype=jnp.int32) * tm
    tile_cls = jnp.sum((starts[:, None] >= ends[None, :]).astype(jnp.int32), axis=1)
    valid = (tile_cls < N_CLASSES).astype(jnp.int32)
    tile_cls = jnp.minimum(tile_cls, N_CLASSES - 1)
    pair_lo = jnp.array([0, 0, 0, 1, 1, 2], jnp.int32)
    pair_hi = jnp.array([1, 2, 3, 2, 3, 3], jnp.int32)
    tile_ea = (tile_cls // N_PAIRS) * EXPERTS_PER_GROUP + pair_lo[tile_cls % N_PAIRS]
    tile_eb = (tile_cls // N_PAIRS) * EXPERTS_PER_GROUP + pair_hi[tile_cls % N_PAIRS]
    src = jnp.zeros((tp,), jnp.int32).at[pos].set(jnp.arange(t, dtype=jnp.int32))
    ws = jnp.zeros((tp, 2), F32).at[pos].set(jnp.stack([route[2], route[3]], axis=1))
    xs = jnp.take(hn2, src, axis=0)

    grid_spec = pltpu.PrefetchScalarGridSpec(
        num_scalar_prefetch=3,
        grid=(nt,),
        in_specs=[
            pl.BlockSpec((tm, D_MODEL), lambda i, a, b, v: (i, 0)),
            pl.BlockSpec((tm, 2), lambda i, a, b, v: (i, 0)),
            pl.BlockSpec((1, D_MODEL, 2 * D_FF), lambda i, a, b, v: (a[i], 0, 0)),
            pl.BlockSpec((1, D_FF, D_MODEL), lambda i, a, b, v: (a[i], 0, 0)),
            pl.BlockSpec((1, D_MODEL, 2 * D_FF), lambda i, a, b, v: (b[i], 0, 0)),
            pl.BlockSpec((1, D_FF, D_MODEL), lambda i, a, b, v: (b[i], 0, 0)),
        ],
        out_specs=pl.BlockSpec((tm, D_MODEL), lambda i, a, b, v: (i, 0)),
    )
    out = pl.pallas_call(
        _moe_kernel,
        grid_spec=grid_spec,
        out_shape=jax.ShapeDtypeStruct((tp, D_MODEL), F32),
        compiler_params=pltpu.CompilerParams(
            dimension_semantics=("arbitrary",), vmem_limit_bytes=VMEM_LIMIT),
        name="moe",
    )(tile_ea, tile_eb, valid, xs, ws, w_gu, w_d, w_gu, w_d)
    return jnp.take(out, pos, axis=0)


def _final_kernel(h_ref, moe_ref, mod_ref, g_ref, o_ref):
    x = h_ref[...] + mod_ref[0, 5:6, :] * moe_ref[...]
    o_ref[...] = _rms(x, g_ref[...])


def _final(h, moe, mod, g, seq):
    t = h.shape[0]
    tm = TM_TOK
    per_seq = seq // tm
    tok = pl.BlockSpec((tm, D_MODEL), lambda i: (i, 0))
    return pl.pallas_call(
        _final_kernel,
        grid=(t // tm,),
        in_specs=[tok, tok, pl.BlockSpec((1, N_MOD, D_MODEL), lambda i: (i // per_seq, 0, 0)),
                  _const_spec((1, D_MODEL))],
        out_specs=tok,
        out_shape=jax.ShapeDtypeStruct((t, D_MODEL), F32),
        compiler_params=pltpu.CompilerParams(
            dimension_semantics=("arbitrary",), vmem_limit_bytes=VMEM_LIMIT),
        name="final",
    )(h, moe, mod, g.reshape(1, D_MODEL))


def kernel(x, c, ctx, c_ctx, w_mod, b_mod, norm1_g, norm2_g, w_in, sgu_norm_g, sgu_w, sgu_b,
           conv_w, s5_lam_re, s5_lam_im, s5_log_dt, s5_b_re, s5_b_im, s5_c_re, s5_c_im, s5_d,
           glu_w, glu_b, w_branch, w_out, router_w, router_b, exp_w_gate, exp_w_up, exp_w_down,
           final_norm_g):
    nb, seq, _ = x.shape
    ctx_len = ctx.shape[1]
    h = x.reshape(nb * seq, D_MODEL)
    hc = ctx.reshape(nb * ctx_len, D_MODEL)

    mod_rows = 24
    cvec = jnp.zeros((mod_rows, D_MODEL), F32).at[:nb].set(c).at[nb].set(c_ctx)
    mods = _adaln(cvec, w_mod, b_mod).reshape(DEPTH, mod_rows, N_MOD, D_MODEL)
    router_wt = router_w.T

    moe = moe_c = None
    mod = mod_c = None
    for l in range(DEPTH):
        last = l == DEPTH - 1
        pmod, pmod_c = mod, mod_c
        mod = mods[l, :nb]
        mod_c = jnp.broadcast_to(mods[l, nb][None], (nb, N_MOD, D_MODEL))
        w_s5 = w_in[l][:, S5_COL0:S5_COL0 + W_MIX].astype(BF16)
        lw = {
            "norm1_g": norm1_g[l].reshape(1, D_MODEL),
            "norm2_g": norm2_g[l].reshape(1, D_MODEL),
            "w_rest": jnp.concatenate([w_in[l][:, :S5_COL0], w_in[l][:, GATE0:]], axis=1).astype(BF16),
            "sgu_norm_g": sgu_norm_g[l].reshape(1, W_MIX),
            "sgu_w": sgu_w[l].astype(BF16),
            "sgu_b_t": sgu_b[l].T,
            "conv_w": conv_w[l],
            "glu_w": glu_w[l].astype(BF16),
            "glu_b": glu_b[l].reshape(1, W_MIX),
            "w_branch": w_branch[l].astype(BF16),
            "w_out": w_out[l].astype(BF16),
        }
        mats = _s5_mats(s5_lam_re[l], s5_lam_im[l], s5_log_dt[l], s5_b_re[l], s5_b_im[l],
                        s5_c_re[l], s5_c_im[l], s5_d[l])
        w_gu = jnp.concatenate([exp_w_gate[l], exp_w_up[l]], axis=-1).astype(BF16)
        w_d = exp_w_down[l].astype(BF16)

        hc, cu_c = _pre(hc, mod_c, norm1_g[l], w_s5, ctx_len, moe=moe_c, pmod=pmod_c)
        yc_c, fin_c = _s5(cu_c, mats, None, ctx_len, want_y=not last)
        h, cu = _pre(h, mod, norm1_g[l], w_s5, seq, moe=moe, pmod=pmod)
        yc, _ = _s5(cu, mats, fin_c, seq, want_y=True)
        h, hn2, route = _mix(h, yc, mod, lw, seq, GRID_W, router_wt, router_b)
        moe = _moe(hn2, route, w_gu, w_d)
        if not last:
            hc, hn2_c, route_c = _mix(hc, yc_c, mod_c, lw, ctx_len, ctx_len, router_wt, router_b)
            moe_c = _moe(hn2_c, route_c, w_gu, w_d)
    out = _final(h, moe, mod, final_norm_g, seq)
    return out.reshape(nb, seq, D_MODEL)
```

```python
import functools

import jax
import jax.numpy as jnp
from jax import lax
from jax.experimental import pallas as pl
from jax.experimental.pallas import tpu as pltpu

F32 = jnp.float32
BF16 = jnp.bfloat16
HIGHEST = lax.Precision.HIGHEST

D_MODEL = 1024
DEPTH = 2
EPS = 1e-6
W_MIX = 512
N_BRANCH = 3
CHUNK = 128
SGU_HEADS = 4
SGU_HEAD_W = W_MIX // SGU_HEADS
GRID_W = 64
S5_H = 16
S5_G = W_MIX // S5_H
S5_P = 64
S5_LC = 16
S5_W = S5_LC * S5_H
S5_COL0 = 5 * W_MIX
GATE0 = 6 * W_MIX
D_IN = GATE0 + N_BRANCH * D_MODEL
N_EXPERTS = 16
EXPERTS_PER_GROUP = 4
N_GROUPS = 4
D_FF = 512
N_PAIRS = 6
N_CLASSES = N_GROUPS * N_PAIRS
N_MOD = 6
LANES = 128

TM_TOK = 256
TM_MOE = 256
PRE_I_PER_STEP = 4
VMEM_LIMIT = 56 * 1024 * 1024

_NT = (((1,), (1,)), ((), ()))
_TN = (((0,), (0,)), ((), ()))


def _rms(x, g):
    return x * lax.rsqrt(jnp.mean(x * x, axis=-1, keepdims=True) + EPS) * g


def _bdot(a, b):
    return jnp.dot(a.astype(BF16), b, preferred_element_type=F32)


def _const_spec(shape):
    nd = len(shape)
    return pl.BlockSpec(shape, lambda *_: (0,) * nd, pipeline_mode=pl.Buffered(1))


def _params(*sem):
    return pltpu.CompilerParams(dimension_semantics=sem, vmem_limit_bytes=VMEM_LIMIT)


def _adaln_kernel(c_ref, w_ref, b_ref, o_ref):
    c = c_ref[...]
    s = c * jax.nn.sigmoid(c)
    o_ref[0] = jnp.dot(s, w_ref[0], precision=HIGHEST, preferred_element_type=F32) + b_ref[0]


def _adaln(cvec, w_mod, b_mod):
    rows = cvec.shape[0]
    tn = 1536
    n_mod = N_MOD * D_MODEL
    return pl.pallas_call(
        _adaln_kernel,
        grid=(DEPTH, n_mod // tn),
        in_specs=[
            pl.BlockSpec((rows, D_MODEL), lambda l, j: (0, 0)),
            pl.BlockSpec((1, D_MODEL, tn), lambda l, j: (l, 0, j)),
            pl.BlockSpec((1, 1, tn), lambda l, j: (l, 0, j)),
        ],
        out_specs=pl.BlockSpec((1, rows, tn), lambda l, j: (l, 0, j)),
        out_shape=jax.ShapeDtypeStruct((DEPTH, rows, n_mod), F32),
        compiler_params=_params("arbitrary", "arbitrary"),
        name="adaln",
    )(cvec, w_mod, b_mod.reshape(DEPTH, 1, n_mod))


def _pre_kernel(*refs, residual):
    if residual:
        h_ref, moe_ref, pmod_ref, mod_ref, g_ref, wt_ref, hnew_ref, xo_ref = refs
    else:
        h_ref, mod_ref, g_ref, wt_ref, xo_ref = refs
    nbt, nc, _ = h_ref.shape
    for i in range(PRE_I_PER_STEP):
        cols = slice(i * D_MODEL, (i + 1) * D_MODEL)
        x = h_ref[:, :, cols]
        if residual:
            x = x + pmod_ref[:, 5:6, :] * moe_ref[:, :, cols]
            hnew_ref[:, :, cols] = x
        xn = _rms(x, g_ref[...]) * (1.0 + mod_ref[:, 1:2, :]) + mod_ref[:, 0:1, :]
        xb = xn.reshape(nbt * nc, D_MODEL).astype(BF16)
        r = lax.dot_general(wt_ref[...], xb, _NT, preferred_element_type=F32)
        xo_ref[:, i * S5_H:(i + 1) * S5_H, :] = r.astype(BF16).reshape(S5_G, S5_H, nbt * nc)


def _pre(h, mod, norm_g, w_s5t, seq, moe=None, pmod=None):
    t = h.shape[0]
    nb = t // seq
    nc = seq // S5_LC
    nbt = LANES // nc
    n_tiles = nb // nbt
    wide = PRE_I_PER_STEP * D_MODEL
    view = lambda a: a.reshape(nb, nc, S5_LC * D_MODEL)
    tok = pl.BlockSpec((nbt, nc, wide), lambda tb, ib: (tb, 0, ib))
    modspec = pl.BlockSpec((nbt, N_MOD, D_MODEL), lambda tb, ib: (tb, 0, 0))
    residual = moe is not None
    in_specs = [tok]
    args = [view(h)]
    if residual:
        in_specs += [tok, modspec]
        args += [view(moe), pmod]
    in_specs += [modspec, _const_spec((1, D_MODEL)), _const_spec((W_MIX, D_MODEL))]
    args += [mod, norm_g.reshape(1, D_MODEL), w_s5t]
    xo_spec = pl.BlockSpec((S5_G, PRE_I_PER_STEP * S5_H, LANES), lambda tb, ib: (0, ib, tb))
    xo_shape = jax.ShapeDtypeStruct((S5_G, S5_W, nb * nc), BF16)
    if residual:
        out_specs = [tok, xo_spec]
        out_shape = [jax.ShapeDtypeStruct((nb, nc, S5_LC * D_MODEL), F32), xo_shape]
    else:
        out_specs = xo_spec
        out_shape = xo_shape
    out = pl.pallas_call(
        functools.partial(_pre_kernel, residual=residual),
        grid=(n_tiles, S5_LC // PRE_I_PER_STEP),
        in_specs=in_specs,
        out_specs=out_specs,
        out_shape=out_shape,
        compiler_params=_params("arbitrary", "arbitrary"),
        name="pre_res" if residual else "pre",
    )(*args)
    if residual:
        return out[0].reshape(t, D_MODEL), out[1]
    return h, out


def _s5_mats(lam_re, lam_im, log_dt, b_re, b_im, c_re, c_im, dvec):
    dt = jnp.exp(log_dt)[..., None]
    mag = jnp.exp(lam_re * dt)
    ab_re = mag * jnp.cos(lam_im * dt)
    ab_im = mag * jnp.sin(lam_im * dt)
    den = lam_re * lam_re + lam_im * lam_im
    nr = ab_re - 1.0
    f_re = (nr * lam_re + ab_im * lam_im) / den
    f_im = (ab_im * lam_re - nr * lam_im) / den
    bb_re = f_re[..., None] * b_re - f_im[..., None] * b_im
    bb_im = f_re[..., None] * b_im + f_im[..., None] * b_re
    k = jnp.arange(S5_LC + 1, dtype=F32)[:, None, None, None]
    pmag = jnp.exp(lam_re * dt * k)
    pw_re = pmag * jnp.cos(lam_im * dt * k)
    pw_im = pmag * jnp.sin(lam_im * dt * k)
    cp_re = c_re[None] * pw_re[:, :, :, None, :] - c_im[None] * pw_im[:, :, :, None, :]
    cp_im = c_re[None] * pw_im[:, :, :, None, :] + c_im[None] * pw_re[:, :, :, None, :]
    kk = (jnp.einsum("kdghp,dgpj->kdghj", cp_re[:S5_LC], bb_re, precision=HIGHEST)
          - jnp.einsum("kdghp,dgpj->kdghj", cp_im[:S5_LC], bb_im, precision=HIGHEST))
    ii = jnp.arange(S5_LC)[:, None]
    jj = jnp.arange(S5_LC)[None, :]
    lag_f = jj - ii
    lag_b = ii - jj
    kf = kk[:, 0][jnp.clip(lag_f, 0, S5_LC - 1)]
    kb = kk[:, 1][jnp.clip(lag_b, 0, S5_LC - 1)]
    tf = jnp.where((lag_f >= 0)[:, :, None, None, None], kf, 0.0)
    tb = jnp.where((lag_b >= 0)[:, :, None, None, None], kb, 0.0)
    dmat = (jnp.eye(S5_LC, dtype=F32)[:, :, None, None, None]
            * jnp.eye(S5_H, dtype=F32)[None, None, None]
            * dvec.reshape(S5_G, S5_H)[None, None, :, :, None])
    tt = jnp.transpose(tf + tb + dmat, (2, 1, 3, 0, 4)).reshape(S5_G, S5_W, S5_W)

    def cmul_pb(pr, pi, br, bi):
        re = pr[..., None] * br[None] - pi[..., None] * bi[None]
        im = pr[..., None] * bi[None] + pi[..., None] * br[None]
        return jnp.transpose(re, (1, 0, 3, 2)), jnp.transpose(im, (1, 0, 3, 2))

    rev = jnp.arange(S5_LC - 1, -1, -1)
    fwd = jnp.arange(S5_LC)
    if_re, if_im = cmul_pb(pw_re[rev, 0], pw_im[rev, 0], bb_re[0], bb_im[0])
    ib_re, ib_im = cmul_pb(pw_re[fwd, 1], pw_im[fwd, 1], bb_re[1], bb_im[1])
    inj = jnp.concatenate([if_re, ib_re, if_im, ib_im], axis=-1).reshape(S5_G, S5_W, 4 * S5_P)

    def readout(cr, ci):
        return jnp.transpose(cr, (1, 0, 2, 3)), jnp.transpose(ci, (1, 0, 2, 3))

    rf_re, rf_im = readout(cp_re[fwd + 1, 0], cp_im[fwd + 1, 0])
    rb_re, rb_im = readout(cp_re[S5_LC - fwd, 1], cp_im[S5_LC - fwd, 1])
    rt = jnp.concatenate([rf_re, rb_re, -rf_im, -rb_im], axis=-1).reshape(S5_G, S5_W, 4 * S5_P)
    a_re, a_im = pw_re[S5_LC], pw_im[S5_LC]
    pvec = jnp.concatenate([a_re[0], a_re[1]], axis=-1)
    qvec = jnp.concatenate([a_im[0], a_im[1]], axis=-1)
    pq = jnp.stack([pvec, qvec], axis=1)
    return tt.astype(BF16), inj.astype(BF16), rt.astype(BF16), pq


def _s5_kernel(*refs, n_chunks, n_batch, want_y):
    if want_y:
        x_ref, tt_ref, inj_ref, rt_ref, pq_ref, h0_ref, y_ref, fin_ref = refs[:8]
    else:
        x_ref, inj_ref, pq_ref, h0_ref, fin_ref = refs[:5]
    i_re_scr, i_im_scr, sf_re_scr, sf_im_scr, sb_re_scr, sb_im_scr = refs[-6:]
    half = 2 * S5_P
    x = x_ref[0]
    inj = lax.dot_general(x, inj_ref[0], _TN, preferred_element_type=F32)
    i_re_scr[...] = inj[:, 0:half]
    i_im_scr[...] = inj[:, half:]
    p = pq_ref[0, 0:1, :]
    q = pq_ref[0, 1:2, :]
    fwd_lanes = lax.broadcasted_iota(jnp.int32, (n_batch, half), 1) < S5_P
    h0 = h0_ref[0]
    s_re, s_im = h0[:, 0:half], h0[:, half:]
    for k in range(n_chunks):
        rows_f = pl.ds(k, n_batch, stride=n_chunks)
        rows_b = pl.ds(n_chunks - 1 - k, n_batch, stride=n_chunks)
        sf_re_scr[rows_f, :] = s_re
        sf_im_scr[rows_f, :] = s_im
        sb_re_scr[rows_b, :] = s_re
        sb_im_scr[rows_b, :] = s_im
        i_re = jnp.where(fwd_lanes, i_re_scr[rows_f, :], i_re_scr[rows_b, :])
        i_im = jnp.where(fwd_lanes, i_im_scr[rows_f, :], i_im_scr[rows_b, :])
        s_re, s_im = p * s_re - q * s_im + i_re, p * s_im + q * s_re + i_im
    fin_ref[0] = jnp.concatenate([s_re, s_im], axis=1)
    if want_y:
        fwd = lax.broadcasted_iota(jnp.int32, (1, half), 1) < S5_P
        s_prev = jnp.concatenate(
            [jnp.where(fwd, sf_re_scr[...], sb_re_scr[...]),
             jnp.where(fwd, sf_im_scr[...], sb_im_scr[...])], axis=1).astype(BF16)
        y = jnp.dot(tt_ref[0], x, preferred_element_type=F32)
        y = y + lax.dot_general(rt_ref[0], s_prev, _NT, preferred_element_type=F32)
        y_ref[0] = y


def _s5(xg, mats, h0, n_batch, want_y):
    tt, inj, rt, pq = mats
    n = xg.shape[2]
    nc = n // n_batch
    if h0 is None:
        h0 = jnp.zeros((S5_G, n_batch, S5_W), F32)
    grp = lambda shape: pl.BlockSpec((1,) + shape, lambda g: (g, 0, 0))
    sq = grp((S5_W, S5_W))
    if want_y:
        in_specs = [grp((S5_W, n)), sq, sq, sq]
        args = [xg, tt, inj, rt]
    else:
        in_specs = [grp((S5_W, n)), sq]
        args = [xg, inj]
    in_specs += [grp((2, 2 * S5_P)), grp((n_batch, S5_W))]
    args += [pq, h0]
    fin_spec = grp((n_batch, S5_W))
    fin_shape = jax.ShapeDtypeStruct((S5_G, n_batch, S5_W), F32)
    if want_y:
        out_specs = [grp((S5_W, n)), fin_spec]
        out_shape = [jax.ShapeDtypeStruct((S5_G, S5_W, n), F32), fin_shape]
    else:
        out_specs = fin_spec
        out_shape = fin_shape
    out = pl.pallas_call(
        functools.partial(_s5_kernel, n_chunks=nc, n_batch=n_batch, want_y=want_y),
        grid=(S5_G,),
        in_specs=in_specs,
        out_specs=out_specs,
        out_shape=out_shape,
        scratch_shapes=[pltpu.VMEM((n, 2 * S5_P), F32)] * 6,
        compiler_params=_params("arbitrary"),
        name="s5" if want_y else "s5_states",
    )(*args)
    if not want_y:
        return None, out
    return out[0], out[1]


def _s5out_kernel(y_ref, gwt_ref, gb_ref, wb_ref, o_ref):
    nbt, nc, _ = o_ref.shape
    for j in range(S5_LC):
        z = jax.nn.gelu(y_ref[:, j * S5_H:(j + 1) * S5_H, :].reshape(W_MIX, nbt * nc))
        gate = jnp.dot(gwt_ref[...], z.astype(BF16), preferred_element_type=F32) + gb_ref[...]
        yc = (z * jax.nn.sigmoid(gate)).astype(BF16)
        pr = lax.dot_general(yc, wb_ref[...], _TN, preferred_element_type=F32)
        o_ref[:, :, j * D_MODEL:(j + 1) * D_MODEL] = pr.reshape(nbt, nc, D_MODEL).astype(BF16)


def _s5out(y, glu_wt, glu_b, w_branch_c, n_batch):
    n = y.shape[2]
    nc = n // n_batch
    nbt = LANES // nc
    out = pl.pallas_call(
        _s5out_kernel,
        grid=(n_batch // nbt,),
        in_specs=[
            pl.BlockSpec((S5_G, S5_W, LANES), lambda tb: (0, 0, tb)),
            _const_spec((W_MIX, W_MIX)),
            _const_spec((W_MIX, 1)),
            _const_spec((W_MIX, D_MODEL)),
        ],
        out_specs=pl.BlockSpec((nbt, nc, S5_LC * D_MODEL), lambda tb: (tb, 0, 0)),
        out_shape=jax.ShapeDtypeStruct((n_batch, nc, S5_LC * D_MODEL), BF16),
        compiler_params=_params("arbitrary"),
        name="s5out",
    )(y, glu_wt, glu_b, w_branch_c)
    return out.reshape(n_batch * nc * S5_LC, D_MODEL)


def _route(logits_t, rb_ref):
    s = [jax.nn.sigmoid(logits_t[e:e + 1, :]) for e in range(N_EXPERTS)]
    bz = [s[e] + rb_ref[e] for e in range(N_EXPERTS)]
    gs = []
    for g in range(N_GROUPS):
        v = bz[g * EXPERTS_PER_GROUP:(g + 1) * EXPERTS_PER_GROUP]
        best = None
        for i in range(EXPERTS_PER_GROUP):
            for j in range(i + 1, EXPERTS_PER_GROUP):
                pair = v[i] + v[j]
                best = pair if best is None else jnp.maximum(best, pair)
        gs.append(best)
    gbest = gs[0]
    gi = jnp.zeros_like(gbest, dtype=jnp.int32)
    for g in range(1, N_GROUPS):
        upd = gs[g] > gbest
        gi = jnp.where(upd, g, gi)
        gbest = jnp.where(upd, gs[g], gbest)

    def pick(vals, k):
        out = vals[k]
        for g in range(1, N_GROUPS):
            out = jnp.where(gi == g, vals[g * EXPERTS_PER_GROUP + k], out)
        return out

    v = [pick(bz, k) for k in range(EXPERTS_PER_GROUP)]
    sc = [pick(s, k) for k in range(EXPERTS_PER_GROUP)]

    def argmax4(vals):
        best, idx = vals[0], jnp.zeros_like(gi)
        for k in range(1, EXPERTS_PER_GROUP):
            upd = vals[k] > best
            idx = jnp.where(upd, k, idx)
            best = jnp.where(upd, vals[k], best)
        return idx

    i1 = argmax4(v)
    v2 = [jnp.where(i1 == k, -jnp.inf, v[k]) for k in range(EXPERTS_PER_GROUP)]
    i2 = argmax4(v2)

    def sel(vals, idx):
        out = vals[0]
        for k in range(1, EXPERTS_PER_GROUP):
            out = jnp.where(idx == k, vals[k], out)
        return out

    w1 = sel(sc, i1)
    w2 = sel(sc, i2)
    tot = w1 + w2
    w1 = w1 / tot
    w2 = w2 / tot
    first_low = i1 < i2
    lo = jnp.where(first_low, i1, i2)
    hi = jnp.where(first_low, i2, i1)
    wa = jnp.where(first_low, w1, w2)
    wb = jnp.where(first_low, w2, w1)
    base = gi * EXPERTS_PER_GROUP
    return (base + lo).astype(F32), (base + hi).astype(F32), wa, wb


def _mix_kernel(h_ref, ycp_ref, mod_ref, n1_ref, n2_ref, win_ref, sgug_ref, sguw_ref, sgub_ref,
                convw_ref, wbr_ref, wout_ref, rwt_ref, rb_ref,
                h2_ref, hn2_ref, route_ref, ya_scr, *, conv_w):
    tm = h_ref.shape[0]
    x = h_ref[...]
    xn = _rms(x, n1_ref[...]) * (1.0 + mod_ref[0, 1:2, :]) + mod_ref[0, 0:1, :]
    xb = xn.astype(BF16)

    def proj(c0, width):
        return jnp.dot(xb, win_ref[:, c0:c0 + width], preferred_element_type=F32)

    def gate(branch):
        return jax.nn.sigmoid(proj(GATE0 + branch * D_MODEL, D_MODEL))

    u = jax.nn.gelu(proj(0, W_MIX))
    v = jax.nn.gelu(proj(W_MIX, W_MIX))
    vc = v - jnp.mean(v, axis=-1, keepdims=True)
    v = vc * lax.rsqrt(jnp.mean(vc * vc, axis=-1, keepdims=True) + EPS) * sgug_ref[...]
    vb = v.astype(BF16)
    for ck in range(tm // CHUNK):
        r0 = ck * CHUNK
        for hh in range(SGU_HEADS):
            c0 = hh * SGU_HEAD_W
            mixed = jnp.dot(sguw_ref[hh], vb[r0:r0 + CHUNK, c0:c0 + SGU_HEAD_W],
                            preferred_element_type=F32) + sgub_ref[:, hh:hh + 1]
            ya_scr[r0:r0 + CHUNK, c0:c0 + SGU_HEAD_W] = (
                u[r0:r0 + CHUNK, c0:c0 + SGU_HEAD_W] * mixed).astype(BF16)
    merged = gate(0) * jnp.dot(ya_scr[...], wbr_ref[0], preferred_element_type=F32)

    m = proj(3 * W_MIX, W_MIX) * proj(4 * W_MIX, W_MIX)
    pos = lax.broadcasted_iota(jnp.int32, (tm, 1), 0) % conv_w
    prev = jnp.where(pos == 0, 0.0, pltpu.roll(m, 1, axis=0))
    nxt = jnp.where(pos == conv_w - 1, 0.0, pltpu.roll(m, tm - 1, axis=0))
    yb = proj(2 * W_MIX, W_MIX) * (
        convw_ref[0:1, :] * prev + convw_ref[1:2, :] * m + convw_ref[2:3, :] * nxt)
    merged = merged + gate(1) * _bdot(yb, wbr_ref[1])

    merged = merged + gate(2) * ycp_ref[...].astype(F32)

    h2 = x + mod_ref[0, 2:3, :] * _bdot(merged, wout_ref[...])
    h2_ref[...] = h2
    hn2 = _rms(h2, n2_ref[...]) * (1.0 + mod_ref[0, 4:5, :]) + mod_ref[0, 3:4, :]
    hn2_ref[...] = hn2.astype(BF16)
    logits_t = lax.dot_general(rwt_ref[...], hn2, _NT, precision=HIGHEST, preferred_element_type=F32)
    ea, eb, wa, wb = _route(logits_t, rb_ref)
    zero = jnp.zeros_like(wa)
    route_ref[...] = jnp.concatenate([ea, eb, wa, wb, zero, zero, zero, zero], axis=0)


def _mix(h, ycp, mod, lw, seq, conv_w, router_wt, router_b):
    t = h.shape[0]
    tm = TM_TOK
    per_seq = seq // tm
    tok = pl.BlockSpec((tm, D_MODEL), lambda i: (i, 0))
    in_specs = [
        tok,
        tok,
        pl.BlockSpec((1, N_MOD, D_MODEL), lambda i: (i // per_seq, 0, 0)),
        _const_spec((1, D_MODEL)),
        _const_spec((1, D_MODEL)),
        _const_spec((D_MODEL, D_IN)),
        _const_spec((1, W_MIX)),
        _const_spec((SGU_HEADS, CHUNK, CHUNK)),
        _const_spec((CHUNK, SGU_HEADS)),
        _const_spec((3, W_MIX)),
        _const_spec((N_BRANCH - 1, W_MIX, D_MODEL)),
        _const_spec((D_MODEL, D_MODEL)),
        _const_spec((N_EXPERTS, D_MODEL)),
        pl.BlockSpec(memory_space=pltpu.SMEM),
    ]
    out_specs = [tok, tok, pl.BlockSpec((8, tm), lambda i: (0, i))]
    out_shape = [
        jax.ShapeDtypeStruct((t, D_MODEL), F32),
        jax.ShapeDtypeStruct((t, D_MODEL), BF16),
        jax.ShapeDtypeStruct((8, t), F32),
    ]
    return pl.pallas_call(
        functools.partial(_mix_kernel, conv_w=conv_w),
        grid=(t // tm,),
        in_specs=in_specs,
        out_specs=out_specs,
        out_shape=out_shape,
        scratch_shapes=[pltpu.VMEM((tm, W_MIX), BF16)],
        compiler_params=_params("arbitrary"),
        name="mix",
    )(h, ycp, mod, lw["norm1_g"], lw["norm2_g"], lw["w_in"], lw["sgu_norm_g"], lw["sgu_w"],
      lw["sgu_b_t"], lw["conv_w"], lw["w_branch_ab"], lw["w_out"], router_wt, router_b)


def _moe_kernel(ea_ref, eb_ref, valid_ref, x_ref, w_ref, ga_ref, ua_ref, da_ref,
                gb_ref, ub_ref, db_ref, o_ref):
    i = pl.program_id(0)

    @pl.when(valid_ref[i] > 0)
    def _():
        x = x_ref[...]

        def expert(g_ref, u_ref, d_ref):
            g = jnp.dot(x, g_ref[0], preferred_element_type=F32)
            act = g * jax.nn.sigmoid(g) * jnp.dot(x, u_ref[0], preferred_element_type=F32)
            return _bdot(act, d_ref[0])

        o_ref[...] = (w_ref[:, 1:2] * expert(ga_ref, ua_ref, da_ref)
                      + w_ref[:, 2:3] * expert(gb_ref, ub_ref, db_ref))

    @pl.when(valid_ref[i] == 0)
    def _():
        o_ref[...] = jnp.zeros_like(o_ref)


def _moe(hn2, route, w_g, w_u, w_d):
    t = hn2.shape[0]
    tm = TM_MOE
    nt = t // tm + N_CLASSES
    tp = nt * tm
    ea = route[0].astype(jnp.int32)
    eb = route[1].astype(jnp.int32)
    lo = ea % EXPERTS_PER_GROUP
    hi = eb % EXPERTS_PER_GROUP
    cls = (ea // EXPERTS_PER_GROUP) * N_PAIRS + (lo * (7 - lo)) // 2 + (hi - lo - 1)
    onehot = (cls[:, None] == jnp.arange(N_CLASSES, dtype=jnp.int32)[None, :]).astype(jnp.int32)
    csum = jnp.cumsum(onehot, axis=0)
    rank = jnp.sum(csum * onehot, axis=1) - 1
    counts = csum[-1]
    padded = ((counts + tm - 1) // tm) * tm
    ends = jnp.cumsum(padded)
    offs = ends - padded
    pos = offs[cls] + rank
    starts = jnp.arange(nt, dtype=jnp.int32) * tm
    tile_cls = jnp.sum((starts[:, None] >= ends[None, :]).astype(jnp.int32), axis=1)
    valid = (tile_cls < N_CLASSES).astype(jnp.int32)
    tile_cls = jnp.minimum(tile_cls, N_CLASSES - 1)
    pair_lo = jnp.array([0, 0, 0, 1, 1, 2], jnp.int32)
    pair_hi = jnp.array([1, 2, 3, 2, 3, 3], jnp.int32)
    tile_ea = (tile_cls // N_PAIRS) * EXPERTS_PER_GROUP + pair_lo[tile_cls % N_PAIRS]
    tile_eb = (tile_cls // N_PAIRS) * EXPERTS_PER_GROUP + pair_hi[tile_cls % N_PAIRS]
    slot = jnp.stack([jnp.arange(t, dtype=F32), route[2], route[3], jnp.zeros((t,), F32)], axis=1)
    ws = jnp.zeros((tp, 4), F32).at[pos].set(slot)
    xs = jnp.take(hn2, ws[:, 0].astype(jnp.int32), axis=0)

    wspec_in = lambda sel: pl.BlockSpec((1, D_MODEL, D_FF), sel)
    wspec_out = lambda sel: pl.BlockSpec((1, D_FF, D_MODEL), sel)
    sel_a = lambda i, a, b, v: (a[i], 0, 0)
    sel_b = lambda i, a, b, v: (b[i], 0, 0)
    grid_spec = pltpu.PrefetchScalarGridSpec(
        num_scalar_prefetch=3,
        grid=(nt,),
        in_specs=[
            pl.BlockSpec((tm, D_MODEL), lambda i, a, b, v: (i, 0)),
            pl.BlockSpec((tm, 4), lambda i, a, b, v: (i, 0)),
            wspec_in(sel_a), wspec_in(sel_a), wspec_out(sel_a),
            wspec_in(sel_b), wspec_in(sel_b), wspec_out(sel_b),
        ],
        out_specs=pl.BlockSpec((tm, D_MODEL), lambda i, a, b, v: (i, 0)),
    )
    out = pl.pallas_call(
        _moe_kernel,
        grid_spec=grid_spec,
        out_shape=jax.ShapeDtypeStruct((tp, D_MODEL), F32),
        compiler_params=_params("arbitrary"),
        name="moe",
    )(tile_ea, tile_eb, valid, xs, ws, w_g, w_u, w_d, w_g, w_u, w_d)
    return jnp.take(out, pos, axis=0)


def _final_kernel(h_ref, moe_ref, mod_ref, g_ref, o_ref):
    x = h_ref[...] + mod_ref[0, 5:6, :] * moe_ref[...]
    o_ref[...] = _rms(x, g_ref[...])


def _final(h, moe, mod, g, seq):
    t = h.shape[0]
    tm = TM_TOK
    per_seq = seq // tm
    tok = pl.BlockSpec((tm, D_MODEL), lambda i: (i, 0))
    return pl.pallas_call(
        _final_kernel,
        grid=(t // tm,),
        in_specs=[tok, tok, pl.BlockSpec((1, N_MOD, D_MODEL), lambda i: (i // per_seq, 0, 0)),
                  _const_spec((1, D_MODEL))],
        out_specs=tok,
        out_shape=jax.ShapeDtypeStruct((t, D_MODEL), F32),
        compiler_params=_params("arbitrary"),
        name="final",
    )(h, moe, mod, g.reshape(1, D_MODEL))


def kernel(x, c, ctx, c_ctx, w_mod, b_mod, norm1_g, norm2_g, w_in, sgu_norm_g, sgu_w, sgu_b,
           conv_w, s5_lam_re, s5_lam_im, s5_log_dt, s5_b_re, s5_b_im, s5_c_re, s5_c_im, s5_d,
           glu_w, glu_b, w_branch, w_out, router_w, router_b, exp_w_gate, exp_w_up, exp_w_down,
           final_norm_g):
    nb, seq, _ = x.shape
    ctx_len = ctx.shape[1]
    h = x.reshape(nb * seq, D_MODEL)
    hc = ctx.reshape(nb * ctx_len, D_MODEL)

    mod_rows = 24
    cvec = jnp.zeros((mod_rows, D_MODEL), F32).at[:nb].set(c).at[nb].set(c_ctx)
    mods = _adaln(cvec, w_mod, b_mod).reshape(DEPTH, mod_rows, N_MOD, D_MODEL)
    router_wt = router_w.T

    moe = moe_c = None
    mod = mod_c = None
    for l in range(DEPTH):
        last = l == DEPTH - 1
        pmod, pmod_c = mod, mod_c
        mod = mods[l, :nb]
        mod_c = jnp.broadcast_to(mods[l, nb][None], (nb, N_MOD, D_MODEL))
        w_in_b = w_in[l].astype(BF16)
        w_s5t = w_in_b[:, S5_COL0:S5_COL0 + W_MIX].T
        lw = {
            "norm1_g": norm1_g[l].reshape(1, D_MODEL),
            "norm2_g": norm2_g[l].reshape(1, D_MODEL),
            "w_in": w_in_b,
            "sgu_norm_g": sgu_norm_g[l].reshape(1, W_MIX),
            "sgu_w": sgu_w[l].astype(BF16),
            "sgu_b_t": sgu_b[l].T,
            "conv_w": conv_w[l],
            "w_branch_ab": w_branch[l, :2].astype(BF16),
            "w_out": w_out[l].astype(BF16),
        }
        glu_wt = glu_w[l].T.astype(BF16)
        glu_bc = glu_b[l].reshape(W_MIX, 1)
        w_branch_c = w_branch[l, 2].astype(BF16)
        mats = _s5_mats(s5_lam_re[l], s5_lam_im[l], s5_log_dt[l], s5_b_re[l], s5_b_im[l],
                        s5_c_re[l], s5_c_im[l], s5_d[l])
        w_g = exp_w_gate[l].astype(BF16)
        w_u = exp_w_up[l].astype(BF16)
        w_d = exp_w_down[l].astype(BF16)

        hc, xg_c = _pre(hc, mod_c, norm1_g[l], w_s5t, ctx_len, moe=moe_c, pmod=pmod_c)
        y_c, fin_c = _s5(xg_c, mats, None, nb, want_y=not last)
        h, xg = _pre(h, mod, norm1_g[l], w_s5t, seq, moe=moe, pmod=pmod)
        y, _ = _s5(xg, mats, fin_c, nb, want_y=True)
        ycp = _s5out(y, glu_wt, glu_bc, w_branch_c, nb)
        h, hn2, route = _mix(h, ycp, mod, lw, seq, GRID_W, router_wt, router_b)
        moe = _moe(hn2, route, w_g, w_u, w_d)
        if not last:
            ycp_c = _s5out(y_c, glu_wt, glu_bc, w_branch_c, nb)
            hc, hn2_c, route_c = _mix(hc, ycp_c, mod_c, lw, ctx_len, ctx_len, router_wt, router_b)
            moe_c = _moe(hn2_c, route_c, w_g, w_u, w_d)
    out = _final(h, moe, mod, final_norm_g, seq)
    return out.reshape(nb, seq, D_MODEL)
```

```python
import functools

import jax
import jax.numpy as jnp
from jax import lax
from jax.experimental import pallas as pl
from jax.experimental.pallas import tpu as pltpu

F32 = jnp.float32
BF16 = jnp.bfloat16
HIGHEST = lax.Precision.HIGHEST

D_MODEL = 1024
DEPTH = 2
EPS = 1e-6
W_MIX = 512
N_BRANCH = 3
CHUNK = 128
SGU_HEADS = 4
SGU_HEAD_W = W_MIX // SGU_HEADS
GRID_W = 64
S5_H = 16
S5_G = W_MIX // S5_H
S5_P = 64
S5_LC = 16
S5_W = S5_LC * S5_H
S5_COL0 = 5 * W_MIX
GATE0 = 6 * W_MIX
D_IN = GATE0 + N_BRANCH * D_MODEL
N_EXPERTS = 16
EXPERTS_PER_GROUP = 4
N_GROUPS = 4
D_FF = 512
N_PAIRS = 6
N_CLASSES = N_GROUPS * N_PAIRS
N_MOD = 6
LANES = 128

TM_TOK = 512
TM_MOE = 256
PRE_I_PER_STEP = 4
VMEM_LIMIT = 56 * 1024 * 1024

_NT = (((1,), (1,)), ((), ()))
_TN = (((0,), (0,)), ((), ()))


def _rms(x, g):
    return x * lax.rsqrt(jnp.mean(x * x, axis=-1, keepdims=True) + EPS) * g


def _bdot(a, b):
    return jnp.dot(a.astype(BF16), b, preferred_element_type=F32)


def _const_spec(shape):
    nd = len(shape)
    return pl.BlockSpec(shape, lambda *_: (0,) * nd, pipeline_mode=pl.Buffered(1))


def _params(*sem):
    return pltpu.CompilerParams(dimension_semantics=sem, vmem_limit_bytes=VMEM_LIMIT)


def _adaln_kernel(c_ref, w_ref, b_ref, o_ref):
    c = c_ref[...]
    s = c * jax.nn.sigmoid(c)
    o_ref[0] = jnp.dot(s, w_ref[0], precision=HIGHEST, preferred_element_type=F32) + b_ref[0]


def _adaln(cvec, w_mod, b_mod):
    rows = cvec.shape[0]
    tn = 1536
    n_mod = N_MOD * D_MODEL
    return pl.pallas_call(
        _adaln_kernel,
        grid=(DEPTH, n_mod // tn),
        in_specs=[
            pl.BlockSpec((rows, D_MODEL), lambda l, j: (0, 0)),
            pl.BlockSpec((1, D_MODEL, tn), lambda l, j: (l, 0, j)),
            pl.BlockSpec((1, 1, tn), lambda l, j: (l, 0, j)),
        ],
        out_specs=pl.BlockSpec((1, rows, tn), lambda l, j: (l, 0, j)),
        out_shape=jax.ShapeDtypeStruct((DEPTH, rows, n_mod), F32),
        compiler_params=_params("arbitrary", "arbitrary"),
        name="adaln",
    )(cvec, w_mod, b_mod.reshape(DEPTH, 1, n_mod))


def _pre_kernel(*refs, residual):
    if residual:
        h_ref, moe_ref, pmod_ref, mod_ref, g_ref, wt_ref, hnew_ref, xo_ref = refs
    else:
        h_ref, mod_ref, g_ref, wt_ref, xo_ref = refs
    nbt, nc, _ = h_ref.shape
    for i in range(PRE_I_PER_STEP):
        cols = slice(i * D_MODEL, (i + 1) * D_MODEL)
        x = h_ref[:, :, cols]
        if residual:
            x = x + pmod_ref[:, 5:6, :] * moe_ref[:, :, cols].astype(F32)
            hnew_ref[:, :, cols] = x
        xn = _rms(x, g_ref[...]) * (1.0 + mod_ref[:, 1:2, :]) + mod_ref[:, 0:1, :]
        xb = xn.reshape(nbt * nc, D_MODEL).astype(BF16)
        r = lax.dot_general(wt_ref[...], xb, _NT, preferred_element_type=F32)
        xo_ref[:, i * S5_H:(i + 1) * S5_H, :] = r.astype(BF16).reshape(S5_G, S5_H, nbt * nc)


def _pre(h, mod, norm_g, w_s5t, seq, moe=None, pmod=None):
    t = h.shape[0]
    nb = t // seq
    nc = seq // S5_LC
    nbt = LANES // nc
    n_tiles = nb // nbt
    wide = PRE_I_PER_STEP * D_MODEL
    view = lambda a: a.reshape(nb, nc, S5_LC * D_MODEL)
    tok = pl.BlockSpec((nbt, nc, wide), lambda tb, ib: (tb, 0, ib))
    modspec = pl.BlockSpec((nbt, N_MOD, D_MODEL), lambda tb, ib: (tb, 0, 0))
    residual = moe is not None
    in_specs = [tok]
    args = [view(h)]
    if residual:
        in_specs += [tok, modspec]
        args += [view(moe), pmod]
    in_specs += [modspec, _const_spec((1, D_MODEL)), _const_spec((W_MIX, D_MODEL))]
    args += [mod, norm_g.reshape(1, D_MODEL), w_s5t]
    xo_spec = pl.BlockSpec((S5_G, PRE_I_PER_STEP * S5_H, LANES), lambda tb, ib: (0, ib, tb))
    xo_shape = jax.ShapeDtypeStruct((S5_G, S5_W, nb * nc), BF16)
    if residual:
        out_specs = [tok, xo_spec]
        out_shape = [jax.ShapeDtypeStruct((nb, nc, S5_LC * D_MODEL), F32), xo_shape]
    else:
        out_specs = xo_spec
        out_shape = xo_shape
    out = pl.pallas_call(
        functools.partial(_pre_kernel, residual=residual),
        grid=(n_tiles, S5_LC // PRE_I_PER_STEP),
        in_specs=in_specs,
        out_specs=out_specs,
        out_shape=out_shape,
        compiler_params=_params("arbitrary", "arbitrary"),
        name="pre_res" if residual else "pre",
    )(*args)
    if residual:
        return out[0].reshape(t, D_MODEL), out[1]
    return h, out


def _s5_mats(lam_re, lam_im, log_dt, b_re, b_im, c_re, c_im, dvec):
    dt = jnp.exp(log_dt)[..., None]
    mag = jnp.exp(lam_re * dt)
    ab_re = mag * jnp.cos(lam_im * dt)
    ab_im = mag * jnp.sin(lam_im * dt)
    den = lam_re * lam_re + lam_im * lam_im
    nr = ab_re - 1.0
    f_re = (nr * lam_re + ab_im * lam_im) / den
    f_im = (ab_im * lam_re - nr * lam_im) / den
    bb_re = f_re[..., None] * b_re - f_im[..., None] * b_im
    bb_im = f_re[..., None] * b_im + f_im[..., None] * b_re
    k = jnp.arange(S5_LC + 1, dtype=F32)[:, None, None, None]
    pmag = jnp.exp(lam_re * dt * k)
    pw_re = pmag * jnp.cos(lam_im * dt * k)
    pw_im = pmag * jnp.sin(lam_im * dt * k)
    cp_re = c_re[None] * pw_re[:, :, :, None, :] - c_im[None] * pw_im[:, :, :, None, :]
    cp_im = c_re[None] * pw_im[:, :, :, None, :] + c_im[None] * pw_re[:, :, :, None, :]
    kk = (jnp.einsum("kdghp,dgpj->kdghj", cp_re[:S5_LC], bb_re, precision=HIGHEST)
          - jnp.einsum("kdghp,dgpj->kdghj", cp_im[:S5_LC], bb_im, precision=HIGHEST))
    kf, kb = kk[:, 0], kk[:, 1]
    dmat = jnp.eye(S5_H, dtype=F32)[None] * dvec.reshape(S5_G, S5_H)[:, :, None]
    table = jnp.concatenate(
        [kf[:0:-1], (kf[0] + kb[0] + dmat)[None], kb[1:], jnp.zeros_like(kf[:1])], axis=0)
    table = jnp.transpose(table, (1, 2, 0, 3)).reshape(S5_G, S5_H, 2 * S5_W)
    tt = jnp.stack(
        [table[:, :, (S5_LC - 1 - j) * S5_H:(S5_LC - 1 - j) * S5_H + S5_W] for j in range(S5_LC)],
        axis=1).reshape(S5_G, S5_W, S5_W)

    rev = jnp.arange(S5_LC - 1, -1, -1)
    fwd = jnp.arange(S5_LC)

    def lanes4(f_re_part, b_re_part, f_im_part, b_im_part):
        return jnp.concatenate([f_re_part, b_re_part, f_im_part, b_im_part], axis=-1)

    def step_major(a):
        return jnp.transpose(a, (1, 0, 2))

    def chan_major(a):
        return jnp.transpose(a, (0, 2, 1))

    pr_f, pi_f = step_major(pw_re[rev, 0]), step_major(pw_im[rev, 0])
    pr_b, pi_b = step_major(pw_re[fwd, 1]), step_major(pw_im[fwd, 1])
    br_f, bi_f = chan_major(bb_re[0]), chan_major(bb_im[0])
    br_b, bi_b = chan_major(bb_re[1]), chan_major(bb_im[1])
    inj = (lanes4(pr_f, pr_b, pr_f, pr_b)[:, :, None, :] * lanes4(br_f, br_b, bi_f, bi_b)[:, None]
           + lanes4(-pi_f, -pi_b, pi_f, pi_b)[:, :, None, :] * lanes4(bi_f, bi_b, br_f, br_b)[:, None]
           ).reshape(S5_G, S5_W, 4 * S5_P)
    qr_f, qi_f = step_major(pw_re[fwd + 1, 0]), step_major(pw_im[fwd + 1, 0])
    qr_b, qi_b = step_major(pw_re[S5_LC - fwd, 1]), step_major(pw_im[S5_LC - fwd, 1])
    rt = (lanes4(qr_f, qr_b, -qi_f, -qi_b)[:, :, None, :]
          * lanes4(c_re[0], c_re[1], c_re[0], c_re[1])[:, None]
          + lanes4(-qi_f, -qi_b, -qr_f, -qr_b)[:, :, None, :]
          * lanes4(c_im[0], c_im[1], c_im[0], c_im[1])[:, None]).reshape(S5_G, S5_W, 4 * S5_P)
    a_re, a_im = pw_re[S5_LC], pw_im[S5_LC]
    pvec = jnp.concatenate([a_re[0], a_re[1]], axis=-1)
    qvec = jnp.concatenate([a_im[0], a_im[1]], axis=-1)
    pq = jnp.stack([pvec, qvec], axis=1)
    return tt.astype(BF16), inj.astype(BF16), rt.astype(BF16), pq


def _s5_kernel(*refs, n_chunks, n_batch, want_y):
    if want_y:
        x_ref, tt_ref, inj_ref, rt_ref, pq_ref, h0_ref, y_ref, fin_ref = refs[:8]
    else:
        x_ref, inj_ref, pq_ref, h0_ref, fin_ref = refs[:5]
    i_re_scr, i_im_scr, sf_re_scr, sf_im_scr, sb_re_scr, sb_im_scr = refs[-6:]
    half = 2 * S5_P
    x = x_ref[0]
    inj = lax.dot_general(x, inj_ref[0], _TN, preferred_element_type=F32)
    i_re_scr[...] = inj[:, 0:half]
    i_im_scr[...] = inj[:, half:]
    p = pq_ref[0, 0:1, :]
    q = pq_ref[0, 1:2, :]
    fwd_lanes = lax.broadcasted_iota(jnp.int32, (n_batch, half), 1) < S5_P
    h0 = h0_ref[0]
    s_re, s_im = h0[:, 0:half], h0[:, half:]
    for k in range(n_chunks):
        rows_f = pl.ds(k, n_batch, stride=n_chunks)
        rows_b = pl.ds(n_chunks - 1 - k, n_batch, stride=n_chunks)
        sf_re_scr[rows_f, :] = s_re
        sf_im_scr[rows_f, :] = s_im
        sb_re_scr[rows_b, :] = s_re
        sb_im_scr[rows_b, :] = s_im
        i_re = jnp.where(fwd_lanes, i_re_scr[rows_f, :], i_re_scr[rows_b, :])
        i_im = jnp.where(fwd_lanes, i_im_scr[rows_f, :], i_im_scr[rows_b, :])
        s_re, s_im = p * s_re - q * s_im + i_re, p * s_im + q * s_re + i_im
    fin_ref[0] = jnp.concatenate([s_re, s_im], axis=1)
    if want_y:
        fwd = lax.broadcasted_iota(jnp.int32, (1, half), 1) < S5_P
        s_prev = jnp.concatenate(
            [jnp.where(fwd, sf_re_scr[...], sb_re_scr[...]),
             jnp.where(fwd, sf_im_scr[...], sb_im_scr[...])], axis=1).astype(BF16)
        y = jnp.dot(tt_ref[0], x, preferred_element_type=F32)
        y = y + lax.dot_general(rt_ref[0], s_prev, _NT, preferred_element_type=F32)
        y_ref[0] = y


def _s5(xg, mats, h0, n_batch, want_y):
    tt, inj, rt, pq = mats
    n = xg.shape[2]
    nc = n // n_batch
    if h0 is None:
        h0 = jnp.zeros((S5_G, n_batch, S5_W), F32)
    grp = lambda shape: pl.BlockSpec((1,) + shape, lambda g: (g, 0, 0))
    sq = grp((S5_W, S5_W))
    if want_y:
        in_specs = [grp((S5_W, n)), sq, sq, sq]
        args = [xg, tt, inj, rt]
    else:
        in_specs = [grp((S5_W, n)), sq]
        args = [xg, inj]
    in_specs += [grp((2, 2 * S5_P)), grp((n_batch, S5_W))]
    args += [pq, h0]
    fin_spec = grp((n_batch, S5_W))
    fin_shape = jax.ShapeDtypeStruct((S5_G, n_batch, S5_W), F32)
    if want_y:
        out_specs = [grp((S5_W, n)), fin_spec]
        out_shape = [jax.ShapeDtypeStruct((S5_G, S5_W, n), F32), fin_shape]
    else:
        out_specs = fin_spec
        out_shape = fin_shape
    out = pl.pallas_call(
        functools.partial(_s5_kernel, n_chunks=nc, n_batch=n_batch, want_y=want_y),
        grid=(S5_G,),
        in_specs=in_specs,
        out_specs=out_specs,
        out_shape=out_shape,
        scratch_shapes=[pltpu.VMEM((n, 2 * S5_P), F32)] * 6,
        compiler_params=_params("arbitrary"),
        name="s5" if want_y else "s5_states",
    )(*args)
    if not want_y:
        return None, out
    return out[0], out[1]


def _s5out_kernel(y_ref, gwt_ref, gb_ref, wb_ref, o_ref):
    nbt, nc, _ = o_ref.shape
    for j in range(S5_LC):
        z = jax.nn.gelu(y_ref[:, j * S5_H:(j + 1) * S5_H, :].reshape(W_MIX, nbt * nc))
        gate = jnp.dot(gwt_ref[...], z.astype(BF16), preferred_element_type=F32) + gb_ref[...]
        yc = (z * jax.nn.sigmoid(gate)).astype(BF16)
        pr = lax.dot_general(yc, wb_ref[...], _TN, preferred_element_type=F32)
        o_ref[:, :, j * D_MODEL:(j + 1) * D_MODEL] = pr.reshape(nbt, nc, D_MODEL).astype(BF16)


def _s5out(y, glu_wt, glu_b, w_branch_c, n_batch):
    n = y.shape[2]
    nc = n // n_batch
    nbt = LANES // nc
    out = pl.pallas_call(
        _s5out_kernel,
        grid=(n_batch // nbt,),
        in_specs=[
            pl.BlockSpec((S5_G, S5_W, LANES), lambda tb: (0, 0, tb)),
            _const_spec((W_MIX, W_MIX)),
            _const_spec((W_MIX, 1)),
            _const_spec((W_MIX, D_MODEL)),
        ],
        out_specs=pl.BlockSpec((nbt, nc, S5_LC * D_MODEL), lambda tb: (tb, 0, 0)),
        out_shape=jax.ShapeDtypeStruct((n_batch, nc, S5_LC * D_MODEL), BF16),
        compiler_params=_params("arbitrary"),
        name="s5out",
    )(y, glu_wt, glu_b, w_branch_c)
    return out.reshape(n_batch * nc * S5_LC, D_MODEL)


def _route(logits_t, rb_ref):
    s = [jax.nn.sigmoid(logits_t[e:e + 1, :]) for e in range(N_EXPERTS)]
    bz = [s[e] + rb_ref[e] for e in range(N_EXPERTS)]
    gs = []
    for g in range(N_GROUPS):
        v = bz[g * EXPERTS_PER_GROUP:(g + 1) * EXPERTS_PER_GROUP]
        best = None
        for i in range(EXPERTS_PER_GROUP):
            for j in range(i + 1, EXPERTS_PER_GROUP):
                pair = v[i] + v[j]
                best = pair if best is None else jnp.maximum(best, pair)
        gs.append(best)
    gbest = gs[0]
    gi = jnp.zeros_like(gbest, dtype=jnp.int32)
    for g in range(1, N_GROUPS):
        upd = gs[g] > gbest
        gi = jnp.where(upd, g, gi)
        gbest = jnp.where(upd, gs[g], gbest)

    def pick(vals, k):
        out = vals[k]
        for g in range(1, N_GROUPS):
            out = jnp.where(gi == g, vals[g * EXPERTS_PER_GROUP + k], out)
        return out

    v = [pick(bz, k) for k in range(EXPERTS_PER_GROUP)]
    sc = [pick(s, k) for k in range(EXPERTS_PER_GROUP)]

    def argmax4(vals):
        best, idx = vals[0], jnp.zeros_like(gi)
        for k in range(1, EXPERTS_PER_GROUP):
            upd = vals[k] > best
            idx = jnp.where(upd, k, idx)
            best = jnp.where(upd, vals[k], best)
        return idx

    i1 = argmax4(v)
    v2 = [jnp.where(i1 == k, -jnp.inf, v[k]) for k in range(EXPERTS_PER_GROUP)]
    i2 = argmax4(v2)

    def sel(vals, idx):
        out = vals[0]
        for k in range(1, EXPERTS_PER_GROUP):
            out = jnp.where(idx == k, vals[k], out)
        return out

    w1 = sel(sc, i1)
    w2 = sel(sc, i2)
    tot = w1 + w2
    w1 = w1 / tot
    w2 = w2 / tot
    first_low = i1 < i2
    lo = jnp.where(first_low, i1, i2)
    hi = jnp.where(first_low, i2, i1)
    wa = jnp.where(first_low, w1, w2)
    wb = jnp.where(first_low, w2, w1)
    base = gi * EXPERTS_PER_GROUP
    return (base + lo).astype(F32), (base + hi).astype(F32), wa, wb


def _mix_kernel(h_ref, ycp_ref, mod_ref, n1_ref, n2_ref, win_ref, sgug_ref, sguw_ref, sgub_ref,
                convw_ref, wbr_ref, wout_ref, rwt_ref, rb_ref,
                h2_ref, hn2_ref, route_ref, ya_scr, *, conv_w):
    tm = h_ref.shape[0]
    x = h_ref[...]
    xn = _rms(x, n1_ref[...]) * (1.0 + mod_ref[0, 1:2, :]) + mod_ref[0, 0:1, :]
    xb = xn.astype(BF16)

    def proj(c0, width):
        return jnp.dot(xb, win_ref[:, c0:c0 + width], preferred_element_type=F32)

    def gate(branch):
        return jax.nn.sigmoid(proj(GATE0 + branch * D_MODEL, D_MODEL))

    u = jax.nn.gelu(proj(0, W_MIX))
    v = jax.nn.gelu(proj(W_MIX, W_MIX))
    vc = v - jnp.mean(v, axis=-1, keepdims=True)
    v = vc * lax.rsqrt(jnp.mean(vc * vc, axis=-1, keepdims=True) + EPS) * sgug_ref[...]
    vb = v.astype(BF16)
    for ck in range(tm // CHUNK):
        r0 = ck * CHUNK
        for hh in range(SGU_HEADS):
            c0 = hh * SGU_HEAD_W
            mixed = jnp.dot(sguw_ref[hh], vb[r0:r0 + CHUNK, c0:c0 + SGU_HEAD_W],
                            preferred_element_type=F32) + sgub_ref[:, hh:hh + 1]
            ya_scr[r0:r0 + CHUNK, c0:c0 + SGU_HEAD_W] = (
                u[r0:r0 + CHUNK, c0:c0 + SGU_HEAD_W] * mixed).astype(BF16)
    merged = gate(0) * jnp.dot(ya_scr[...], wbr_ref[0], preferred_element_type=F32)

    m = proj(3 * W_MIX, W_MIX) * proj(4 * W_MIX, W_MIX)
    pos = lax.broadcasted_iota(jnp.int32, (tm, 1), 0) % conv_w
    prev = jnp.where(pos == 0, 0.0, pltpu.roll(m, 1, axis=0))
    nxt = jnp.where(pos == conv_w - 1, 0.0, pltpu.roll(m, tm - 1, axis=0))
    yb = proj(2 * W_MIX, W_MIX) * (
        convw_ref[0:1, :] * prev + convw_ref[1:2, :] * m + convw_ref[2:3, :] * nxt)
    merged = merged + gate(1) * _bdot(yb, wbr_ref[1])

    merged = merged + gate(2) * ycp_ref[...].astype(F32)

    h2 = x + mod_ref[0, 2:3, :] * _bdot(merged, wout_ref[...])
    h2_ref[...] = h2
    hn2 = _rms(h2, n2_ref[...]) * (1.0 + mod_ref[0, 4:5, :]) + mod_ref[0, 3:4, :]
    hn2_ref[...] = hn2.astype(BF16)
    logits_t = lax.dot_general(rwt_ref[...], hn2, _NT, precision=HIGHEST, preferred_element_type=F32)
    ea, eb, wa, wb = _route(logits_t, rb_ref)
    zero = jnp.zeros_like(wa)
    route_ref[...] = jnp.concatenate([ea, eb, wa, wb, zero, zero, zero, zero], axis=0)


def _mix(h, ycp, mod, lw, seq, conv_w, router_wt, router_b):
    t = h.shape[0]
    tm = min(TM_TOK, seq)
    per_seq = seq // tm
    tok = pl.BlockSpec((tm, D_MODEL), lambda i: (i, 0))
    in_specs = [
        tok,
        tok,
        pl.BlockSpec((1, N_MOD, D_MODEL), lambda i: (i // per_seq, 0, 0)),
        _const_spec((1, D_MODEL)),
        _const_spec((1, D_MODEL)),
        _const_spec((D_MODEL, D_IN)),
        _const_spec((1, W_MIX)),
        _const_spec((SGU_HEADS, CHUNK, CHUNK)),
        _const_spec((CHUNK, SGU_HEADS)),
        _const_spec((3, W_MIX)),
        _const_spec((N_BRANCH - 1, W_MIX, D_MODEL)),
        _const_spec((D_MODEL, D_MODEL)),
        _const_spec((N_EXPERTS, D_MODEL)),
        pl.BlockSpec(memory_space=pltpu.SMEM),
    ]
    out_specs = [tok, tok, pl.BlockSpec((8, tm), lambda i: (0, i))]
    out_shape = [
        jax.ShapeDtypeStruct((t, D_MODEL), F32),
        jax.ShapeDtypeStruct((t, D_MODEL), BF16),
        jax.ShapeDtypeStruct((8, t), F32),
    ]
    return pl.pallas_call(
        functools.partial(_mix_kernel, conv_w=conv_w),
        grid=(t // tm,),
        in_specs=in_specs,
        out_specs=out_specs,
        out_shape=out_shape,
        scratch_shapes=[pltpu.VMEM((tm, W_MIX), BF16)],
        compiler_params=_params("arbitrary"),
        name="mix",
    )(h, ycp, mod, lw["norm1_g"], lw["norm2_g"], lw["w_in"], lw["sgu_norm_g"], lw["sgu_w"],
      lw["sgu_b_t"], lw["conv_w"], lw["w_branch_ab"], lw["w_out"], router_wt, router_b)


def _moe_kernel(ea_ref, eb_ref, valid_ref, x_ref, w_ref, ga_ref, ua_ref, da_ref,
                gb_ref, ub_ref, db_ref, o_ref):
    i = pl.program_id(0)

    @pl.when(valid_ref[i] > 0)
    def _():
        x = x_ref[...]

        def expert(g_ref, u_ref, d_ref):
            g = jnp.dot(x, g_ref[0], preferred_element_type=F32)
            act = g * jax.nn.sigmoid(g) * jnp.dot(x, u_ref[0], preferred_element_type=F32)
            return _bdot(act, d_ref[0])

        o_ref[...] = (w_ref[:, 1:2] * expert(ga_ref, ua_ref, da_ref)
                      + w_ref[:, 2:3] * expert(gb_ref, ub_ref, db_ref)).astype(BF16)

    @pl.when(valid_ref[i] == 0)
    def _():
        o_ref[...] = jnp.zeros_like(o_ref)


def _moe(hn2, route, w_g, w_u, w_d):
    t = hn2.shape[0]
    tm = TM_MOE
    nt = t // tm + N_CLASSES
    tp = nt * tm
    ea = route[0].astype(jnp.int32)
    eb = route[1].astype(jnp.int32)
    lo = ea % EXPERTS_PER_GROUP
    hi = eb % EXPERTS_PER_GROUP
    cls = (ea // EXPERTS_PER_GROUP) * N_PAIRS + (lo * (7 - lo)) // 2 + (hi - lo - 1)
    onehot = (cls[:, None] == jnp.arange(N_CLASSES, dtype=jnp.int32)[None, :]).astype(jnp.int32)
    csum = jnp.cumsum(onehot, axis=0)
    rank = jnp.sum(csum * onehot, axis=1) - 1
    counts = csum[-1]
    padded = ((counts + tm - 1) // tm) * tm
    ends = jnp.cumsum(padded)
    offs = ends - padded
    pos = offs[cls] + rank
    starts = jnp.arange(nt, dtype=jnp.int32) * tm
    tile_cls = jnp.sum((starts[:, None] >= ends[None, :]).astype(jnp.int32), axis=1)
    valid = (tile_cls < N_CLASSES).astype(jnp.int32)
    tile_cls = jnp.minimum(tile_cls, N_CLASSES - 1)
    pair_lo = jnp.array([0, 0, 0, 1, 1, 2], jnp.int32)
    pair_hi = jnp.array([1, 2, 3, 2, 3, 3], jnp.int32)
    tile_ea = (tile_cls // N_PAIRS) * EXPERTS_PER_GROUP + pair_lo[tile_cls % N_PAIRS]
    tile_eb = (tile_cls // N_PAIRS) * EXPERTS_PER_GROUP + pair_hi[tile_cls % N_PAIRS]
    slot = jnp.stack([jnp.arange(t, dtype=F32), route[2], route[3], jnp.zeros((t,), F32)], axis=1)
    ws = jnp.zeros((tp, 4), F32).at[pos].set(slot)
    xs = jnp.take(hn2, ws[:, 0].astype(jnp.int32), axis=0)

    wspec_in = lambda sel: pl.BlockSpec((1, D_MODEL, D_FF), sel)
    wspec_out = lambda sel: pl.BlockSpec((1, D_FF, D_MODEL), sel)
    sel_a = lambda i, a, b, v: (a[i], 0, 0)
    sel_b = lambda i, a, b, v: (b[i], 0, 0)
    grid_spec = pltpu.PrefetchScalarGridSpec(
        num_scalar_prefetch=3,
        grid=(nt,),
        in_specs=[
            pl.BlockSpec((tm, D_MODEL), lambda i, a, b, v: (i, 0)),
            pl.BlockSpec((tm, 4), lambda i, a, b, v: (i, 0)),
            wspec_in(sel_a), wspec_in(sel_a), wspec_out(sel_a),
            wspec_in(sel_b), wspec_in(sel_b), wspec_out(sel_b),
        ],
        out_specs=pl.BlockSpec((tm, D_MODEL), lambda i, a, b, v: (i, 0)),
    )
    out = pl.pallas_call(
        _moe_kernel,
        grid_spec=grid_spec,
        out_shape=jax.ShapeDtypeStruct((tp, D_MODEL), BF16),
        compiler_params=_params("arbitrary"),
        name="moe",
    )(tile_ea, tile_eb, valid, xs, ws, w_g, w_u, w_d, w_g, w_u, w_d)
    return jnp.take(out, pos, axis=0)


def _final_kernel(h_ref, moe_ref, mod_ref, g_ref, o_ref):
    x = h_ref[...] + mod_ref[0, 5:6, :] * moe_ref[...].astype(F32)
    o_ref[...] = _rms(x, g_ref[...])


def _final(h, moe, mod, g, seq):
    t = h.shape[0]
    tm = TM_TOK
    per_seq = seq // tm
    tok = pl.BlockSpec((tm, D_MODEL), lambda i: (i, 0))
    return pl.pallas_call(
        _final_kernel,
        grid=(t // tm,),
        in_specs=[tok, tok, pl.BlockSpec((1, N_MOD, D_MODEL), lambda i: (i // per_seq, 0, 0)),
                  _const_spec((1, D_MODEL))],
        out_specs=tok,
        out_shape=jax.ShapeDtypeStruct((t, D_MODEL), F32),
        compiler_params=_params("arbitrary"),
        name="final",
    )(h, moe, mod, g.reshape(1, D_MODEL))


def kernel(x, c, ctx, c_ctx, w_mod, b_mod, norm1_g, norm2_g, w_in, sgu_norm_g, sgu_w, sgu_b,
           conv_w, s5_lam_re, s5_lam_im, s5_log_dt, s5_b_re, s5_b_im, s5_c_re, s5_c_im, s5_d,
           glu_w, glu_b, w_branch, w_out, router_w, router_b, exp_w_gate, exp_w_up, exp_w_down,
           final_norm_g):
    nb, seq, _ = x.shape
    ctx_len = ctx.shape[1]
    h = x.reshape(nb * seq, D_MODEL)
    hc = ctx.reshape(nb * ctx_len, D_MODEL)

    mod_rows = 24
    cvec = jnp.zeros((mod_rows, D_MODEL), F32).at[:nb].set(c).at[nb].set(c_ctx)
    mods = _adaln(cvec, w_mod, b_mod).reshape(DEPTH, mod_rows, N_MOD, D_MODEL)
    router_wt = router_w.T

    all_mats = jax.vmap(_s5_mats)(s5_lam_re, s5_lam_im, s5_log_dt, s5_b_re, s5_b_im,
                                  s5_c_re, s5_c_im, s5_d)
    moe = moe_c = None
    mod = mod_c = None
    for l in range(DEPTH):
        last = l == DEPTH - 1
        pmod, pmod_c = mod, mod_c
        mod = mods[l, :nb]
        mod_c = jnp.broadcast_to(mods[l, nb][None], (nb, N_MOD, D_MODEL))
        w_in_b = w_in[l].astype(BF16)
        w_s5t = w_in_b[:, S5_COL0:S5_COL0 + W_MIX].T
        lw = {
            "norm1_g": norm1_g[l].reshape(1, D_MODEL),
            "norm2_g": norm2_g[l].reshape(1, D_MODEL),
            "w_in": w_in_b,
            "sgu_norm_g": sgu_norm_g[l].reshape(1, W_MIX),
            "sgu_w": sgu_w[l].astype(BF16),
            "sgu_b_t": sgu_b[l].T,
            "conv_w": conv_w[l],
            "w_branch_ab": w_branch[l, :2].astype(BF16),
            "w_out": w_out[l].astype(BF16),
        }
        glu_wt = glu_w[l].T.astype(BF16)
        glu_bc = glu_b[l].reshape(W_MIX, 1)
        w_branch_c = w_branch[l, 2].astype(BF16)
        mats = tuple(m[l] for m in all_mats)
        w_g = exp_w_gate[l].astype(BF16)
        w_u = exp_w_up[l].astype(BF16)
        w_d = exp_w_down[l].astype(BF16)

        hc, xg_c = _pre(hc, mod_c, norm1_g[l], w_s5t, ctx_len, moe=moe_c, pmod=pmod_c)
        y_c, fin_c = _s5(xg_c, mats, None, nb, want_y=not last)
        h, xg = _pre(h, mod, norm1_g[l], w_s5t, seq, moe=moe, pmod=pmod)
        y, _ = _s5(xg, mats, fin_c, nb, want_y=True)
        ycp = _s5out(y, glu_wt, glu_bc, w_branch_c, nb)
        h, hn2, route = _mix(h, ycp, mod, lw, seq, GRID_W, router_wt, router_b)
        if last:
            moe = _moe(hn2, route, w_g, w_u, w_d)
        else:
            ycp_c = _s5out(y_c, glu_wt, glu_bc, w_branch_c, nb)
            hc, hn2_c, route_c = _mix(hc, ycp_c, mod_c, lw, ctx_len, ctx_len, router_wt, router_b)
            moe_all = _moe(jnp.concatenate([hn2, hn2_c], axis=0),
                           jnp.concatenate([route, route_c], axis=1), w_g, w_u, w_d)
            moe, moe_c = moe_all[:nb * seq], moe_all[nb * seq:]
    out = _final(h, moe, mod, final_norm_g, seq)
    return out.reshape(nb, seq, D_MODEL)
```

```python
import functools

import jax
import jax.numpy as jnp
from jax import lax
from jax.experimental import pallas as pl
from jax.experimental.pallas import tpu as pltpu

F32 = jnp.float32
BF16 = jnp.bfloat16
HIGHEST = lax.Precision.HIGHEST

D_MODEL = 1024
DEPTH = 2
EPS = 1e-6
W_MIX = 512
N_BRANCH = 3
CHUNK = 128
SGU_HEADS = 4
SGU_HEAD_W = W_MIX // SGU_HEADS
GRID_W = 64
S5_H = 16
S5_G = W_MIX // S5_H
S5_P = 64
S5_LC = 16
S5_W = S5_LC * S5_H
S5_COL0 = 5 * W_MIX
GATE0 = 6 * W_MIX
D_IN = GATE0 + N_BRANCH * D_MODEL
N_EXPERTS = 16
EXPERTS_PER_GROUP = 4
N_GROUPS = 4
D_FF = 512
N_PAIRS = 6
N_CLASSES = N_GROUPS * N_PAIRS
N_MOD = 6
LANES = 128

TM_TOK = 512
TM_MOE = 256
PRE_B_PER_DOT = 4
VMEM_LIMIT = 56 * 1024 * 1024

_NT = (((1,), (1,)), ((), ()))
_TN = (((0,), (0,)), ((), ()))


def _rms(x, g):
    return x * lax.rsqrt(jnp.mean(x * x, axis=-1, keepdims=True) + EPS) * g


def _bdot(a, b):
    return jnp.dot(a.astype(BF16), b, preferred_element_type=F32)


def _const_spec(shape):
    nd = len(shape)
    return pl.BlockSpec(shape, lambda *_: (0,) * nd, pipeline_mode=pl.Buffered(1))


def _params(*sem):
    return pltpu.CompilerParams(dimension_semantics=sem, vmem_limit_bytes=VMEM_LIMIT)


def _adaln_kernel(c_ref, w_ref, b_ref, o_ref):
    c = c_ref[...]
    s = c * jax.nn.sigmoid(c)
    o_ref[0] = jnp.dot(s, w_ref[0], precision=HIGHEST, preferred_element_type=F32) + b_ref[0]


def _adaln(cvec, w_mod, b_mod):
    rows = cvec.shape[0]
    tn = 1536
    n_mod = N_MOD * D_MODEL
    return pl.pallas_call(
        _adaln_kernel,
        grid=(DEPTH, n_mod // tn),
        in_specs=[
            pl.BlockSpec((rows, D_MODEL), lambda l, j: (0, 0)),
            pl.BlockSpec((1, D_MODEL, tn), lambda l, j: (l, 0, j)),
            pl.BlockSpec((1, 1, tn), lambda l, j: (l, 0, j)),
        ],
        out_specs=pl.BlockSpec((1, rows, tn), lambda l, j: (l, 0, j)),
        out_shape=jax.ShapeDtypeStruct((DEPTH, rows, n_mod), F32),
        compiler_params=_params("arbitrary", "arbitrary"),
        name="adaln",
    )(cvec, w_mod, b_mod.reshape(DEPTH, 1, n_mod))


def _pre_kernel(*refs, residual):
    if residual:
        h_ref, moe_ref, pmod_ref, mod_ref, g_ref, w_ref, hnew_ref, xo_ref = refs[:8]
    else:
        h_ref, mod_ref, g_ref, w_ref, xo_ref = refs[:5]
    cu_scr = refs[-W_MIX // LANES:]
    nb, tt, _ = h_ref.shape
    for b0 in range(0, nb, PRE_B_PER_DOT):
        bs = slice(b0, b0 + PRE_B_PER_DOT)
        x = h_ref[bs]
        if residual:
            x = x + pmod_ref[bs, 5:6, :] * moe_ref[bs].astype(F32)
            hnew_ref[bs] = x
        xn = _rms(x, g_ref[...]) * (1.0 + mod_ref[bs, 1:2, :]) + mod_ref[bs, 0:1, :]
        cu = _bdot(xn.reshape(PRE_B_PER_DOT * tt, D_MODEL), w_ref[...])
        for lb, scr in enumerate(cu_scr):
            scr[b0 * tt:(b0 + PRE_B_PER_DOT) * tt, :] = cu[:, lb * LANES:(lb + 1) * LANES]
    groups_per_block = LANES // S5_H
    for i in range(S5_LC):
        for lb, scr in enumerate(cu_scr):
            piece = jnp.concatenate(
                [scr[pl.ds(c * S5_LC + i, nb, stride=tt), :] for c in range(tt // S5_LC)], axis=0)
            xo_ref[lb * groups_per_block:(lb + 1) * groups_per_block, i * S5_H:(i + 1) * S5_H, :] = (
                piece.T.astype(BF16).reshape(groups_per_block, S5_H, LANES))


def _pre(h, mod, norm_g, w_s5, seq, moe=None, pmod=None):
    t = h.shape[0]
    nb = t // seq
    tt = (LANES // nb) * S5_LC
    view = lambda a: a.reshape(nb, seq, D_MODEL)
    tok = pl.BlockSpec((nb, tt, D_MODEL), lambda k: (0, k, 0))
    modspec = _const_spec((nb, N_MOD, D_MODEL))
    residual = moe is not None
    in_specs = [tok]
    args = [view(h)]
    if residual:
        in_specs += [tok, modspec]
        args += [view(moe), pmod]
    in_specs += [modspec, _const_spec((1, D_MODEL)), _const_spec((D_MODEL, W_MIX))]
    args += [mod, norm_g.reshape(1, D_MODEL), w_s5]
    xo_spec = pl.BlockSpec((S5_G, S5_W, LANES), lambda k: (0, 0, k))
    xo_shape = jax.ShapeDtypeStruct((S5_G, S5_W, t // S5_LC), BF16)
    if residual:
        out_specs = [tok, xo_spec]
        out_shape = [jax.ShapeDtypeStruct((nb, seq, D_MODEL), F32), xo_shape]
    else:
        out_specs = xo_spec
        out_shape = xo_shape
    out = pl.pallas_call(
        functools.partial(_pre_kernel, residual=residual),
        grid=(seq // tt,),
        in_specs=in_specs,
        out_specs=out_specs,
        out_shape=out_shape,
        scratch_shapes=[pltpu.VMEM((nb * tt, LANES), F32)] * (W_MIX // LANES),
        compiler_params=_params("arbitrary"),
        name="pre_res" if residual else "pre",
    )(*args)
    if residual:
        return out[0].reshape(t, D_MODEL), out[1]
    return h, out


def _s5_mats(lam_re, lam_im, log_dt, b_re, b_im, c_re, c_im, dvec):
    dt = jnp.exp(log_dt)[..., None]
    mag = jnp.exp(lam_re * dt)
    ab_re = mag * jnp.cos(lam_im * dt)
    ab_im = mag * jnp.sin(lam_im * dt)
    den = lam_re * lam_re + lam_im * lam_im
    nr = ab_re - 1.0
    f_re = (nr * lam_re + ab_im * lam_im) / den
    f_im = (ab_im * lam_re - nr * lam_im) / den
    bb_re = f_re[..., None] * b_re - f_im[..., None] * b_im
    bb_im = f_re[..., None] * b_im + f_im[..., None] * b_re
    k = jnp.arange(S5_LC + 1, dtype=F32)[:, None, None, None]
    pmag = jnp.exp(lam_re * dt * k)
    pw_re = pmag * jnp.cos(lam_im * dt * k)
    pw_im = pmag * jnp.sin(lam_im * dt * k)
    cp_re = c_re[None] * pw_re[:, :, :, None, :] - c_im[None] * pw_im[:, :, :, None, :]
    cp_im = c_re[None] * pw_im[:, :, :, None, :] + c_im[None] * pw_re[:, :, :, None, :]
    kk = (jnp.einsum("kdghp,dgpj->kdghj", cp_re[:S5_LC], bb_re, precision=HIGHEST)
          - jnp.einsum("kdghp,dgpj->kdghj", cp_im[:S5_LC], bb_im, precision=HIGHEST))
    kf, kb = kk[:, 0], kk[:, 1]
    dmat = jnp.eye(S5_H, dtype=F32)[None] * dvec.reshape(S5_G, S5_H)[:, :, None]
    table = jnp.concatenate(
        [kf[:0:-1], (kf[0] + kb[0] + dmat)[None], kb[1:], jnp.zeros_like(kf[:1])], axis=0)
    table = jnp.transpose(table, (1, 2, 0, 3)).reshape(S5_G, S5_H, 2 * S5_W)
    tt = jnp.stack(
        [table[:, :, (S5_LC - 1 - j) * S5_H:(S5_LC - 1 - j) * S5_H + S5_W] for j in range(S5_LC)],
        axis=1).reshape(S5_G, S5_W, S5_W)

    rev = jnp.arange(S5_LC - 1, -1, -1)
    fwd = jnp.arange(S5_LC)

    def lanes4(f_re_part, b_re_part, f_im_part, b_im_part):
        return jnp.concatenate([f_re_part, b_re_part, f_im_part, b_im_part], axis=-1)

    def step_major(a):
        return jnp.transpose(a, (1, 0, 2))

    def chan_major(a):
        return jnp.transpose(a, (0, 2, 1))

    pr_f, pi_f = step_major(pw_re[rev, 0]), step_major(pw_im[rev, 0])
    pr_b, pi_b = step_major(pw_re[fwd, 1]), step_major(pw_im[fwd, 1])
    br_f, bi_f = chan_major(bb_re[0]), chan_major(bb_im[0])
    br_b, bi_b = chan_major(bb_re[1]), chan_major(bb_im[1])
    inj = (lanes4(pr_f, pr_b, pr_f, pr_b)[:, :, None, :] * lanes4(br_f, br_b, bi_f, bi_b)[:, None]
           + lanes4(-pi_f, -pi_b, pi_f, pi_b)[:, :, None, :] * lanes4(bi_f, bi_b, br_f, br_b)[:, None]
           ).reshape(S5_G, S5_W, 4 * S5_P)
    qr_f, qi_f = step_major(pw_re[fwd + 1, 0]), step_major(pw_im[fwd + 1, 0])
    qr_b, qi_b = step_major(pw_re[S5_LC - fwd, 1]), step_major(pw_im[S5_LC - fwd, 1])
    rt = (lanes4(qr_f, qr_b, -qi_f, -qi_b)[:, :, None, :]
          * lanes4(c_re[0], c_re[1], c_re[0], c_re[1])[:, None]
          + lanes4(-qi_f, -qi_b, -qr_f, -qr_b)[:, :, None, :]
          * lanes4(c_im[0], c_im[1], c_im[0], c_im[1])[:, None]).reshape(S5_G, S5_W, 4 * S5_P)
    a_re, a_im = pw_re[S5_LC], pw_im[S5_LC]
    pvec = jnp.concatenate([a_re[0], a_re[1]], axis=-1)
    qvec = jnp.concatenate([a_im[0], a_im[1]], axis=-1)
    pq = jnp.stack([pvec, qvec], axis=1)
    return tt.astype(BF16), inj.astype(BF16), rt.astype(BF16), pq


def _s5_kernel(*refs, n_chunks, n_batch, want_y):
    if want_y:
        x_ref, tt_ref, inj_ref, rt_ref, pq_ref, h0_ref, y_ref, fin_ref = refs[:8]
    else:
        x_ref, inj_ref, pq_ref, h0_ref, fin_ref = refs[:5]
    i_re_scr, i_im_scr, sf_re_scr, sf_im_scr, sb_re_scr, sb_im_scr = refs[-6:]
    half = 2 * S5_P
    x = x_ref[0]
    inj = lax.dot_general(x, inj_ref[0], _TN, preferred_element_type=F32)
    i_re_scr[...] = inj[:, 0:half]
    i_im_scr[...] = inj[:, half:]
    p = pq_ref[0, 0:1, :]
    q = pq_ref[0, 1:2, :]
    fwd_lanes = lax.broadcasted_iota(jnp.int32, (n_batch, half), 1) < S5_P
    h0 = h0_ref[0]
    s_re, s_im = h0[:, 0:half], h0[:, half:]
    for k in range(n_chunks):
        rows_f = slice(k * n_batch, (k + 1) * n_batch)
        rows_b = slice((n_chunks - 1 - k) * n_batch, (n_chunks - k) * n_batch)
        sf_re_scr[rows_f, :] = s_re
        sf_im_scr[rows_f, :] = s_im
        sb_re_scr[rows_b, :] = s_re
        sb_im_scr[rows_b, :] = s_im
        i_re = jnp.where(fwd_lanes, i_re_scr[rows_f, :], i_re_scr[rows_b, :])
        i_im = jnp.where(fwd_lanes, i_im_scr[rows_f, :], i_im_scr[rows_b, :])
        s_re, s_im = p * s_re - q * s_im + i_re, p * s_im + q * s_re + i_im
    fin_ref[0] = jnp.concatenate([s_re, s_im], axis=1)
    if want_y:
        fwd = lax.broadcasted_iota(jnp.int32, (1, half), 1) < S5_P
        s_prev = jnp.concatenate(
            [jnp.where(fwd, sf_re_scr[...], sb_re_scr[...]),
             jnp.where(fwd, sf_im_scr[...], sb_im_scr[...])], axis=1).astype(BF16)
        y = jnp.dot(tt_ref[0], x, preferred_element_type=F32)
        y = y + lax.dot_general(rt_ref[0], s_prev, _NT, preferred_element_type=F32)
        y_ref[0] = y


def _s5(xg, mats, h0, n_batch, want_y):
    tt, inj, rt, pq = mats
    n = xg.shape[2]
    nc = n // n_batch
    if h0 is None:
        h0 = jnp.zeros((S5_G, n_batch, S5_W), F32)
    grp = lambda shape: pl.BlockSpec((1,) + shape, lambda g: (g, 0, 0))
    sq = grp((S5_W, S5_W))
    if want_y:
        in_specs = [grp((S5_W, n)), sq, sq, sq]
        args = [xg, tt, inj, rt]
    else:
        in_specs = [grp((S5_W, n)), sq]
        args = [xg, inj]
    in_specs += [grp((2, 2 * S5_P)), grp((n_batch, S5_W))]
    args += [pq, h0]
    fin_spec = grp((n_batch, S5_W))
    fin_shape = jax.ShapeDtypeStruct((S5_G, n_batch, S5_W), F32)
    if want_y:
        out_specs = [grp((S5_W, n)), fin_spec]
        out_shape = [jax.ShapeDtypeStruct((S5_G, S5_W, n), F32), fin_shape]
    else:
        out_specs = fin_spec
        out_shape = fin_shape
    out = pl.pallas_call(
        functools.partial(_s5_kernel, n_chunks=nc, n_batch=n_batch, want_y=want_y),
        grid=(S5_G,),
        in_specs=in_specs,
        out_specs=out_specs,
        out_shape=out_shape,
        scratch_shapes=[pltpu.VMEM((n, 2 * S5_P), F32)] * 6,
        compiler_params=_params("arbitrary"),
        name="s5" if want_y else "s5_states",
    )(*args)
    if not want_y:
        return None, out
    return out[0], out[1]


def _s5out_kernel(y_ref, gwt_ref, gb_ref, o_ref, *tok_scr):
    nb, tt, _ = o_ref.shape
    for j in range(S5_LC):
        z = jax.nn.gelu(y_ref[:, j * S5_H:(j + 1) * S5_H, :].reshape(W_MIX, LANES))
        gate = jnp.dot(gwt_ref[...], z.astype(BF16), preferred_element_type=F32) + gb_ref[...]
        yc = z * jax.nn.sigmoid(gate)
        for lb, scr in enumerate(tok_scr):
            piece = yc[lb * LANES:(lb + 1) * LANES, :].T
            for c in range(tt // S5_LC):
                scr[pl.ds(c * S5_LC + j, nb, stride=tt), :] = piece[c * nb:(c + 1) * nb, :]
    for lb, scr in enumerate(tok_scr):
        o_ref[:, :, lb * LANES:(lb + 1) * LANES] = scr[...].reshape(nb, tt, LANES).astype(BF16)


def _s5out(y, glu_wt, glu_b, n_batch):
    n = y.shape[2]
    seq = (n // n_batch) * S5_LC
    tt = (LANES // n_batch) * S5_LC
    out = pl.pallas_call(
        _s5out_kernel,
        grid=(seq // tt,),
        in_specs=[
            pl.BlockSpec((S5_G, S5_W, LANES), lambda k: (0, 0, k)),
            _const_spec((W_MIX, W_MIX)),
            _const_spec((W_MIX, 1)),
        ],
        out_specs=pl.BlockSpec((n_batch, tt, W_MIX), lambda k: (0, k, 0)),
        out_shape=jax.ShapeDtypeStruct((n_batch, seq, W_MIX), BF16),
        scratch_shapes=[pltpu.VMEM((n_batch * tt, LANES), F32)] * (W_MIX // LANES),
        compiler_params=_params("arbitrary"),
        name="s5out",
    )(y, glu_wt, glu_b)
    return out.reshape(n_batch * seq, W_MIX)


def _route(logits_t, rb_ref):
    s = [jax.nn.sigmoid(logits_t[e:e + 1, :]) for e in range(N_EXPERTS)]
    bz = [s[e] + rb_ref[e] for e in range(N_EXPERTS)]
    gs = []
    for g in range(N_GROUPS):
        v = bz[g * EXPERTS_PER_GROUP:(g + 1) * EXPERTS_PER_GROUP]
        best = None
        for i in range(EXPERTS_PER_GROUP):
            for j in range(i + 1, EXPERTS_PER_GROUP):
                pair = v[i] + v[j]
                best = pair if best is None else jnp.maximum(best, pair)
        gs.append(best)
    gbest = gs[0]
    gi = jnp.zeros_like(gbest, dtype=jnp.int32)
    for g in range(1, N_GROUPS):
        upd = gs[g] > gbest
        gi = jnp.where(upd, g, gi)
        gbest = jnp.where(upd, gs[g], gbest)

    def pick(vals, k):
        out = vals[k]
        for g in range(1, N_GROUPS):
            out = jnp.where(gi == g, vals[g * EXPERTS_PER_GROUP + k], out)
        return out

    v = [pick(bz, k) for k in range(EXPERTS_PER_GROUP)]
    sc = [pick(s, k) for k in range(EXPERTS_PER_GROUP)]

    def argmax4(vals):
        best, idx = vals[0], jnp.zeros_like(gi)
        for k in range(1, EXPERTS_PER_GROUP):
            upd = vals[k] > best
            idx = jnp.where(upd, k, idx)
            best = jnp.where(upd, vals[k], best)
        return idx

    i1 = argmax4(v)
    v2 = [jnp.where(i1 == k, -jnp.inf, v[k]) for k in range(EXPERTS_PER_GROUP)]
    i2 = argmax4(v2)

    def sel(vals, idx):
        out = vals[0]
        for k in range(1, EXPERTS_PER_GROUP):
            out = jnp.where(idx == k, vals[k], out)
        return out

    w1 = sel(sc, i1)
    w2 = sel(sc, i2)
    tot = w1 + w2
    w1 = w1 / tot
    w2 = w2 / tot
    first_low = i1 < i2
    lo = jnp.where(first_low, i1, i2)
    hi = jnp.where(first_low, i2, i1)
    wa = jnp.where(first_low, w1, w2)
    wb = jnp.where(first_low, w2, w1)
    base = gi * EXPERTS_PER_GROUP
    return (base + lo).astype(F32), (base + hi).astype(F32), wa, wb


def _mix_kernel(h_ref, yc_ref, mod_ref, n1_ref, n2_ref, win_ref, sgug_ref, sguw_ref, sgub_ref,
                convw_ref, wbr_ref, wout_ref, rwt_ref, rb_ref,
                h2_ref, hn2_ref, route_ref, ya_scr, *, conv_w):
    tm = h_ref.shape[0]
    x = h_ref[...]
    xn = _rms(x, n1_ref[...]) * (1.0 + mod_ref[0, 1:2, :]) + mod_ref[0, 0:1, :]
    xb = xn.astype(BF16)

    def proj(c0, width):
        return jnp.dot(xb, win_ref[:, c0:c0 + width], preferred_element_type=F32)

    def gate(branch):
        return jax.nn.sigmoid(proj(GATE0 + branch * D_MODEL, D_MODEL))

    u = jax.nn.gelu(proj(0, W_MIX))
    v = jax.nn.gelu(proj(W_MIX, W_MIX))
    vc = v - jnp.mean(v, axis=-1, keepdims=True)
    v = vc * lax.rsqrt(jnp.mean(vc * vc, axis=-1, keepdims=True) + EPS) * sgug_ref[...]
    vb = v.astype(BF16)
    for ck in range(tm // CHUNK):
        r0 = ck * CHUNK
        for hh in range(SGU_HEADS):
            c0 = hh * SGU_HEAD_W
            mixed = jnp.dot(sguw_ref[hh], vb[r0:r0 + CHUNK, c0:c0 + SGU_HEAD_W],
                            preferred_element_type=F32) + sgub_ref[:, hh:hh + 1]
            ya_scr[r0:r0 + CHUNK, c0:c0 + SGU_HEAD_W] = (
                u[r0:r0 + CHUNK, c0:c0 + SGU_HEAD_W] * mixed).astype(BF16)
    merged = gate(0) * jnp.dot(ya_scr[...], wbr_ref[0], preferred_element_type=F32)

    m = proj(3 * W_MIX, W_MIX) * proj(4 * W_MIX, W_MIX)
    pos = lax.broadcasted_iota(jnp.int32, (tm, 1), 0) % conv_w
    prev = jnp.where(pos == 0, 0.0, pltpu.roll(m, 1, axis=0))
    nxt = jnp.where(pos == conv_w - 1, 0.0, pltpu.roll(m, tm - 1, axis=0))
    yb = proj(2 * W_MIX, W_MIX) * (
        convw_ref[0:1, :] * prev + convw_ref[1:2, :] * m + convw_ref[2:3, :] * nxt)
    merged = merged + gate(1) * _bdot(yb, wbr_ref[1])

    merged = merged + gate(2) * jnp.dot(yc_ref[...], wbr_ref[2], preferred_element_type=F32)

    h2 = x + mod_ref[0, 2:3, :] * _bdot(merged, wout_ref[...])
    h2_ref[...] = h2
    hn2 = _rms(h2, n2_ref[...]) * (1.0 + mod_ref[0, 4:5, :]) + mod_ref[0, 3:4, :]
    hn2_ref[...] = hn2.astype(BF16)
    logits_t = lax.dot_general(rwt_ref[...], hn2, _NT, precision=HIGHEST, preferred_element_type=F32)
    ea, eb, wa, wb = _route(logits_t, rb_ref)
    zero = jnp.zeros_like(wa)
    route_ref[...] = jnp.concatenate([ea, eb, wa, wb, zero, zero, zero, zero], axis=0)


def _mix(h, yc, mod, lw, seq, conv_w, router_wt, router_b):
    t = h.shape[0]
    tm = min(TM_TOK, seq)
    per_seq = seq // tm
    tok = pl.BlockSpec((tm, D_MODEL), lambda i: (i, 0))
    in_specs = [
        tok,
        pl.BlockSpec((tm, W_MIX), lambda i: (i, 0)),
        pl.BlockSpec((1, N_MOD, D_MODEL), lambda i: (i // per_seq, 0, 0)),
        _const_spec((1, D_MODEL)),
        _const_spec((1, D_MODEL)),
        _const_spec((D_MODEL, D_IN)),
        _const_spec((1, W_MIX)),
        _const_spec((SGU_HEADS, CHUNK, CHUNK)),
        _const_spec((CHUNK, SGU_HEADS)),
        _const_spec((3, W_MIX)),
        _const_spec((N_BRANCH, W_MIX, D_MODEL)),
        _const_spec((D_MODEL, D_MODEL)),
        _const_spec((N_EXPERTS, D_MODEL)),
        pl.BlockSpec(memory_space=pltpu.SMEM),
    ]
    out_specs = [tok, tok, pl.BlockSpec((8, tm), lambda i: (0, i))]
    out_shape = [
        jax.ShapeDtypeStruct((t, D_MODEL), F32),
        jax.ShapeDtypeStruct((t, D_MODEL), BF16),
        jax.ShapeDtypeStruct((8, t), F32),
    ]
    return pl.pallas_call(
        functools.partial(_mix_kernel, conv_w=conv_w),
        grid=(t // tm,),
        in_specs=in_specs,
        out_specs=out_specs,
        out_shape=out_shape,
        scratch_shapes=[pltpu.VMEM((tm, W_MIX), BF16)],
        compiler_params=_params("arbitrary"),
        name="mix",
    )(h, yc, mod, lw["norm1_g"], lw["norm2_g"], lw["w_in"], lw["sgu_norm_g"], lw["sgu_w"],
      lw["sgu_b_t"], lw["conv_w"], lw["w_branch"], lw["w_out"], router_wt, router_b)


def _moe_kernel(ea_ref, eb_ref, valid_ref, x_ref, w_ref, ga_ref, ua_ref, da_ref,
                gb_ref, ub_ref, db_ref, o_ref):
    i = pl.program_id(0)

    @pl.when(valid_ref[i] > 0)
    def _():
        x = x_ref[...]

        def expert(g_ref, u_ref, d_ref):
            g = jnp.dot(x, g_ref[0], preferred_element_type=F32)
            act = g * jax.nn.sigmoid(g) * jnp.dot(x, u_ref[0], preferred_element_type=F32)
            return _bdot(act, d_ref[0])

        o_ref[...] = (w_ref[:, 1:2] * expert(ga_ref, ua_ref, da_ref)
                      + w_ref[:, 2:3] * expert(gb_ref, ub_ref, db_ref)).astype(BF16)

    @pl.when(valid_ref[i] == 0)
    def _():
        o_ref[...] = jnp.zeros_like(o_ref)


def _moe(hn2, route, w_g, w_u, w_d):
    t = hn2.shape[0]
    tm = TM_MOE
    nt = t // tm + N_CLASSES
    tp = nt * tm
    ea = route[0].astype(jnp.int32)
    eb = route[1].astype(jnp.int32)
    lo = ea % EXPERTS_PER_GROUP
    hi = eb % EXPERTS_PER_GROUP
    cls = (ea // EXPERTS_PER_GROUP) * N_PAIRS + (lo * (7 - lo)) // 2 + (hi - lo - 1)
    onehot = (cls[:, None] == jnp.arange(N_CLASSES, dtype=jnp.int32)[None, :]).astype(jnp.int32)
    csum = jnp.cumsum(onehot, axis=0)
    rank = jnp.sum(csum * onehot, axis=1) - 1
    counts = csum[-1]
    padded = ((counts + tm - 1) // tm) * tm
    ends = jnp.cumsum(padded)
    offs = ends - padded
    pos = offs[cls] + rank
    starts = jnp.arange(nt, dtype=jnp.int32) * tm
    tile_cls = jnp.sum((starts[:, None] >= ends[None, :]).astype(jnp.int32), axis=1)
    valid = (tile_cls < N_CLASSES).astype(jnp.int32)
    tile_cls = jnp.minimum(tile_cls, N_CLASSES - 1)
    pair_lo = jnp.array([0, 0, 0, 1, 1, 2], jnp.int32)
    pair_hi = jnp.array([1, 2, 3, 2, 3, 3], jnp.int32)
    tile_ea = (tile_cls // N_PAIRS) * EXPERTS_PER_GROUP + pair_lo[tile_cls % N_PAIRS]
    tile_eb = (tile_cls // N_PAIRS) * EXPERTS_PER_GROUP + pair_hi[tile_cls % N_PAIRS]
    slot = jnp.stack([jnp.arange(t, dtype=F32), route[2], route[3], jnp.zeros((t,), F32)], axis=1)
    ws = jnp.zeros((tp, 4), F32).at[pos].set(slot)
    xs = jnp.take(hn2, ws[:, 0].astype(jnp.int32), axis=0)

    wspec_in = lambda sel: pl.BlockSpec((1, D_MODEL, D_FF), sel)
    wspec_out = lambda sel: pl.BlockSpec((1, D_FF, D_MODEL), sel)
    sel_a = lambda i, a, b, v: (a[i], 0, 0)
    sel_b = lambda i, a, b, v: (b[i], 0, 0)
    grid_spec = pltpu.PrefetchScalarGridSpec(
        num_scalar_prefetch=3,
        grid=(nt,),
        in_specs=[
            pl.BlockSpec((tm, D_MODEL), lambda i, a, b, v: (i, 0)),
            pl.BlockSpec((tm, 4), lambda i, a, b, v: (i, 0)),
            wspec_in(sel_a), wspec_in(sel_a), wspec_out(sel_a),
            wspec_in(sel_b), wspec_in(sel_b), wspec_out(sel_b),
        ],
        out_specs=pl.BlockSpec((tm, D_MODEL), lambda i, a, b, v: (i, 0)),
    )
    out = pl.pallas_call(
        _moe_kernel,
        grid_spec=grid_spec,
        out_shape=jax.ShapeDtypeStruct((tp, D_MODEL), BF16),
        compiler_params=_params("arbitrary"),
        name="moe",
    )(tile_ea, tile_eb, valid, xs, ws, w_g, w_u, w_d, w_g, w_u, w_d)
    return jnp.take(out, pos, axis=0)


def _final_kernel(h_ref, moe_ref, mod_ref, g_ref, o_ref):
    x = h_ref[...] + mod_ref[0, 5:6, :] * moe_ref[...].astype(F32)
    o_ref[...] = _rms(x, g_ref[...])


def _final(h, moe, mod, g, seq):
    t = h.shape[0]
    tm = TM_TOK
    per_seq = seq // tm
    tok = pl.BlockSpec((tm, D_MODEL), lambda i: (i, 0))
    return pl.pallas_call(
        _final_kernel,
        grid=(t // tm,),
        in_specs=[tok, tok, pl.BlockSpec((1, N_MOD, D_MODEL), lambda i: (i // per_seq, 0, 0)),
                  _const_spec((1, D_MODEL))],
        out_specs=tok,
        out_shape=jax.ShapeDtypeStruct((t, D_MODEL), F32),
        compiler_params=_params("arbitrary"),
        name="final",
    )(h, moe, mod, g.reshape(1, D_MODEL))


def kernel(x, c, ctx, c_ctx, w_mod, b_mod, norm1_g, norm2_g, w_in, sgu_norm_g, sgu_w, sgu_b,
           conv_w, s5_lam_re, s5_lam_im, s5_log_dt, s5_b_re, s5_b_im, s5_c_re, s5_c_im, s5_d,
           glu_w, glu_b, w_branch, w_out, router_w, router_b, exp_w_gate, exp_w_up, exp_w_down,
           final_norm_g):
    nb, seq, _ = x.shape
    ctx_len = ctx.shape[1]
    h = x.reshape(nb * seq, D_MODEL)
    hc = ctx.reshape(nb * ctx_len, D_MODEL)

    mod_rows = 24
    cvec = jnp.zeros((mod_rows, D_MODEL), F32).at[:nb].set(c).at[nb].set(c_ctx)
    mods = _adaln(cvec, w_mod, b_mod).reshape(DEPTH, mod_rows, N_MOD, D_MODEL)
    router_wt = router_w.T

    all_mats = jax.vmap(_s5_mats)(s5_lam_re, s5_lam_im, s5_log_dt, s5_b_re, s5_b_im,
                                  s5_c_re, s5_c_im, s5_d)
    moe = moe_c = None
    mod = mod_c = None
    for l in range(DEPTH):
        last = l == DEPTH - 1
        pmod, pmod_c = mod, mod_c
        mod = mods[l, :nb]
        mod_c = jnp.broadcast_to(mods[l, nb][None], (nb, N_MOD, D_MODEL))
        w_in_b = w_in[l].astype(BF16)
        w_s5 = w_in_b[:, S5_COL0:S5_COL0 + W_MIX]
        lw = {
            "norm1_g": norm1_g[l].reshape(1, D_MODEL),
            "norm2_g": norm2_g[l].reshape(1, D_MODEL),
            "w_in": w_in_b,
            "sgu_norm_g": sgu_norm_g[l].reshape(1, W_MIX),
            "sgu_w": sgu_w[l].astype(BF16),
            "sgu_b_t": sgu_b[l].T,
            "conv_w": conv_w[l],
            "w_branch": w_branch[l].astype(BF16),
            "w_out": w_out[l].astype(BF16),
        }
        glu_wt = glu_w[l].T.astype(BF16)
        glu_bc = glu_b[l].reshape(W_MIX, 1)
        mats = tuple(m[l] for m in all_mats)
        w_g = exp_w_gate[l].astype(BF16)
        w_u = exp_w_up[l].astype(BF16)
        w_d = exp_w_down[l].astype(BF16)

        hc, xg_c = _pre(hc, mod_c, norm1_g[l], w_s5, ctx_len, moe=moe_c, pmod=pmod_c)
        y_c, fin_c = _s5(xg_c, mats, None, nb, want_y=not last)
        h, xg = _pre(h, mod, norm1_g[l], w_s5, seq, moe=moe, pmod=pmod)
        y, _ = _s5(xg, mats, fin_c, nb, want_y=True)
        yc = _s5out(y, glu_wt, glu_bc, nb)
        h, hn2, route = _mix(h, yc, mod, lw, seq, GRID_W, router_wt, router_b)
        if last:
            moe = _moe(hn2, route, w_g, w_u, w_d)
        else:
            yc_c = _s5out(y_c, glu_wt, glu_bc, nb)
            hc, hn2_c, route_c = _mix(hc, yc_c, mod_c, lw, ctx_len, ctx_len, router_wt, router_b)
            moe_all = _moe(jnp.concatenate([hn2, hn2_c], axis=0),
                           jnp.concatenate([route, route_c], axis=1), w_g, w_u, w_d)
            moe, moe_c = moe_all[:nb * seq], moe_all[nb * seq:]
    out = _final(h, moe, mod, final_norm_g, seq)
    return out.reshape(nb, seq, D_MODEL)
```

```python
import functools

import jax
import jax.numpy as jnp
from jax import lax
from jax.experimental import pallas as pl
from jax.experimental.pallas import tpu as pltpu

F32 = jnp.float32
BF16 = jnp.bfloat16
HIGHEST = lax.Precision.HIGHEST

D_MODEL = 1024
DEPTH = 2
EPS = 1e-6
W_MIX = 512
N_BRANCH = 3
CHUNK = 128
SGU_HEADS = 4
SGU_HEAD_W = W_MIX // SGU_HEADS
GRID_W = 64
S5_H = 16
S5_G = W_MIX // S5_H
S5_P = 64
S5_LC = 16
S5_W = S5_LC * S5_H
S5_COL0 = 5 * W_MIX
GATE0 = 6 * W_MIX
D_IN = GATE0 + N_BRANCH * D_MODEL
N_EXPERTS = 16
EXPERTS_PER_GROUP = 4
N_GROUPS = 4
D_FF = 512
N_PAIRS = 6
N_CLASSES = N_GROUPS * N_PAIRS
CLASS_ROWS = 32
N_MOD = 6
LANES = 128

TM_TOK = 512
TM_MOE = 256
PRE_B_PER_DOT = 4
CAST_BLOCK_ELEMS = 1 << 20
VMEM_LIMIT = 56 * 1024 * 1024

_NT = (((1,), (1,)), ((), ()))
_TN = (((0,), (0,)), ((), ()))


def _rms(x, g):
    return x * lax.rsqrt(jnp.mean(x * x, axis=-1, keepdims=True) + EPS) * g


def _bdot(a, b):
    return jnp.dot(a.astype(BF16), b, preferred_element_type=F32)


def _const_spec(shape):
    nd = len(shape)
    return pl.BlockSpec(shape, lambda *_: (0,) * nd, pipeline_mode=pl.Buffered(1))


def _params(*sem):
    return pltpu.CompilerParams(dimension_semantics=sem, vmem_limit_bytes=VMEM_LIMIT)


def _cast_kernel(x_ref, o_ref):
    o_ref[...] = x_ref[0].astype(BF16)


def _cast_layer(a, l):
    cols = a.shape[-1]
    rows = a[0].size // cols
    tr = 1 << ((CAST_BLOCK_ELEMS // cols).bit_length() - 1)
    while rows % tr:
        tr //= 2
    out = pl.pallas_call(
        _cast_kernel,
        grid=(rows // tr,),
        in_specs=[pl.BlockSpec((1, tr, cols), lambda i: (l, i, 0))],
        out_specs=pl.BlockSpec((tr, cols), lambda i: (i, 0)),
        out_shape=jax.ShapeDtypeStruct((rows, cols), BF16),
        compiler_params=_params("arbitrary"),
        name="cast",
    )(a.reshape(a.shape[0], rows, cols))
    return out.reshape(a.shape[1:])


def _adaln_kernel(c_ref, w_ref, b_ref, o_ref):
    c = c_ref[...]
    s = c * jax.nn.sigmoid(c)
    o_ref[0] = jnp.dot(s, w_ref[0], precision=HIGHEST, preferred_element_type=F32) + b_ref[0]


def _adaln(cvec, w_mod, b_mod):
    rows = cvec.shape[0]
    tn = 1536
    n_mod = N_MOD * D_MODEL
    return pl.pallas_call(
        _adaln_kernel,
        grid=(DEPTH, n_mod // tn),
        in_specs=[
            pl.BlockSpec((rows, D_MODEL), lambda l, j: (0, 0)),
            pl.BlockSpec((1, D_MODEL, tn), lambda l, j: (l, 0, j)),
            pl.BlockSpec((1, 1, tn), lambda l, j: (l, 0, j)),
        ],
        out_specs=pl.BlockSpec((1, rows, tn), lambda l, j: (l, 0, j)),
        out_shape=jax.ShapeDtypeStruct((DEPTH, rows, n_mod), F32),
        compiler_params=_params("arbitrary", "arbitrary"),
        name="adaln",
    )(cvec, w_mod, b_mod.reshape(DEPTH, 1, n_mod))


def _pre_kernel(*refs, residual):
    if residual:
        h_ref, moe_ref, pmod_ref, mod_ref, g_ref, w_ref, hnew_ref, xo_ref = refs[:8]
    else:
        h_ref, mod_ref, g_ref, w_ref, xo_ref = refs[:5]
    cu_scr = refs[-W_MIX // LANES:]
    nb, tt, _ = h_ref.shape
    for b0 in range(0, nb, PRE_B_PER_DOT):
        bs = slice(b0, b0 + PRE_B_PER_DOT)
        x = h_ref[bs]
        if residual:
            x = x + pmod_ref[bs, 5:6, :] * moe_ref[bs].astype(F32)
            hnew_ref[bs] = x
        xn = _rms(x, g_ref[...]) * (1.0 + mod_ref[bs, 1:2, :]) + mod_ref[bs, 0:1, :]
        cu = _bdot(xn.reshape(PRE_B_PER_DOT * tt, D_MODEL), w_ref[...])
        for lb, scr in enumerate(cu_scr):
            scr[b0 * tt:(b0 + PRE_B_PER_DOT) * tt, :] = cu[:, lb * LANES:(lb + 1) * LANES]
    groups_per_block = LANES // S5_H
    for i in range(S5_LC):
        for lb, scr in enumerate(cu_scr):
            piece = jnp.concatenate(
                [scr[pl.ds(c * S5_LC + i, nb, stride=tt), :] for c in range(tt // S5_LC)], axis=0)
            xo_ref[lb * groups_per_block:(lb + 1) * groups_per_block, i * S5_H:(i + 1) * S5_H, :] = (
                piece.T.astype(BF16).reshape(groups_per_block, S5_H, LANES))


def _pre(h, mod, norm_g, w_s5, seq, moe=None, pmod=None):
    t = h.shape[0]
    nb = t // seq
    tt = (LANES // nb) * S5_LC
    view = lambda a: a.reshape(nb, seq, D_MODEL)
    tok = pl.BlockSpec((nb, tt, D_MODEL), lambda k: (0, k, 0))
    modspec = _const_spec((nb, N_MOD, D_MODEL))
    residual = moe is not None
    in_specs = [tok]
    args = [view(h)]
    if residual:
        in_specs += [tok, modspec]
        args += [view(moe), pmod]
    in_specs += [modspec, _const_spec((1, D_MODEL)), _const_spec((D_MODEL, W_MIX))]
    args += [mod, norm_g.reshape(1, D_MODEL), w_s5]
    xo_spec = pl.BlockSpec((S5_G, S5_W, LANES), lambda k: (0, 0, k))
    xo_shape = jax.ShapeDtypeStruct((S5_G, S5_W, t // S5_LC), BF16)
    if residual:
        out_specs = [tok, xo_spec]
        out_shape = [jax.ShapeDtypeStruct((nb, seq, D_MODEL), F32), xo_shape]
    else:
        out_specs = xo_spec
        out_shape = xo_shape
    out = pl.pallas_call(
        functools.partial(_pre_kernel, residual=residual),
        grid=(seq // tt,),
        in_specs=in_specs,
        out_specs=out_specs,
        out_shape=out_shape,
        scratch_shapes=[pltpu.VMEM((nb * tt, LANES), F32)] * (W_MIX // LANES),
        compiler_params=_params("arbitrary"),
        name="pre_res" if residual else "pre",
    )(*args)
    if residual:
        return out[0].reshape(t, D_MODEL), out[1]
    return h, out


def _s5_mats(lam_re, lam_im, log_dt, b_re, b_im, c_re, c_im, dvec):
    dt = jnp.exp(log_dt)[..., None]
    mag = jnp.exp(lam_re * dt)
    ab_re = mag * jnp.cos(lam_im * dt)
    ab_im = mag * jnp.sin(lam_im * dt)
    den = lam_re * lam_re + lam_im * lam_im
    nr = ab_re - 1.0
    f_re = (nr * lam_re + ab_im * lam_im) / den
    f_im = (ab_im * lam_re - nr * lam_im) / den
    bb_re = f_re[..., None] * b_re - f_im[..., None] * b_im
    bb_im = f_re[..., None] * b_im + f_im[..., None] * b_re
    k = jnp.arange(S5_LC + 1, dtype=F32)[:, None, None, None]
    pmag = jnp.exp(lam_re * dt * k)
    pw_re = pmag * jnp.cos(lam_im * dt * k)
    pw_im = pmag * jnp.sin(lam_im * dt * k)
    cp_re = c_re[None] * pw_re[:, :, :, None, :] - c_im[None] * pw_im[:, :, :, None, :]
    cp_im = c_re[None] * pw_im[:, :, :, None, :] + c_im[None] * pw_re[:, :, :, None, :]
    kk = (jnp.einsum("kdghp,dgpj->kdghj", cp_re[:S5_LC], bb_re, precision=HIGHEST)
          - jnp.einsum("kdghp,dgpj->kdghj", cp_im[:S5_LC], bb_im, precision=HIGHEST))
    kf, kb = kk[:, 0], kk[:, 1]
    dmat = jnp.eye(S5_H, dtype=F32)[None] * dvec.reshape(S5_G, S5_H)[:, :, None]
    table = jnp.concatenate(
        [kf[:0:-1], (kf[0] + kb[0] + dmat)[None], kb[1:], jnp.zeros_like(kf[:1])], axis=0)
    table = jnp.transpose(table, (1, 2, 0, 3)).reshape(S5_G, S5_H, 2 * S5_W)
    tt = jnp.stack(
        [table[:, :, (S5_LC - 1 - j) * S5_H:(S5_LC - 1 - j) * S5_H + S5_W] for j in range(S5_LC)],
        axis=1).reshape(S5_G, S5_W, S5_W)

    rev = jnp.arange(S5_LC - 1, -1, -1)
    fwd = jnp.arange(S5_LC)

    def lanes4(f_re_part, b_re_part, f_im_part, b_im_part):
        return jnp.concatenate([f_re_part, b_re_part, f_im_part, b_im_part], axis=-1)

    def step_major(a):
        return jnp.transpose(a, (1, 0, 2))

    def chan_major(a):
        return jnp.transpose(a, (0, 2, 1))

    pr_f, pi_f = step_major(pw_re[rev, 0]), step_major(pw_im[rev, 0])
    pr_b, pi_b = step_major(pw_re[fwd, 1]), step_major(pw_im[fwd, 1])
    br_f, bi_f = chan_major(bb_re[0]), chan_major(bb_im[0])
    br_b, bi_b = chan_major(bb_re[1]), chan_major(bb_im[1])
    inj = (lanes4(pr_f, pr_b, pr_f, pr_b)[:, :, None, :] * lanes4(br_f, br_b, bi_f, bi_b)[:, None]
           + lanes4(-pi_f, -pi_b, pi_f, pi_b)[:, :, None, :] * lanes4(bi_f, bi_b, br_f, br_b)[:, None]
           ).reshape(S5_G, S5_W, 4 * S5_P)
    qr_f, qi_f = step_major(pw_re[fwd + 1, 0]), step_major(pw_im[fwd + 1, 0])
    qr_b, qi_b = step_major(pw_re[S5_LC - fwd, 1]), step_major(pw_im[S5_LC - fwd, 1])
    rt = (lanes4(qr_f, qr_b, -qi_f, -qi_b)[:, :, None, :]
          * lanes4(c_re[0], c_re[1], c_re[0], c_re[1])[:, None]
          + lanes4(-qi_f, -qi_b, -qr_f, -qr_b)[:, :, None, :]
          * lanes4(c_im[0], c_im[1], c_im[0], c_im[1])[:, None]).reshape(S5_G, S5_W, 4 * S5_P)
    a_re, a_im = pw_re[S5_LC], pw_im[S5_LC]
    pvec = jnp.concatenate([a_re[0], a_re[1]], axis=-1)
    qvec = jnp.concatenate([a_im[0], a_im[1]], axis=-1)
    pq = jnp.stack([pvec, qvec], axis=1)
    return tt.astype(BF16), inj.astype(BF16), rt.astype(BF16), pq


def _s5_kernel(*refs, n_chunks, n_batch, want_y):
    if want_y:
        x_ref, tt_ref, inj_ref, rt_ref, pq_ref, h0_ref, y_ref, fin_ref = refs[:8]
    else:
        x_ref, inj_ref, pq_ref, h0_ref, fin_ref = refs[:5]
    i_re_scr, i_im_scr, sf_re_scr, sf_im_scr, sb_re_scr, sb_im_scr = refs[-6:]
    half = 2 * S5_P
    x = x_ref[0]
    inj = lax.dot_general(x, inj_ref[0], _TN, preferred_element_type=F32)
    i_re_scr[...] = inj[:, 0:half]
    i_im_scr[...] = inj[:, half:]
    p = pq_ref[0, 0:1, :]
    q = pq_ref[0, 1:2, :]
    fwd_lanes = lax.broadcasted_iota(jnp.int32, (n_batch, half), 1) < S5_P
    h0 = h0_ref[0]
    s_re, s_im = h0[:, 0:half], h0[:, half:]
    for k in range(n_chunks):
        rows_f = slice(k * n_batch, (k + 1) * n_batch)
        rows_b = slice((n_chunks - 1 - k) * n_batch, (n_chunks - k) * n_batch)
        sf_re_scr[rows_f, :] = s_re
        sf_im_scr[rows_f, :] = s_im
        sb_re_scr[rows_b, :] = s_re
        sb_im_scr[rows_b, :] = s_im
        i_re = jnp.where(fwd_lanes, i_re_scr[rows_f, :], i_re_scr[rows_b, :])
        i_im = jnp.where(fwd_lanes, i_im_scr[rows_f, :], i_im_scr[rows_b, :])
        s_re, s_im = p * s_re - q * s_im + i_re, p * s_im + q * s_re + i_im
    fin_ref[0] = jnp.concatenate([s_re, s_im], axis=1)
    if want_y:
        fwd = lax.broadcasted_iota(jnp.int32, (1, half), 1) < S5_P
        s_prev = jnp.concatenate(
            [jnp.where(fwd, sf_re_scr[...], sb_re_scr[...]),
             jnp.where(fwd, sf_im_scr[...], sb_im_scr[...])], axis=1).astype(BF16)
        y = jnp.dot(tt_ref[0], x, preferred_element_type=F32)
        y = y + lax.dot_general(rt_ref[0], s_prev, _NT, preferred_element_type=F32)
        y_ref[0] = y


def _s5(xg, mats, h0, n_batch, want_y):
    tt, inj, rt, pq = mats
    n = xg.shape[2]
    nc = n // n_batch
    if h0 is None:
        h0 = jnp.zeros((S5_G, n_batch, S5_W), F32)
    grp = lambda shape: pl.BlockSpec((1,) + shape, lambda g: (g, 0, 0))
    sq = grp((S5_W, S5_W))
    if want_y:
        in_specs = [grp((S5_W, n)), sq, sq, sq]
        args = [xg, tt, inj, rt]
    else:
        in_specs = [grp((S5_W, n)), sq]
        args = [xg, inj]
    in_specs += [grp((2, 2 * S5_P)), grp((n_batch, S5_W))]
    args += [pq, h0]
    fin_spec = grp((n_batch, S5_W))
    fin_shape = jax.ShapeDtypeStruct((S5_G, n_batch, S5_W), F32)
    if want_y:
        out_specs = [grp((S5_W, n)), fin_spec]
        out_shape = [jax.ShapeDtypeStruct((S5_G, S5_W, n), F32), fin_shape]
    else:
        out_specs = fin_spec
        out_shape = fin_shape
    out = pl.pallas_call(
        functools.partial(_s5_kernel, n_chunks=nc, n_batch=n_batch, want_y=want_y),
        grid=(S5_G,),
        in_specs=in_specs,
        out_specs=out_specs,
        out_shape=out_shape,
        scratch_shapes=[pltpu.VMEM((n, 2 * S5_P), F32)] * 6,
        compiler_params=_params("arbitrary"),
        name="s5" if want_y else "s5_states",
    )(*args)
    if not want_y:
        return None, out
    return out[0], out[1]


def _s5out_kernel(y_ref, gwt_ref, gb_ref, o_ref, *tok_scr):
    nb, tt, _ = o_ref.shape
    for j in range(S5_LC):
        z = jax.nn.gelu(y_ref[:, j * S5_H:(j + 1) * S5_H, :].reshape(W_MIX, LANES))
        gate = jnp.dot(gwt_ref[...], z.astype(BF16), preferred_element_type=F32) + gb_ref[...]
        yc = z * jax.nn.sigmoid(gate)
        for lb, scr in enumerate(tok_scr):
            piece = yc[lb * LANES:(lb + 1) * LANES, :].T
            for c in range(tt // S5_LC):
                scr[pl.ds(c * S5_LC + j, nb, stride=tt), :] = piece[c * nb:(c + 1) * nb, :]
    for lb, scr in enumerate(tok_scr):
        o_ref[:, :, lb * LANES:(lb + 1) * LANES] = scr[...].reshape(nb, tt, LANES).astype(BF16)


def _s5out(y, glu_wt, glu_b, n_batch):
    n = y.shape[2]
    seq = (n // n_batch) * S5_LC
    tt = (LANES // n_batch) * S5_LC
    out = pl.pallas_call(
        _s5out_kernel,
        grid=(seq // tt,),
        in_specs=[
            pl.BlockSpec((S5_G, S5_W, LANES), lambda k: (0, 0, k)),
            _const_spec((W_MIX, W_MIX)),
            _const_spec((W_MIX, 1)),
        ],
        out_specs=pl.BlockSpec((n_batch, tt, W_MIX), lambda k: (0, k, 0)),
        out_shape=jax.ShapeDtypeStruct((n_batch, seq, W_MIX), BF16),
        scratch_shapes=[pltpu.VMEM((n_batch * tt, LANES), F32)] * (W_MIX // LANES),
        compiler_params=_params("arbitrary"),
        name="s5out",
    )(y, glu_wt, glu_b)
    return out.reshape(n_batch * seq, W_MIX)


def _route(logits_t, rb_ref):
    s = [jax.nn.sigmoid(logits_t[e:e + 1, :]) for e in range(N_EXPERTS)]
    bz = [s[e] + rb_ref[e] for e in range(N_EXPERTS)]
    gs = []
    for g in range(N_GROUPS):
        v = bz[g * EXPERTS_PER_GROUP:(g + 1) * EXPERTS_PER_GROUP]
        best = None
        for i in range(EXPERTS_PER_GROUP):
            for j in range(i + 1, EXPERTS_PER_GROUP):
                pair = v[i] + v[j]
                best = pair if best is None else jnp.maximum(best, pair)
        gs.append(best)
    gbest = gs[0]
    gi = jnp.zeros_like(gbest, dtype=jnp.int32)
    for g in range(1, N_GROUPS):
        upd = gs[g] > gbest
        gi = jnp.where(upd, g, gi)
        gbest = jnp.where(upd, gs[g], gbest)

    def pick(vals, k):
        out = vals[k]
        for g in range(1, N_GROUPS):
            out = jnp.where(gi == g, vals[g * EXPERTS_PER_GROUP + k], out)
        return out

    v = [pick(bz, k) for k in range(EXPERTS_PER_GROUP)]
    sc = [pick(s, k) for k in range(EXPERTS_PER_GROUP)]

    def argmax4(vals):
        best, idx = vals[0], jnp.zeros_like(gi)
        for k in range(1, EXPERTS_PER_GROUP):
            upd = vals[k] > best
            idx = jnp.where(upd, k, idx)
            best = jnp.where(upd, vals[k], best)
        return idx

    i1 = argmax4(v)
    v2 = [jnp.where(i1 == k, -jnp.inf, v[k]) for k in range(EXPERTS_PER_GROUP)]
    i2 = argmax4(v2)

    def sel(vals, idx):
        out = vals[0]
        for k in range(1, EXPERTS_PER_GROUP):
            out = jnp.where(idx == k, vals[k], out)
        return out

    w1 = sel(sc, i1)
    w2 = sel(sc, i2)
    tot = w1 + w2
    w1 = w1 / tot
    w2 = w2 / tot
    first_low = i1 < i2
    lo = jnp.where(first_low, i1, i2)
    hi = jnp.where(first_low, i2, i1)
    wa = jnp.where(first_low, w1, w2)
    wb = jnp.where(first_low, w2, w1)
    base = gi * EXPERTS_PER_GROUP
    pair = jnp.where(lo == 0, 0, jnp.where(lo == 1, 3, 5)) + (hi - lo - 1)
    cls = gi * N_PAIRS + pair
    return (base + lo).astype(F32), (base + hi).astype(F32), wa, wb, cls


def _mix_kernel(h_ref, yc_ref, mod_ref, n1_ref, n2_ref, win_ref, sgug_ref, sguw_ref, sgub_ref,
                convw_ref, wbr_ref, wout_ref, rwt_ref, tri_ref, rb_ref,
                h2_ref, hn2_ref, route_ref, counts_ref, ya_scr, count_scr, *, conv_w):
    tm = h_ref.shape[0]
    x = h_ref[...]
    xn = _rms(x, n1_ref[...]) * (1.0 + mod_ref[0, 1:2, :]) + mod_ref[0, 0:1, :]
    xb = xn.astype(BF16)

    def proj(c0, width):
        return jnp.dot(xb, win_ref[:, c0:c0 + width], preferred_element_type=F32)

    def gate(branch):
        return jax.nn.sigmoid(proj(GATE0 + branch * D_MODEL, D_MODEL))

    u = jax.nn.gelu(proj(0, W_MIX))
    v = jax.nn.gelu(proj(W_MIX, W_MIX))
    vc = v - jnp.mean(v, axis=-1, keepdims=True)
    v = vc * lax.rsqrt(jnp.mean(vc * vc, axis=-1, keepdims=True) + EPS) * sgug_ref[...]
    vb = v.astype(BF16)
    for ck in range(tm // CHUNK):
        r0 = ck * CHUNK
        for hh in range(SGU_HEADS):
            c0 = hh * SGU_HEAD_W
            mixed = jnp.dot(sguw_ref[hh], vb[r0:r0 + CHUNK, c0:c0 + SGU_HEAD_W],
                            preferred_element_type=F32) + sgub_ref[:, hh:hh + 1]
            ya_scr[r0:r0 + CHUNK, c0:c0 + SGU_HEAD_W] = (
                u[r0:r0 + CHUNK, c0:c0 + SGU_HEAD_W] * mixed).astype(BF16)
    merged = gate(0) * jnp.dot(ya_scr[...], wbr_ref[0], preferred_element_type=F32)

    m = proj(3 * W_MIX, W_MIX) * proj(4 * W_MIX, W_MIX)
    pos = lax.broadcasted_iota(jnp.int32, (tm, 1), 0) % conv_w
    prev = jnp.where(pos == 0, 0.0, pltpu.roll(m, 1, axis=0))
    nxt = jnp.where(pos == conv_w - 1, 0.0, pltpu.roll(m, tm - 1, axis=0))
    yb = proj(2 * W_MIX, W_MIX) * (
        convw_ref[0:1, :] * prev + convw_ref[1:2, :] * m + convw_ref[2:3, :] * nxt)
    merged = merged + gate(1) * _bdot(yb, wbr_ref[1])

    merged = merged + gate(2) * jnp.dot(yc_ref[...], wbr_ref[2], preferred_element_type=F32)

    h2 = x + mod_ref[0, 2:3, :] * _bdot(merged, wout_ref[...])
    h2_ref[...] = h2
    hn2 = _rms(h2, n2_ref[...]) * (1.0 + mod_ref[0, 4:5, :]) + mod_ref[0, 3:4, :]
    hn2_ref[...] = hn2.astype(BF16)
    logits_t = lax.dot_general(rwt_ref[...], hn2, _NT, precision=HIGHEST, preferred_element_type=F32)
    ea, eb, wa, wb, cls = _route(logits_t, rb_ref)

    @pl.when(pl.program_id(0) == 0)
    def _():
        count_scr[...] = jnp.zeros_like(count_scr)

    onehot = lax.broadcasted_iota(jnp.int32, (CLASS_ROWS, tm), 0) == cls
    within = jnp.dot(onehot.astype(BF16), tri_ref[...], preferred_element_type=F32)
    before = count_scr[:, 0:1]
    rank = jnp.sum(jnp.where(onehot, within + before, 0.0), axis=0, keepdims=True) - 1.0
    total = before + within[:, tm - 1:tm]
    count_scr[...] = jnp.broadcast_to(total, count_scr.shape)
    counts_ref[...] = jnp.broadcast_to(total, counts_ref.shape)
    zero = jnp.zeros_like(wa)
    route_ref[...] = jnp.concatenate([ea, eb, wa, wb, rank, cls.astype(F32), zero, zero], axis=0)


def _mix(h, yc, mod, lw, seq, conv_w, router_wt, router_b):
    t = h.shape[0]
    tm = min(TM_TOK, seq)
    per_seq = seq // tm
    tok = pl.BlockSpec((tm, D_MODEL), lambda i: (i, 0))
    in_specs = [
        tok,
        pl.BlockSpec((tm, W_MIX), lambda i: (i, 0)),
        pl.BlockSpec((1, N_MOD, D_MODEL), lambda i: (i // per_seq, 0, 0)),
        _const_spec((1, D_MODEL)),
        _const_spec((1, D_MODEL)),
        _const_spec((D_MODEL, D_IN)),
        _const_spec((1, W_MIX)),
        _const_spec((SGU_HEADS, CHUNK, CHUNK)),
        _const_spec((CHUNK, SGU_HEADS)),
        _const_spec((3, W_MIX)),
        _const_spec((N_BRANCH, W_MIX, D_MODEL)),
        _const_spec((D_MODEL, D_MODEL)),
        _const_spec((N_EXPERTS, D_MODEL)),
        _const_spec((tm, tm)),
        pl.BlockSpec(memory_space=pltpu.SMEM),
    ]
    out_specs = [tok, tok, pl.BlockSpec((8, tm), lambda i: (0, i)),
                 pl.BlockSpec((CLASS_ROWS, LANES), lambda i: (0, 0))]
    out_shape = [
        jax.ShapeDtypeStruct((t, D_MODEL), F32),
        jax.ShapeDtypeStruct((t, D_MODEL), BF16),
        jax.ShapeDtypeStruct((8, t), F32),
        jax.ShapeDtypeStruct((CLASS_ROWS, LANES), F32),
    ]
    tri = (jnp.arange(tm)[:, None] <= jnp.arange(tm)[None, :]).astype(BF16)
    h2, hn2, route, counts = pl.pallas_call(
        functools.partial(_mix_kernel, conv_w=conv_w),
        grid=(t // tm,),
        in_specs=in_specs,
        out_specs=out_specs,
        out_shape=out_shape,
        scratch_shapes=[pltpu.VMEM((tm, W_MIX), BF16), pltpu.VMEM((CLASS_ROWS, LANES), F32)],
        compiler_params=_params("arbitrary"),
        name="mix",
    )(h, yc, mod, lw["norm1_g"], lw["norm2_g"], lw["w_in"], lw["sgu_norm_g"], lw["sgu_w"],
      lw["sgu_b_t"], lw["conv_w"], lw["w_branch"], lw["w_out"], router_wt, tri, router_b)
    return h2, hn2, route, counts[:N_CLASSES, 0].astype(jnp.int32)


def _moe_kernel(ea_ref, eb_ref, valid_ref, x_ref, w_ref, ga_ref, ua_ref, da_ref,
                gb_ref, ub_ref, db_ref, o_ref):
    i = pl.program_id(0)

    @pl.when(valid_ref[i] > 0)
    def _():
        x = x_ref[...]

        def expert(g_ref, u_ref, d_ref):
            g = jnp.dot(x, g_ref[0], preferred_element_type=F32)
            act = g * jax.nn.sigmoid(g) * jnp.dot(x, u_ref[0], preferred_element_type=F32)
            return _bdot(act, d_ref[0])

        o_ref[...] = (w_ref[:, 1:2] * expert(ga_ref, ua_ref, da_ref)
                      + w_ref[:, 2:3] * expert(gb_ref, ub_ref, db_ref)).astype(BF16)

    @pl.when(valid_ref[i] == 0)
    def _():
        o_ref[...] = jnp.zeros_like(o_ref)


def _moe(parts, w_g, w_u, w_d):
    ranks, seen = [], jnp.zeros((N_CLASSES,), jnp.int32)
    for _, r, cnt in parts:
        ranks.append(r[4].astype(jnp.int32) + seen[r[5].astype(jnp.int32)])
        seen = seen + cnt
    counts = seen
    hn2 = parts[0][0] if len(parts) == 1 else jnp.concatenate([p[0] for p in parts], axis=0)
    route = parts[0][1] if len(parts) == 1 else jnp.concatenate([p[1] for p in parts], axis=1)
    rank = ranks[0] if len(parts) == 1 else jnp.concatenate(ranks, axis=0)
    cls = route[5].astype(jnp.int32)
    t = hn2.shape[0]
    tm = TM_MOE
    nt = t // tm + N_CLASSES
    tp = nt * tm
    padded = ((counts + tm - 1) // tm) * tm
    ends = jnp.cumsum(padded)
    offs = ends - padded
    pos = offs[cls] + rank
    starts = jnp.arange(nt, dtype=jnp.int32) * tm
    tile_cls = jnp.sum((starts[:, None] >= ends[None, :]).astype(jnp.int32), axis=1)
    valid = (tile_cls < N_CLASSES).astype(jnp.int32)
    tile_cls = jnp.minimum(tile_cls, N_CLASSES - 1)
    pair_lo = jnp.array([0, 0, 0, 1, 1, 2], jnp.int32)
    pair_hi = jnp.array([1, 2, 3, 2, 3, 3], jnp.int32)
    tile_ea = (tile_cls // N_PAIRS) * EXPERTS_PER_GROUP + pair_lo[tile_cls % N_PAIRS]
    tile_eb = (tile_cls // N_PAIRS) * EXPERTS_PER_GROUP + pair_hi[tile_cls % N_PAIRS]
    slot = jnp.stack([jnp.arange(t, dtype=F32), route[2], route[3], jnp.zeros((t,), F32)], axis=1)
    ws = jnp.zeros((tp, 4), F32).at[pos].set(slot)
    xs = jnp.take(hn2, ws[:, 0].astype(jnp.int32), axis=0)

    wspec_in = lambda sel: pl.BlockSpec((1, D_MODEL, D_FF), sel)
    wspec_out = lambda sel: pl.BlockSpec((1, D_FF, D_MODEL), sel)
    sel_a = lambda i, a, b, v: (a[i], 0, 0)
    sel_b = lambda i, a, b, v: (b[i], 0, 0)
    grid_spec = pltpu.PrefetchScalarGridSpec(
        num_scalar_prefetch=3,
        grid=(nt,),
        in_specs=[
            pl.BlockSpec((tm, D_MODEL), lambda i, a, b, v: (i, 0)),
            pl.BlockSpec((tm, 4), lambda i, a, b, v: (i, 0)),
            wspec_in(sel_a), wspec_in(sel_a), wspec_out(sel_a),
            wspec_in(sel_b), wspec_in(sel_b), wspec_out(sel_b),
        ],
        out_specs=pl.BlockSpec((tm, D_MODEL), lambda i, a, b, v: (i, 0)),
    )
    out = pl.pallas_call(
        _moe_kernel,
        grid_spec=grid_spec,
        out_shape=jax.ShapeDtypeStruct((tp, D_MODEL), BF16),
        compiler_params=_params("arbitrary"),
        name="moe",
    )(tile_ea, tile_eb, valid, xs, ws, w_g, w_u, w_d, w_g, w_u, w_d)
    return jnp.take(out, pos, axis=0)


def _final_kernel(h_ref, moe_ref, mod_ref, g_ref, o_ref):
    x = h_ref[...] + mod_ref[0, 5:6, :] * moe_ref[...].astype(F32)
    o_ref[...] = _rms(x, g_ref[...])


def _final(h, moe, mod, g, seq):
    t = h.shape[0]
    tm = TM_TOK
    per_seq = seq // tm
    tok = pl.BlockSpec((tm, D_MODEL), lambda i: (i, 0))
    return pl.pallas_call(
        _final_kernel,
        grid=(t // tm,),
        in_specs=[tok, tok, pl.BlockSpec((1, N_MOD, D_MODEL), lambda i: (i // per_seq, 0, 0)),
                  _const_spec((1, D_MODEL))],
        out_specs=tok,
        out_shape=jax.ShapeDtypeStruct((t, D_MODEL), F32),
        compiler_params=_params("arbitrary"),
        name="final",
    )(h, moe, mod, g.reshape(1, D_MODEL))


def kernel(x, c, ctx, c_ctx, w_mod, b_mod, norm1_g, norm2_g, w_in, sgu_norm_g, sgu_w, sgu_b,
           conv_w, s5_lam_re, s5_lam_im, s5_log_dt, s5_b_re, s5_b_im, s5_c_re, s5_c_im, s5_d,
           glu_w, glu_b, w_branch, w_out, router_w, router_b, exp_w_gate, exp_w_up, exp_w_down,
           final_norm_g):
    nb, seq, _ = x.shape
    ctx_len = ctx.shape[1]
    h = x.reshape(nb * seq, D_MODEL)
    hc = ctx.reshape(nb * ctx_len, D_MODEL)

    mod_rows = 24
    cvec = jnp.zeros((mod_rows, D_MODEL), F32).at[:nb].set(c).at[nb].set(c_ctx)
    mods = _adaln(cvec, w_mod, b_mod).reshape(DEPTH, mod_rows, N_MOD, D_MODEL)
    router_wt = router_w.T

    all_mats = jax.vmap(_s5_mats)(s5_lam_re, s5_lam_im, s5_log_dt, s5_b_re, s5_b_im,
                                  s5_c_re, s5_c_im, s5_d)
    moe = moe_c = None
    mod = mod_c = None
    for l in range(DEPTH):
        last = l == DEPTH - 1
        pmod, pmod_c = mod, mod_c
        mod = mods[l, :nb]
        mod_c = jnp.broadcast_to(mods[l, nb][None], (nb, N_MOD, D_MODEL))
        w_in_b = _cast_layer(w_in, l)
        w_s5 = w_in_b[:, S5_COL0:S5_COL0 + W_MIX]
        lw = {
            "norm1_g": norm1_g[l].reshape(1, D_MODEL),
            "norm2_g": norm2_g[l].reshape(1, D_MODEL),
            "w_in": w_in_b,
            "sgu_norm_g": sgu_norm_g[l].reshape(1, W_MIX),
            "sgu_w": sgu_w[l].astype(BF16),
            "sgu_b_t": sgu_b[l].T,
            "conv_w": conv_w[l],
            "w_branch": _cast_layer(w_branch, l),
            "w_out": _cast_layer(w_out, l),
        }
        glu_wt = glu_w[l].T.astype(BF16)
        glu_bc = glu_b[l].reshape(W_MIX, 1)
        mats = tuple(m[l] for m in all_mats)
        w_g = _cast_layer(exp_w_gate, l)
        w_u = _cast_layer(exp_w_up, l)
        w_d = _cast_layer(exp_w_down, l)

        hc, xg_c = _pre(hc, mod_c, norm1_g[l], w_s5, ctx_len, moe=moe_c, pmod=pmod_c)
        y_c, fin_c = _s5(xg_c, mats, None, nb, want_y=not last)
        h, xg = _pre(h, mod, norm1_g[l], w_s5, seq, moe=moe, pmod=pmod)
        y, _ = _s5(xg, mats, fin_c, nb, want_y=True)
        yc = _s5out(y, glu_wt, glu_bc, nb)
        h, *routed = _mix(h, yc, mod, lw, seq, GRID_W, router_wt, router_b)
        if last:
            moe = _moe([routed], w_g, w_u, w_d)
        else:
            yc_c = _s5out(y_c, glu_wt, glu_bc, nb)
            hc, *routed_c = _mix(hc, yc_c, mod_c, lw, ctx_len, ctx_len, router_wt, router_b)
            moe_all = _moe([routed, routed_c], w_g, w_u, w_d)
            moe, moe_c = moe_all[:nb * seq], moe_all[nb * seq:]
    out = _final(h, moe, mod, final_norm_g, seq)
    return out.reshape(nb, seq, D_MODEL)
```

```python
import functools

import jax
import jax.numpy as jnp
from jax import lax
from jax.experimental import pallas as pl
from jax.experimental.pallas import tpu as pltpu

F32 = jnp.float32
BF16 = jnp.bfloat16
HIGHEST = lax.Precision.HIGHEST

D_MODEL = 1024
DEPTH = 2
EPS = 1e-6
W_MIX = 512
N_BRANCH = 3
CHUNK = 128
SGU_HEADS = 4
SGU_HEAD_W = W_MIX // SGU_HEADS
GRID_W = 64
S5_H = 16
S5_G = W_MIX // S5_H
S5_P = 64
S5_LC = 16
S5_W = S5_LC * S5_H
S5_COL0 = 5 * W_MIX
GATE0 = 6 * W_MIX
D_IN = GATE0 + N_BRANCH * D_MODEL
N_EXPERTS = 16
EXPERTS_PER_GROUP = 4
N_GROUPS = 4
D_FF = 512
N_PAIRS = 6
N_CLASSES = N_GROUPS * N_PAIRS
CLASS_ROWS = 32
N_MOD = 6
LANES = 128

TM_TOK = 512
TM_MOE = 256
PRE_B_PER_DOT = 4
CAST_BLOCK_ELEMS = 1 << 20
VMEM_LIMIT = 56 * 1024 * 1024

_NT = (((1,), (1,)), ((), ()))
_TN = (((0,), (0,)), ((), ()))


def _rms(x, g):
    return x * lax.rsqrt(jnp.mean(x * x, axis=-1, keepdims=True) + EPS) * g


def _bdot(a, b):
    return jnp.dot(a.astype(BF16), b, preferred_element_type=F32)


def _const_spec(shape):
    nd = len(shape)
    return pl.BlockSpec(shape, lambda *_: (0,) * nd, pipeline_mode=pl.Buffered(1))


def _params(*sem):
    return pltpu.CompilerParams(dimension_semantics=sem, vmem_limit_bytes=VMEM_LIMIT)


def _cast_kernel(x_ref, o_ref):
    o_ref[...] = x_ref[0].astype(BF16)


def _cast_layer(a, l):
    cols = a.shape[-1]
    rows = a[0].size // cols
    tr = 1 << ((CAST_BLOCK_ELEMS // cols).bit_length() - 1)
    while rows % tr:
        tr //= 2
    out = pl.pallas_call(
        _cast_kernel,
        grid=(rows // tr,),
        in_specs=[pl.BlockSpec((1, tr, cols), lambda i: (l, i, 0))],
        out_specs=pl.BlockSpec((tr, cols), lambda i: (i, 0)),
        out_shape=jax.ShapeDtypeStruct((rows, cols), BF16),
        compiler_params=_params("arbitrary"),
        name="cast",
    )(a.reshape(a.shape[0], rows, cols))
    return out.reshape(a.shape[1:])


def _adaln_kernel(c_ref, w_ref, b_ref, o_ref):
    c = c_ref[...]
    s = c * jax.nn.sigmoid(c)
    o_ref[0] = jnp.dot(s, w_ref[0], precision=HIGHEST, preferred_element_type=F32) + b_ref[0]


def _adaln(cvec, w_mod, b_mod):
    rows = cvec.shape[0]
    tn = 1536
    n_mod = N_MOD * D_MODEL
    return pl.pallas_call(
        _adaln_kernel,
        grid=(DEPTH, n_mod // tn),
        in_specs=[
            pl.BlockSpec((rows, D_MODEL), lambda l, j: (0, 0)),
            pl.BlockSpec((1, D_MODEL, tn), lambda l, j: (l, 0, j)),
            pl.BlockSpec((1, 1, tn), lambda l, j: (l, 0, j)),
        ],
        out_specs=pl.BlockSpec((1, rows, tn), lambda l, j: (l, 0, j)),
        out_shape=jax.ShapeDtypeStruct((DEPTH, rows, n_mod), F32),
        compiler_params=_params("arbitrary", "arbitrary"),
        name="adaln",
    )(cvec, w_mod, b_mod.reshape(DEPTH, 1, n_mod))


def _pre_kernel(*refs, residual):
    if residual:
        h_ref, moe_ref, pmod_ref, mod_ref, g_ref, w_ref, hnew_ref, xo_ref = refs[:8]
    else:
        h_ref, mod_ref, g_ref, w_ref, xo_ref = refs[:5]
    cu_scr = refs[-W_MIX // LANES:]
    nb, tt, _ = h_ref.shape
    for b0 in range(0, nb, PRE_B_PER_DOT):
        bs = slice(b0, b0 + PRE_B_PER_DOT)
        x = h_ref[bs]
        if residual:
            x = x + pmod_ref[bs, 5:6, :] * moe_ref[bs].astype(F32)
            hnew_ref[bs] = x
        xn = _rms(x, g_ref[...]) * (1.0 + mod_ref[bs, 1:2, :]) + mod_ref[bs, 0:1, :]
        cu = _bdot(xn.reshape(PRE_B_PER_DOT * tt, D_MODEL), w_ref[...])
        for lb, scr in enumerate(cu_scr):
            scr[b0 * tt:(b0 + PRE_B_PER_DOT) * tt, :] = cu[:, lb * LANES:(lb + 1) * LANES]
    groups_per_block = LANES // S5_H
    for i in range(S5_LC):
        for lb, scr in enumerate(cu_scr):
            piece = jnp.concatenate(
                [scr[pl.ds(c * S5_LC + i, nb, stride=tt), :] for c in range(tt // S5_LC)], axis=0)
            xo_ref[lb * groups_per_block:(lb + 1) * groups_per_block, i * S5_H:(i + 1) * S5_H, :] = (
                piece.T.astype(BF16).reshape(groups_per_block, S5_H, LANES))


def _pre(h, mod, norm_g, w_s5, seq, moe=None, pmod=None):
    t = h.shape[0]
    nb = t // seq
    tt = (LANES // nb) * S5_LC
    view = lambda a: a.reshape(nb, seq, D_MODEL)
    tok = pl.BlockSpec((nb, tt, D_MODEL), lambda k: (0, k, 0))
    modspec = _const_spec((nb, N_MOD, D_MODEL))
    residual = moe is not None
    in_specs = [tok]
    args = [view(h)]
    if residual:
        in_specs += [tok, modspec]
        args += [view(moe), pmod]
    in_specs += [modspec, _const_spec((1, D_MODEL)), _const_spec((D_MODEL, W_MIX))]
    args += [mod, norm_g.reshape(1, D_MODEL), w_s5]
    xo_spec = pl.BlockSpec((S5_G, S5_W, LANES), lambda k: (0, 0, k))
    xo_shape = jax.ShapeDtypeStruct((S5_G, S5_W, t // S5_LC), BF16)
    if residual:
        out_specs = [tok, xo_spec]
        out_shape = [jax.ShapeDtypeStruct((nb, seq, D_MODEL), F32), xo_shape]
    else:
        out_specs = xo_spec
        out_shape = xo_shape
    out = pl.pallas_call(
        functools.partial(_pre_kernel, residual=residual),
        grid=(seq // tt,),
        in_specs=in_specs,
        out_specs=out_specs,
        out_shape=out_shape,
        scratch_shapes=[pltpu.VMEM((nb * tt, LANES), F32)] * (W_MIX // LANES),
        compiler_params=_params("arbitrary"),
        name="pre_res" if residual else "pre",
    )(*args)
    if residual:
        return out[0].reshape(t, D_MODEL), out[1]
    return h, out


def _s5_mats(lam_re, lam_im, log_dt, b_re, b_im, c_re, c_im, dvec):
    dt = jnp.exp(log_dt)[..., None]
    mag = jnp.exp(lam_re * dt)
    ab_re = mag * jnp.cos(lam_im * dt)
    ab_im = mag * jnp.sin(lam_im * dt)
    den = lam_re * lam_re + lam_im * lam_im
    nr = ab_re - 1.0
    f_re = (nr * lam_re + ab_im * lam_im) / den
    f_im = (ab_im * lam_re - nr * lam_im) / den
    bb_re = f_re[..., None] * b_re - f_im[..., None] * b_im
    bb_im = f_re[..., None] * b_im + f_im[..., None] * b_re
    k = jnp.arange(S5_LC + 1, dtype=F32)[:, None, None, None]
    pmag = jnp.exp(lam_re * dt * k)
    pw_re = pmag * jnp.cos(lam_im * dt * k)
    pw_im = pmag * jnp.sin(lam_im * dt * k)
    cp_re = c_re[None] * pw_re[:, :, :, None, :] - c_im[None] * pw_im[:, :, :, None, :]
    cp_im = c_re[None] * pw_im[:, :, :, None, :] + c_im[None] * pw_re[:, :, :, None, :]
    kk = jnp.einsum("kdghp,dgpj->kdghj",
                    jnp.concatenate([cp_re[:S5_LC], -cp_im[:S5_LC]], axis=-1),
                    jnp.concatenate([bb_re, bb_im], axis=-2))
    kf, kb = kk[:, 0], kk[:, 1]
    dmat = jnp.eye(S5_H, dtype=F32)[None] * dvec.reshape(S5_G, S5_H)[:, :, None]
    table = jnp.concatenate(
        [kf[:0:-1], (kf[0] + kb[0] + dmat)[None], kb[1:], jnp.zeros_like(kf[:1])], axis=0)
    table = jnp.transpose(table, (1, 2, 0, 3)).reshape(S5_G, S5_H, 2 * S5_W)
    tt = jnp.stack(
        [table[:, :, (S5_LC - 1 - j) * S5_H:(S5_LC - 1 - j) * S5_H + S5_W] for j in range(S5_LC)],
        axis=1).reshape(S5_G, S5_W, S5_W)

    rev = jnp.arange(S5_LC - 1, -1, -1)
    fwd = jnp.arange(S5_LC)

    def lanes4(f_re_part, b_re_part, f_im_part, b_im_part):
        return jnp.concatenate([f_re_part, b_re_part, f_im_part, b_im_part], axis=-1)

    def step_major(a):
        return jnp.transpose(a, (1, 0, 2))

    def chan_major(a):
        return jnp.transpose(a, (0, 2, 1))

    pr_f, pi_f = step_major(pw_re[rev, 0]), step_major(pw_im[rev, 0])
    pr_b, pi_b = step_major(pw_re[fwd, 1]), step_major(pw_im[fwd, 1])
    br_f, bi_f = chan_major(bb_re[0]), chan_major(bb_im[0])
    br_b, bi_b = chan_major(bb_re[1]), chan_major(bb_im[1])
    inj = (lanes4(pr_f, pr_b, pr_f, pr_b)[:, :, None, :] * lanes4(br_f, br_b, bi_f, bi_b)[:, None]
           + lanes4(-pi_f, -pi_b, pi_f, pi_b)[:, :, None, :] * lanes4(bi_f, bi_b, br_f, br_b)[:, None]
           ).reshape(S5_G, S5_W, 4 * S5_P)
    qr_f, qi_f = step_major(pw_re[fwd + 1, 0]), step_major(pw_im[fwd + 1, 0])
    qr_b, qi_b = step_major(pw_re[S5_LC - fwd, 1]), step_major(pw_im[S5_LC - fwd, 1])
    rt = (lanes4(qr_f, qr_b, -qi_f, -qi_b)[:, :, None, :]
          * lanes4(c_re[0], c_re[1], c_re[0], c_re[1])[:, None]
          + lanes4(-qi_f, -qi_b, -qr_f, -qr_b)[:, :, None, :]
          * lanes4(c_im[0], c_im[1], c_im[0], c_im[1])[:, None]).reshape(S5_G, S5_W, 4 * S5_P)
    a_re, a_im = pw_re[S5_LC], pw_im[S5_LC]
    pvec = jnp.concatenate([a_re[0], a_re[1]], axis=-1)
    qvec = jnp.concatenate([a_im[0], a_im[1]], axis=-1)
    pq = jnp.stack([pvec, qvec], axis=1)
    return tt.astype(BF16), inj.astype(BF16), rt.astype(BF16), pq


def _s5_kernel(*refs, n_chunks, n_batch, want_y):
    if want_y:
        x_ref, tt_ref, inj_ref, rt_ref, pq_ref, h0_ref, y_ref, fin_ref = refs[:8]
    else:
        x_ref, inj_ref, pq_ref, h0_ref, fin_ref = refs[:5]
    i_re_scr, i_im_scr, sf_re_scr, sf_im_scr, sb_re_scr, sb_im_scr = refs[-6:]
    half = 2 * S5_P
    x = x_ref[0]
    inj = lax.dot_general(x, inj_ref[0], _TN, preferred_element_type=F32)
    i_re_scr[...] = inj[:, 0:half]
    i_im_scr[...] = inj[:, half:]
    p = pq_ref[0, 0:1, :]
    q = pq_ref[0, 1:2, :]
    fwd_lanes = lax.broadcasted_iota(jnp.int32, (n_batch, half), 1) < S5_P
    h0 = h0_ref[0]
    s_re, s_im = h0[:, 0:half], h0[:, half:]
    for k in range(n_chunks):
        rows_f = slice(k * n_batch, (k + 1) * n_batch)
        rows_b = slice((n_chunks - 1 - k) * n_batch, (n_chunks - k) * n_batch)
        sf_re_scr[rows_f, :] = s_re
        sf_im_scr[rows_f, :] = s_im
        sb_re_scr[rows_b, :] = s_re
        sb_im_scr[rows_b, :] = s_im
        i_re = jnp.where(fwd_lanes, i_re_scr[rows_f, :], i_re_scr[rows_b, :])
        i_im = jnp.where(fwd_lanes, i_im_scr[rows_f, :], i_im_scr[rows_b, :])
        s_re, s_im = p * s_re - q * s_im + i_re, p * s_im + q * s_re + i_im
    fin_ref[0] = jnp.concatenate([s_re, s_im], axis=1)
    if want_y:
        fwd = lax.broadcasted_iota(jnp.int32, (1, half), 1) < S5_P
        s_prev = jnp.concatenate(
            [jnp.where(fwd, sf_re_scr[...], sb_re_scr[...]),
             jnp.where(fwd, sf_im_scr[...], sb_im_scr[...])], axis=1).astype(BF16)
        y = jnp.dot(tt_ref[0], x, preferred_element_type=F32)
        y = y + lax.dot_general(rt_ref[0], s_prev, _NT, preferred_element_type=F32)
        y_ref[0] = y


def _s5(xg, mats, h0, n_batch, want_y):
    tt, inj, rt, pq = mats
    n = xg.shape[2]
    nc = n // n_batch
    if h0 is None:
        h0 = jnp.zeros((S5_G, n_batch, S5_W), F32)
    grp = lambda shape: pl.BlockSpec((1,) + shape, lambda g: (g, 0, 0))
    sq = grp((S5_W, S5_W))
    if want_y:
        in_specs = [grp((S5_W, n)), sq, sq, sq]
        args = [xg, tt, inj, rt]
    else:
        in_specs = [grp((S5_W, n)), sq]
        args = [xg, inj]
    in_specs += [grp((2, 2 * S5_P)), grp((n_batch, S5_W))]
    args += [pq, h0]
    fin_spec = grp((n_batch, S5_W))
    fin_shape = jax.ShapeDtypeStruct((S5_G, n_batch, S5_W), F32)
    if want_y:
        out_specs = [grp((S5_W, n)), fin_spec]
        out_shape = [jax.ShapeDtypeStruct((S5_G, S5_W, n), F32), fin_shape]
    else:
        out_specs = fin_spec
        out_shape = fin_shape
    out = pl.pallas_call(
        functools.partial(_s5_kernel, n_chunks=nc, n_batch=n_batch, want_y=want_y),
        grid=(S5_G,),
        in_specs=in_specs,
        out_specs=out_specs,
        out_shape=out_shape,
        scratch_shapes=[pltpu.VMEM((n, 2 * S5_P), F32)] * 6,
        compiler_params=_params("arbitrary"),
        name="s5" if want_y else "s5_states",
    )(*args)
    if not want_y:
        return None, out
    return out[0], out[1]


def _s5out_kernel(y_ref, gwt_ref, gb_ref, o_ref, *tok_scr):
    nb, tt, _ = o_ref.shape
    for j in range(S5_LC):
        z = jax.nn.gelu(y_ref[:, j * S5_H:(j + 1) * S5_H, :].reshape(W_MIX, LANES))
        gate = jnp.dot(gwt_ref[...], z.astype(BF16), preferred_element_type=F32) + gb_ref[...]
        yc = z * jax.nn.sigmoid(gate)
        for lb, scr in enumerate(tok_scr):
            piece = yc[lb * LANES:(lb + 1) * LANES, :].T
            for c in range(tt // S5_LC):
                scr[pl.ds(c * S5_LC + j, nb, stride=tt), :] = piece[c * nb:(c + 1) * nb, :]
    for lb, scr in enumerate(tok_scr):
        o_ref[:, :, lb * LANES:(lb + 1) * LANES] = scr[...].reshape(nb, tt, LANES).astype(BF16)


def _s5out(y, glu_wt, glu_b, n_batch):
    n = y.shape[2]
    seq = (n // n_batch) * S5_LC
    tt = (LANES // n_batch) * S5_LC
    out = pl.pallas_call(
        _s5out_kernel,
        grid=(seq // tt,),
        in_specs=[
            pl.BlockSpec((S5_G, S5_W, LANES), lambda k: (0, 0, k)),
            _const_spec((W_MIX, W_MIX)),
            _const_spec((W_MIX, 1)),
        ],
        out_specs=pl.BlockSpec((n_batch, tt, W_MIX), lambda k: (0, k, 0)),
        out_shape=jax.ShapeDtypeStruct((n_batch, seq, W_MIX), BF16),
        scratch_shapes=[pltpu.VMEM((n_batch * tt, LANES), F32)] * (W_MIX // LANES),
        compiler_params=_params("arbitrary"),
        name="s5out",
    )(y, glu_wt, glu_b)
    return out.reshape(n_batch * seq, W_MIX)


def _route(logits_t, rb_ref):
    s = [jax.nn.sigmoid(logits_t[e:e + 1, :]) for e in range(N_EXPERTS)]
    bz = [s[e] + rb_ref[e] for e in range(N_EXPERTS)]
    gs = []
    for g in range(N_GROUPS):
        v = bz[g * EXPERTS_PER_GROUP:(g + 1) * EXPERTS_PER_GROUP]
        best = None
        for i in range(EXPERTS_PER_GROUP):
            for j in range(i + 1, EXPERTS_PER_GROUP):
                pair = v[i] + v[j]
                best = pair if best is None else jnp.maximum(best, pair)
        gs.append(best)
    gbest = gs[0]
    gi = jnp.zeros_like(gbest, dtype=jnp.int32)
    for g in range(1, N_GROUPS):
        upd = gs[g] > gbest
        gi = jnp.where(upd, g, gi)
        gbest = jnp.where(upd, gs[g], gbest)

    def pick(vals, k):
        out = vals[k]
        for g in range(1, N_GROUPS):
            out = jnp.where(gi == g, vals[g * EXPERTS_PER_GROUP + k], out)
        return out

    v = [pick(bz, k) for k in range(EXPERTS_PER_GROUP)]
    sc = [pick(s, k) for k in range(EXPERTS_PER_GROUP)]

    def argmax4(vals):
        best, idx = vals[0], jnp.zeros_like(gi)
        for k in range(1, EXPERTS_PER_GROUP):
            upd = vals[k] > best
            idx = jnp.where(upd, k, idx)
            best = jnp.where(upd, vals[k], best)
        return idx

    i1 = argmax4(v)
    v2 = [jnp.where(i1 == k, -jnp.inf, v[k]) for k in range(EXPERTS_PER_GROUP)]
    i2 = argmax4(v2)

    def sel(vals, idx):
        out = vals[0]
        for k in range(1, EXPERTS_PER_GROUP):
            out = jnp.where(idx == k, vals[k], out)
        return out

    w1 = sel(sc, i1)
    w2 = sel(sc, i2)
    tot = w1 + w2
    w1 = w1 / tot
    w2 = w2 / tot
    first_low = i1 < i2
    lo = jnp.where(first_low, i1, i2)
    hi = jnp.where(first_low, i2, i1)
    wa = jnp.where(first_low, w1, w2)
    wb = jnp.where(first_low, w2, w1)
    base = gi * EXPERTS_PER_GROUP
    pair = jnp.where(lo == 0, 0, jnp.where(lo == 1, 3, 5)) + (hi - lo - 1)
    cls = gi * N_PAIRS + pair
    return (base + lo).astype(F32), (base + hi).astype(F32), wa, wb, cls


def _mix_kernel(h_ref, yc_ref, mod_ref, n1_ref, n2_ref, win_ref, sgug_ref, sguw_ref, sgub_ref,
                convw_ref, wbr_ref, wout_ref, rwt_ref, tri_ref, rb_ref,
                h2_ref, hn2_ref, route_ref, counts_ref, ya_scr, count_scr, *, conv_w):
    tm = h_ref.shape[0]
    x = h_ref[...]
    xn = _rms(x, n1_ref[...]) * (1.0 + mod_ref[0, 1:2, :]) + mod_ref[0, 0:1, :]
    xb = xn.astype(BF16)

    def proj(c0, width):
        return jnp.dot(xb, win_ref[:, c0:c0 + width], preferred_element_type=F32)

    def gate(branch):
        return jax.nn.sigmoid(proj(GATE0 + branch * D_MODEL, D_MODEL))

    u = jax.nn.gelu(proj(0, W_MIX))
    v = jax.nn.gelu(proj(W_MIX, W_MIX))
    vc = v - jnp.mean(v, axis=-1, keepdims=True)
    v = vc * lax.rsqrt(jnp.mean(vc * vc, axis=-1, keepdims=True) + EPS) * sgug_ref[...]
    vb = v.astype(BF16)
    for ck in range(tm // CHUNK):
        r0 = ck * CHUNK
        for hh in range(SGU_HEADS):
            c0 = hh * SGU_HEAD_W
            mixed = jnp.dot(sguw_ref[hh], vb[r0:r0 + CHUNK, c0:c0 + SGU_HEAD_W],
                            preferred_element_type=F32) + sgub_ref[:, hh:hh + 1]
            ya_scr[r0:r0 + CHUNK, c0:c0 + SGU_HEAD_W] = (
                u[r0:r0 + CHUNK, c0:c0 + SGU_HEAD_W] * mixed).astype(BF16)
    merged = gate(0) * jnp.dot(ya_scr[...], wbr_ref[0], preferred_element_type=F32)

    m = proj(3 * W_MIX, W_MIX) * proj(4 * W_MIX, W_MIX)
    pos = lax.broadcasted_iota(jnp.int32, (tm, 1), 0) % conv_w
    prev = jnp.where(pos == 0, 0.0, pltpu.roll(m, 1, axis=0))
    nxt = jnp.where(pos == conv_w - 1, 0.0, pltpu.roll(m, tm - 1, axis=0))
    yb = proj(2 * W_MIX, W_MIX) * (
        convw_ref[0:1, :] * prev + convw_ref[1:2, :] * m + convw_ref[2:3, :] * nxt)
    merged = merged + gate(1) * _bdot(yb, wbr_ref[1])

    merged = merged + gate(2) * jnp.dot(yc_ref[...], wbr_ref[2], preferred_element_type=F32)

    h2 = x + mod_ref[0, 2:3, :] * _bdot(merged, wout_ref[...])
    h2_ref[...] = h2
    hn2 = _rms(h2, n2_ref[...]) * (1.0 + mod_ref[0, 4:5, :]) + mod_ref[0, 3:4, :]
    hn2_ref[...] = hn2.astype(BF16)
    logits_t = lax.dot_general(rwt_ref[...], hn2, _NT, precision=HIGHEST, preferred_element_type=F32)
    ea, eb, wa, wb, cls = _route(logits_t, rb_ref)

    @pl.when(pl.program_id(0) == 0)
    def _():
        count_scr[...] = jnp.zeros_like(count_scr)

    onehot = lax.broadcasted_iota(jnp.int32, (CLASS_ROWS, tm), 0) == cls
    within = jnp.dot(onehot.astype(BF16), tri_ref[...], preferred_element_type=F32)
    before = count_scr[:, 0:1]
    rank = jnp.sum(jnp.where(onehot, within + before, 0.0), axis=0, keepdims=True) - 1.0
    total = before + within[:, tm - 1:tm]
    count_scr[...] = jnp.broadcast_to(total, count_scr.shape)
    counts_ref[...] = jnp.broadcast_to(total, counts_ref.shape)
    zero = jnp.zeros_like(wa)
    route_ref[...] = jnp.concatenate([ea, eb, wa, wb, rank, cls.astype(F32), zero, zero], axis=0)


def _mix(h, yc, mod, lw, seq, conv_w, router_wt, router_b):
    t = h.shape[0]
    tm = min(TM_TOK, seq)
    per_seq = seq // tm
    tok = pl.BlockSpec((tm, D_MODEL), lambda i: (i, 0))
    in_specs = [
        tok,
        pl.BlockSpec((tm, W_MIX), lambda i: (i, 0)),
        pl.BlockSpec((1, N_MOD, D_MODEL), lambda i: (i // per_seq, 0, 0)),
        _const_spec((1, D_MODEL)),
        _const_spec((1, D_MODEL)),
        _const_spec((D_MODEL, D_IN)),
        _const_spec((1, W_MIX)),
        _const_spec((SGU_HEADS, CHUNK, CHUNK)),
        _const_spec((CHUNK, SGU_HEADS)),
        _const_spec((3, W_MIX)),
        _const_spec((N_BRANCH, W_MIX, D_MODEL)),
        _const_spec((D_MODEL, D_MODEL)),
        _const_spec((N_EXPERTS, D_MODEL)),
        _const_spec((tm, tm)),
        pl.BlockSpec(memory_space=pltpu.SMEM),
    ]
    out_specs = [tok, tok, pl.BlockSpec((8, tm), lambda i: (0, i)),
                 pl.BlockSpec((CLASS_ROWS, LANES), lambda i: (0, 0))]
    out_shape = [
        jax.ShapeDtypeStruct((t, D_MODEL), F32),
        jax.ShapeDtypeStruct((t, D_MODEL), BF16),
        jax.ShapeDtypeStruct((8, t), F32),
        jax.ShapeDtypeStruct((CLASS_ROWS, LANES), F32),
    ]
    tri = (jnp.arange(tm)[:, None] <= jnp.arange(tm)[None, :]).astype(BF16)
    h2, hn2, route, counts = pl.pallas_call(
        functools.partial(_mix_kernel, conv_w=conv_w),
        grid=(t // tm,),
        in_specs=in_specs,
        out_specs=out_specs,
        out_shape=out_shape,
        scratch_shapes=[pltpu.VMEM((tm, W_MIX), BF16), pltpu.VMEM((CLASS_ROWS, LANES), F32)],
        compiler_params=_params("arbitrary"),
        name="mix",
    )(h, yc, mod, lw["norm1_g"], lw["norm2_g"], lw["w_in"], lw["sgu_norm_g"], lw["sgu_w"],
      lw["sgu_b_t"], lw["conv_w"], lw["w_branch"], lw["w_out"], router_wt, tri, router_b)
    return h2, hn2, route, counts[:N_CLASSES, 0].astype(jnp.int32)


def _moe_kernel(ea_ref, eb_ref, valid_ref, x_ref, w_ref, ga_ref, ua_ref, da_ref,
                gb_ref, ub_ref, db_ref, o_ref):
    i = pl.program_id(0)

    @pl.when(valid_ref[i] > 0)
    def _():
        x = x_ref[...]

        def expert(g_ref, u_ref, d_ref):
            g = jnp.dot(x, g_ref[0], preferred_element_type=F32)
            act = g * jax.nn.sigmoid(g) * jnp.dot(x, u_ref[0], preferred_element_type=F32)
            return _bdot(act, d_ref[0])

        o_ref[...] = (w_ref[:, 1:2] * expert(ga_ref, ua_ref, da_ref)
                      + w_ref[:, 2:3] * expert(gb_ref, ub_ref, db_ref)).astype(BF16)

    @pl.when(valid_ref[i] == 0)
    def _():
        o_ref[...] = jnp.zeros_like(o_ref)


def _moe(parts, w_g, w_u, w_d):
    ranks, seen = [], jnp.zeros((N_CLASSES,), jnp.int32)
    for _, r, cnt in parts:
        ranks.append(r[4].astype(jnp.int32) + seen[r[5].astype(jnp.int32)])
        seen = seen + cnt
    counts = seen
    hn2 = parts[0][0] if len(parts) == 1 else jnp.concatenate([p[0] for p in parts], axis=0)
    route = parts[0][1] if len(parts) == 1 else jnp.concatenate([p[1] for p in parts], axis=1)
    rank = ranks[0] if len(parts) == 1 else jnp.concatenate(ranks, axis=0)
    cls = route[5].astype(jnp.int32)
    t = hn2.shape[0]
    tm = TM_MOE
    nt = t // tm + N_CLASSES
    tp = nt * tm
    padded = ((counts + tm - 1) // tm) * tm
    ends = jnp.cumsum(padded)
    offs = ends - padded
    pos = offs[cls] + rank
    starts = jnp.arange(nt, dtype=jnp.int32) * tm
    tile_cls = jnp.sum((starts[:, None] >= ends[None, :]).astype(jnp.int32), axis=1)
    valid = (tile_cls < N_CLASSES).astype(jnp.int32)
    tile_cls = jnp.minimum(tile_cls, N_CLASSES - 1)
    pair_lo = jnp.array([0, 0, 0, 1, 1, 2], jnp.int32)
    pair_hi = jnp.array([1, 2, 3, 2, 3, 3], jnp.int32)
    tile_ea = (tile_cls // N_PAIRS) * EXPERTS_PER_GROUP + pair_lo[tile_cls % N_PAIRS]
    tile_eb = (tile_cls // N_PAIRS) * EXPERTS_PER_GROUP + pair_hi[tile_cls % N_PAIRS]
    slot = jnp.stack([jnp.arange(t, dtype=F32), route[2], route[3], jnp.zeros((t,), F32)], axis=1)
    spread = (jnp.arange(tp, dtype=jnp.int32) % t).astype(F32)
    empty = jnp.stack([spread] + [jnp.zeros((tp,), F32)] * 3, axis=1)
    ws = empty.at[pos].set(slot, mode="promise_in_bounds", unique_indices=True)
    xs = hn2.at[ws[:, 0].astype(jnp.int32)].get(mode="promise_in_bounds")

    wspec_in = lambda sel: pl.BlockSpec((1, D_MODEL, D_FF), sel)
    wspec_out = lambda sel: pl.BlockSpec((1, D_FF, D_MODEL), sel)
    sel_a = lambda i, a, b, v: (a[i], 0, 0)
    sel_b = lambda i, a, b, v: (b[i], 0, 0)
    grid_spec = pltpu.PrefetchScalarGridSpec(
        num_scalar_prefetch=3,
        grid=(nt,),
        in_specs=[
            pl.BlockSpec((tm, D_MODEL), lambda i, a, b, v: (i, 0)),
            pl.BlockSpec((tm, 4), lambda i, a, b, v: (i, 0)),
            wspec_in(sel_a), wspec_in(sel_a), wspec_out(sel_a),
            wspec_in(sel_b), wspec_in(sel_b), wspec_out(sel_b),
        ],
        out_specs=pl.BlockSpec((tm, D_MODEL), lambda i, a, b, v: (i, 0)),
    )
    out = pl.pallas_call(
        _moe_kernel,
        grid_spec=grid_spec,
        out_shape=jax.ShapeDtypeStruct((tp, D_MODEL), BF16),
        compiler_params=_params("arbitrary"),
        name="moe",
    )(tile_ea, tile_eb, valid, xs, ws, w_g, w_u, w_d, w_g, w_u, w_d)
    return out.at[pos].get(mode="promise_in_bounds", unique_indices=True)


def _final_kernel(h_ref, moe_ref, mod_ref, g_ref, o_ref):
    x = h_ref[...] + mod_ref[0, 5:6, :] * moe_ref[...].astype(F32)
    o_ref[...] = _rms(x, g_ref[...])


def _final(h, moe, mod, g, seq):
    t = h.shape[0]
    tm = TM_TOK
    per_seq = seq // tm
    tok = pl.BlockSpec((tm, D_MODEL), lambda i: (i, 0))
    return pl.pallas_call(
        _final_kernel,
        grid=(t // tm,),
        in_specs=[tok, tok, pl.BlockSpec((1, N_MOD, D_MODEL), lambda i: (i // per_seq, 0, 0)),
                  _const_spec((1, D_MODEL))],
        out_specs=tok,
        out_shape=jax.ShapeDtypeStruct((t, D_MODEL), F32),
        compiler_params=_params("arbitrary"),
        name="final",
    )(h, moe, mod, g.reshape(1, D_MODEL))


def kernel(x, c, ctx, c_ctx, w_mod, b_mod, norm1_g, norm2_g, w_in, sgu_norm_g, sgu_w, sgu_b,
           conv_w, s5_lam_re, s5_lam_im, s5_log_dt, s5_b_re, s5_b_im, s5_c_re, s5_c_im, s5_d,
           glu_w, glu_b, w_branch, w_out, router_w, router_b, exp_w_gate, exp_w_up, exp_w_down,
           final_norm_g):
    nb, seq, _ = x.shape
    ctx_len = ctx.shape[1]
    h = x.reshape(nb * seq, D_MODEL)
    hc = ctx.reshape(nb * ctx_len, D_MODEL)

    mod_rows = 24
    cvec = jnp.zeros((mod_rows, D_MODEL), F32).at[:nb].set(c).at[nb].set(c_ctx)
    mods = _adaln(cvec, w_mod, b_mod).reshape(DEPTH, mod_rows, N_MOD, D_MODEL)
    router_wt = router_w.T

    all_mats = jax.vmap(_s5_mats)(s5_lam_re, s5_lam_im, s5_log_dt, s5_b_re, s5_b_im,
                                  s5_c_re, s5_c_im, s5_d)
    moe = moe_c = None
    mod = mod_c = None
    for l in range(DEPTH):
        last = l == DEPTH - 1
        pmod, pmod_c = mod, mod_c
        mod = mods[l, :nb]
        mod_c = jnp.broadcast_to(mods[l, nb][None], (nb, N_MOD, D_MODEL))
        w_in_b = _cast_layer(w_in, l)
        w_s5 = w_in_b[:, S5_COL0:S5_COL0 + W_MIX]
        lw = {
            "norm1_g": norm1_g[l].reshape(1, D_MODEL),
            "norm2_g": norm2_g[l].reshape(1, D_MODEL),
            "w_in": w_in_b,
            "sgu_norm_g": sgu_norm_g[l].reshape(1, W_MIX),
            "sgu_w": sgu_w[l].astype(BF16),
            "sgu_b_t": sgu_b[l].T,
            "conv_w": conv_w[l],
            "w_branch": _cast_layer(w_branch, l),
            "w_out": _cast_layer(w_out, l),
        }
        glu_wt = glu_w[l].T.astype(BF16)
        glu_bc = glu_b[l].reshape(W_MIX, 1)
        mats = tuple(m[l] for m in all_mats)
        w_g = _cast_layer(exp_w_gate, l)
        w_u = _cast_layer(exp_w_up, l)
        w_d = _cast_layer(exp_w_down, l)

        hc, xg_c = _pre(hc, mod_c, norm1_g[l], w_s5, ctx_len, moe=moe_c, pmod=pmod_c)
        y_c, fin_c = _s5(xg_c, mats, None, nb, want_y=not last)
        h, xg = _pre(h, mod, norm1_g[l], w_s5, seq, moe=moe, pmod=pmod)
        y, _ = _s5(xg, mats, fin_c, nb, want_y=True)
        yc = _s5out(y, glu_wt, glu_bc, nb)
        h, *routed = _mix(h, yc, mod, lw, seq, GRID_W, router_wt, router_b)
        if last:
            moe = _moe([routed], w_g, w_u, w_d)
        else:
            yc_c = _s5out(y_c, glu_wt, glu_bc, nb)
            hc, *routed_c = _mix(hc, yc_c, mod_c, lw, ctx_len, ctx_len, router_wt, router_b)
            moe_all = _moe([routed, routed_c], w_g, w_u, w_d)
            moe, moe_c = moe_all[:nb * seq], moe_all[nb * seq:]
    out = _final(h, moe, mod, final_norm_g, seq)
    return out.reshape(nb, seq, D_MODEL)
```

```python
import functools

import jax
import jax.numpy as jnp
from jax import lax
from jax.experimental import pallas as pl
from jax.experimental.pallas import tpu as pltpu

F32 = jnp.float32
BF16 = jnp.bfloat16
HIGHEST = lax.Precision.HIGHEST

D_MODEL = 1024
DEPTH = 2
EPS = 1e-6
W_MIX = 512
N_BRANCH = 3
CHUNK = 128
SGU_HEADS = 4
SGU_HEAD_W = W_MIX // SGU_HEADS
GRID_W = 64
S5_H = 16
S5_G = W_MIX // S5_H
S5_P = 64
S5_LC = 16
S5_W = S5_LC * S5_H
S5_COL0 = 5 * W_MIX
GATE0 = 6 * W_MIX
D_IN = GATE0 + N_BRANCH * D_MODEL
N_EXPERTS = 16
EXPERTS_PER_GROUP = 4
N_GROUPS = 4
D_FF = 512
N_PAIRS = 6
N_CLASSES = N_GROUPS * N_PAIRS
CLASS_ROWS = 32
N_MOD = 6
LANES = 128

TM_TOK = 512
TM_MOE = 256
PRE_B_PER_DOT = 4
CAST_BLOCK_ELEMS = 1 << 20
VMEM_LIMIT = 56 * 1024 * 1024

_NT = (((1,), (1,)), ((), ()))
_TN = (((0,), (0,)), ((), ()))


def _rms(x, g):
    return x * lax.rsqrt(jnp.mean(x * x, axis=-1, keepdims=True) + EPS) * g


def _bdot(a, b):
    return jnp.dot(a.astype(BF16), b, preferred_element_type=F32)


def _const_spec(shape):
    nd = len(shape)
    return pl.BlockSpec(shape, lambda *_: (0,) * nd, pipeline_mode=pl.Buffered(1))


def _params(*sem):
    return pltpu.CompilerParams(dimension_semantics=sem, vmem_limit_bytes=VMEM_LIMIT)


def _cast_kernel(x_ref, o_ref):
    o_ref[...] = x_ref[0].astype(BF16)


def _cast_layer(a, l):
    cols = a.shape[-1]
    rows = a[0].size // cols
    tr = 1 << ((CAST_BLOCK_ELEMS // cols).bit_length() - 1)
    while rows % tr:
        tr //= 2
    out = pl.pallas_call(
        _cast_kernel,
        grid=(rows // tr,),
        in_specs=[pl.BlockSpec((1, tr, cols), lambda i: (l, i, 0))],
        out_specs=pl.BlockSpec((tr, cols), lambda i: (i, 0)),
        out_shape=jax.ShapeDtypeStruct((rows, cols), BF16),
        compiler_params=_params("arbitrary"),
        name="cast",
    )(a.reshape(a.shape[0], rows, cols))
    return out.reshape(a.shape[1:])


def _adaln_kernel(c_ref, w_ref, b_ref, o_ref):
    c = c_ref[...]
    s = c * jax.nn.sigmoid(c)
    o_ref[0] = jnp.dot(s, w_ref[0], precision=HIGHEST, preferred_element_type=F32) + b_ref[0]


def _adaln(cvec, w_mod, b_mod):
    rows = cvec.shape[0]
    tn = 1536
    n_mod = N_MOD * D_MODEL
    return pl.pallas_call(
        _adaln_kernel,
        grid=(DEPTH, n_mod // tn),
        in_specs=[
            pl.BlockSpec((rows, D_MODEL), lambda l, j: (0, 0)),
            pl.BlockSpec((1, D_MODEL, tn), lambda l, j: (l, 0, j)),
            pl.BlockSpec((1, 1, tn), lambda l, j: (l, 0, j)),
        ],
        out_specs=pl.BlockSpec((1, rows, tn), lambda l, j: (l, 0, j)),
        out_shape=jax.ShapeDtypeStruct((DEPTH, rows, n_mod), F32),
        compiler_params=_params("arbitrary", "arbitrary"),
        name="adaln",
    )(cvec, w_mod, b_mod.reshape(DEPTH, 1, n_mod))


def _pre_kernel(*refs, residual):
    if residual:
        h_ref, moe_ref, pmod_ref, mod_ref, g_ref, w_ref, hnew_ref, xo_ref = refs[:8]
    else:
        h_ref, mod_ref, g_ref, w_ref, xo_ref = refs[:5]
    cu_scr = refs[-W_MIX // LANES:]
    nb, tt, _ = h_ref.shape
    for b0 in range(0, nb, PRE_B_PER_DOT):
        bs = slice(b0, b0 + PRE_B_PER_DOT)
        x = h_ref[bs]
        if residual:
            x = x + pmod_ref[bs, 5:6, :] * moe_ref[bs].astype(F32)
            hnew_ref[bs] = x
        xn = _rms(x, g_ref[...]) * (1.0 + mod_ref[bs, 1:2, :]) + mod_ref[bs, 0:1, :]
        cu = _bdot(xn.reshape(PRE_B_PER_DOT * tt, D_MODEL), w_ref[...])
        for lb, scr in enumerate(cu_scr):
            scr[b0 * tt:(b0 + PRE_B_PER_DOT) * tt, :] = cu[:, lb * LANES:(lb + 1) * LANES]
    groups_per_block = LANES // S5_H
    for i in range(S5_LC):
        for lb, scr in enumerate(cu_scr):
            piece = jnp.concatenate(
                [scr[pl.ds(c * S5_LC + i, nb, stride=tt), :] for c in range(tt // S5_LC)], axis=0)
            xo_ref[lb * groups_per_block:(lb + 1) * groups_per_block, i * S5_H:(i + 1) * S5_H, :] = (
                piece.T.astype(BF16).reshape(groups_per_block, S5_H, LANES))


def _pre(h, mod, norm_g, w_s5, seq, moe=None, moe_row0=0, pmod=None):
    t = h.shape[0]
    nb = t // seq
    tt = (LANES // nb) * S5_LC
    view = lambda a: a.reshape(a.shape[0] // seq, seq, D_MODEL)
    tok = pl.BlockSpec((nb, tt, D_MODEL), lambda k: (0, k, 0))
    modspec = _const_spec((nb, N_MOD, D_MODEL))
    residual = moe is not None
    in_specs = [tok]
    args = [view(h)]
    if residual:
        moe_blk = moe_row0 // t
        in_specs += [pl.BlockSpec((nb, tt, D_MODEL), lambda k: (moe_blk, k, 0)), modspec]
        args += [view(moe), pmod]
    in_specs += [modspec, _const_spec((1, D_MODEL)), _const_spec((D_MODEL, W_MIX))]
    args += [mod, norm_g.reshape(1, D_MODEL), w_s5]
    xo_spec = pl.BlockSpec((S5_G, S5_W, LANES), lambda k: (0, 0, k))
    xo_shape = jax.ShapeDtypeStruct((S5_G, S5_W, t // S5_LC), BF16)
    if residual:
        out_specs = [tok, xo_spec]
        out_shape = [jax.ShapeDtypeStruct((nb, seq, D_MODEL), F32), xo_shape]
    else:
        out_specs = xo_spec
        out_shape = xo_shape
    out = pl.pallas_call(
        functools.partial(_pre_kernel, residual=residual),
        grid=(seq // tt,),
        in_specs=in_specs,
        out_specs=out_specs,
        out_shape=out_shape,
        scratch_shapes=[pltpu.VMEM((nb * tt, LANES), F32)] * (W_MIX // LANES),
        compiler_params=_params("arbitrary"),
        name="pre_res" if residual else "pre",
    )(*args)
    if residual:
        return out[0].reshape(t, D_MODEL), out[1]
    return h, out


def _s5_mats(lam_re, lam_im, log_dt, b_re, b_im, c_re, c_im, dvec):
    dt = jnp.exp(log_dt)[..., None]
    mag = jnp.exp(lam_re * dt)
    ab_re = mag * jnp.cos(lam_im * dt)
    ab_im = mag * jnp.sin(lam_im * dt)
    den = lam_re * lam_re + lam_im * lam_im
    nr = ab_re - 1.0
    f_re = (nr * lam_re + ab_im * lam_im) / den
    f_im = (ab_im * lam_re - nr * lam_im) / den
    bb_re = f_re[..., None] * b_re - f_im[..., None] * b_im
    bb_im = f_re[..., None] * b_im + f_im[..., None] * b_re
    k = jnp.arange(S5_LC + 1, dtype=F32)[:, None, None, None]
    pmag = jnp.exp(lam_re * dt * k)
    pw_re = pmag * jnp.cos(lam_im * dt * k)
    pw_im = pmag * jnp.sin(lam_im * dt * k)
    cp_re = c_re[None] * pw_re[:, :, :, None, :] - c_im[None] * pw_im[:, :, :, None, :]
    cp_im = c_re[None] * pw_im[:, :, :, None, :] + c_im[None] * pw_re[:, :, :, None, :]
    kk = jnp.einsum("kdghp,dgpj->kdghj",
                    jnp.concatenate([cp_re[:S5_LC], -cp_im[:S5_LC]], axis=-1),
                    jnp.concatenate([bb_re, bb_im], axis=-2))
    kf, kb = kk[:, 0], kk[:, 1]
    dmat = jnp.eye(S5_H, dtype=F32)[None] * dvec.reshape(S5_G, S5_H)[:, :, None]
    table = jnp.concatenate(
        [kf[:0:-1], (kf[0] + kb[0] + dmat)[None], kb[1:], jnp.zeros_like(kf[:1])], axis=0)
    table = jnp.transpose(table, (1, 2, 0, 3)).reshape(S5_G, S5_H, 2 * S5_W)
    tt = jnp.stack(
        [table[:, :, (S5_LC - 1 - j) * S5_H:(S5_LC - 1 - j) * S5_H + S5_W] for j in range(S5_LC)],
        axis=1).reshape(S5_G, S5_W, S5_W)

    rev = jnp.arange(S5_LC - 1, -1, -1)
    fwd = jnp.arange(S5_LC)

    def lanes4(f_re_part, b_re_part, f_im_part, b_im_part):
        return jnp.concatenate([f_re_part, b_re_part, f_im_part, b_im_part], axis=-1)

    def step_major(a):
        return jnp.transpose(a, (1, 0, 2))

    def chan_major(a):
        return jnp.transpose(a, (0, 2, 1))

    pr_f, pi_f = step_major(pw_re[rev, 0]), step_major(pw_im[rev, 0])
    pr_b, pi_b = step_major(pw_re[fwd, 1]), step_major(pw_im[fwd, 1])
    br_f, bi_f = chan_major(bb_re[0]), chan_major(bb_im[0])
    br_b, bi_b = chan_major(bb_re[1]), chan_major(bb_im[1])
    inj = (lanes4(pr_f, pr_b, pr_f, pr_b)[:, :, None, :] * lanes4(br_f, br_b, bi_f, bi_b)[:, None]
           + lanes4(-pi_f, -pi_b, pi_f, pi_b)[:, :, None, :] * lanes4(bi_f, bi_b, br_f, br_b)[:, None]
           ).reshape(S5_G, S5_W, 4 * S5_P)
    qr_f, qi_f = step_major(pw_re[fwd + 1, 0]), step_major(pw_im[fwd + 1, 0])
    qr_b, qi_b = step_major(pw_re[S5_LC - fwd, 1]), step_major(pw_im[S5_LC - fwd, 1])
    rt = (lanes4(qr_f, qr_b, -qi_f, -qi_b)[:, :, None, :]
          * lanes4(c_re[0], c_re[1], c_re[0], c_re[1])[:, None]
          + lanes4(-qi_f, -qi_b, -qr_f, -qr_b)[:, :, None, :]
          * lanes4(c_im[0], c_im[1], c_im[0], c_im[1])[:, None]).reshape(S5_G, S5_W, 4 * S5_P)
    a_re, a_im = pw_re[S5_LC], pw_im[S5_LC]
    pvec = jnp.concatenate([a_re[0], a_re[1]], axis=-1)
    qvec = jnp.concatenate([a_im[0], a_im[1]], axis=-1)
    pq = jnp.stack([pvec, qvec], axis=1)
    return tt.astype(BF16), inj.astype(BF16), rt.astype(BF16), pq


def _s5_kernel(*refs, n_chunks, n_batch, want_y):
    if want_y:
        x_ref, tt_ref, inj_ref, rt_ref, pq_ref, h0_ref, y_ref, fin_ref = refs[:8]
    else:
        x_ref, inj_ref, pq_ref, h0_ref, fin_ref = refs[:5]
    i_re_scr, i_im_scr, sf_re_scr, sf_im_scr, sb_re_scr, sb_im_scr = refs[-6:]
    half = 2 * S5_P
    x = x_ref[0]
    inj = lax.dot_general(x, inj_ref[0], _TN, preferred_element_type=F32)
    i_re_scr[...] = inj[:, 0:half]
    i_im_scr[...] = inj[:, half:]
    p = pq_ref[0, 0:1, :]
    q = pq_ref[0, 1:2, :]
    fwd_lanes = lax.broadcasted_iota(jnp.int32, (n_batch, half), 1) < S5_P
    h0 = h0_ref[0]
    s_re, s_im = h0[:, 0:half], h0[:, half:]
    for k in range(n_chunks):
        rows_f = slice(k * n_batch, (k + 1) * n_batch)
        rows_b = slice((n_chunks - 1 - k) * n_batch, (n_chunks - k) * n_batch)
        sf_re_scr[rows_f, :] = s_re
        sf_im_scr[rows_f, :] = s_im
        sb_re_scr[rows_b, :] = s_re
        sb_im_scr[rows_b, :] = s_im
        i_re = jnp.where(fwd_lanes, i_re_scr[rows_f, :], i_re_scr[rows_b, :])
        i_im = jnp.where(fwd_lanes, i_im_scr[rows_f, :], i_im_scr[rows_b, :])
        s_re, s_im = p * s_re - q * s_im + i_re, p * s_im + q * s_re + i_im
    fin_ref[0] = jnp.concatenate([s_re, s_im], axis=1)
    if want_y:
        fwd = lax.broadcasted_iota(jnp.int32, (1, half), 1) < S5_P
        s_prev = jnp.concatenate(
            [jnp.where(fwd, sf_re_scr[...], sb_re_scr[...]),
             jnp.where(fwd, sf_im_scr[...], sb_im_scr[...])], axis=1).astype(BF16)
        y = jnp.dot(tt_ref[0], x, preferred_element_type=F32)
        y = y + lax.dot_general(rt_ref[0], s_prev, _NT, preferred_element_type=F32)
        y_ref[0] = y


def _s5(xg, mats, h0, n_batch, want_y):
    tt, inj, rt, pq = mats
    n = xg.shape[2]
    nc = n // n_batch
    if h0 is None:
        h0 = jnp.zeros((S5_G, n_batch, S5_W), F32)
    grp = lambda shape: pl.BlockSpec((1,) + shape, lambda g: (g, 0, 0))
    sq = grp((S5_W, S5_W))
    if want_y:
        in_specs = [grp((S5_W, n)), sq, sq, sq]
        args = [xg, tt, inj, rt]
    else:
        in_specs = [grp((S5_W, n)), sq]
        args = [xg, inj]
    in_specs += [grp((2, 2 * S5_P)), grp((n_batch, S5_W))]
    args += [pq, h0]
    fin_spec = grp((n_batch, S5_W))
    fin_shape = jax.ShapeDtypeStruct((S5_G, n_batch, S5_W), F32)
    if want_y:
        out_specs = [grp((S5_W, n)), fin_spec]
        out_shape = [jax.ShapeDtypeStruct((S5_G, S5_W, n), F32), fin_shape]
    else:
        out_specs = fin_spec
        out_shape = fin_shape
    out = pl.pallas_call(
        functools.partial(_s5_kernel, n_chunks=nc, n_batch=n_batch, want_y=want_y),
        grid=(S5_G,),
        in_specs=in_specs,
        out_specs=out_specs,
        out_shape=out_shape,
        scratch_shapes=[pltpu.VMEM((n, 2 * S5_P), F32)] * 6,
        compiler_params=_params("arbitrary"),
        name="s5" if want_y else "s5_states",
    )(*args)
    if not want_y:
        return None, out
    return out[0], out[1]


def _s5out_kernel(y_ref, gwt_ref, gb_ref, o_ref, *tok_scr):
    nb, tt, _ = o_ref.shape
    for j in range(S5_LC):
        z = jax.nn.gelu(y_ref[:, j * S5_H:(j + 1) * S5_H, :].reshape(W_MIX, LANES))
        gate = jnp.dot(gwt_ref[...], z.astype(BF16), preferred_element_type=F32) + gb_ref[...]
        yc = z * jax.nn.sigmoid(gate)
        for lb, scr in enumerate(tok_scr):
            piece = yc[lb * LANES:(lb + 1) * LANES, :].T
            for c in range(tt // S5_LC):
                scr[pl.ds(c * S5_LC + j, nb, stride=tt), :] = piece[c * nb:(c + 1) * nb, :]
    for lb, scr in enumerate(tok_scr):
        o_ref[:, :, lb * LANES:(lb + 1) * LANES] = scr[...].reshape(nb, tt, LANES).astype(BF16)


def _s5out(y, glu_wt, glu_b, n_batch):
    n = y.shape[2]
    seq = (n // n_batch) * S5_LC
    tt = (LANES // n_batch) * S5_LC
    out = pl.pallas_call(
        _s5out_kernel,
        grid=(seq // tt,),
        in_specs=[
            pl.BlockSpec((S5_G, S5_W, LANES), lambda k: (0, 0, k)),
            _const_spec((W_MIX, W_MIX)),
            _const_spec((W_MIX, 1)),
        ],
        out_specs=pl.BlockSpec((n_batch, tt, W_MIX), lambda k: (0, k, 0)),
        out_shape=jax.ShapeDtypeStruct((n_batch, seq, W_MIX), BF16),
        scratch_shapes=[pltpu.VMEM((n_batch * tt, LANES), F32)] * (W_MIX // LANES),
        compiler_params=_params("arbitrary"),
        name="s5out",
    )(y, glu_wt, glu_b)
    return out.reshape(n_batch * seq, W_MIX)


def _route(logits_t, rb_ref):
    s = [jax.nn.sigmoid(logits_t[e:e + 1, :]) for e in range(N_EXPERTS)]
    bz = [s[e] + rb_ref[e] for e in range(N_EXPERTS)]
    gs = []
    for g in range(N_GROUPS):
        v = bz[g * EXPERTS_PER_GROUP:(g + 1) * EXPERTS_PER_GROUP]
        best = None
        for i in range(EXPERTS_PER_GROUP):
            for j in range(i + 1, EXPERTS_PER_GROUP):
                pair = v[i] + v[j]
                best = pair if best is None else jnp.maximum(best, pair)
        gs.append(best)
    gbest = gs[0]
    gi = jnp.zeros_like(gbest, dtype=jnp.int32)
    for g in range(1, N_GROUPS):
        upd = gs[g] > gbest
        gi = jnp.where(upd, g, gi)
        gbest = jnp.where(upd, gs[g], gbest)

    def pick(vals, k):
        out = vals[k]
        for g in range(1, N_GROUPS):
            out = jnp.where(gi == g, vals[g * EXPERTS_PER_GROUP + k], out)
        return out

    v = [pick(bz, k) for k in range(EXPERTS_PER_GROUP)]
    sc = [pick(s, k) for k in range(EXPERTS_PER_GROUP)]

    def argmax4(vals):
        best, idx = vals[0], jnp.zeros_like(gi)
        for k in range(1, EXPERTS_PER_GROUP):
            upd = vals[k] > best
            idx = jnp.where(upd, k, idx)
            best = jnp.where(upd, vals[k], best)
        return idx

    i1 = argmax4(v)
    v2 = [jnp.where(i1 == k, -jnp.inf, v[k]) for k in range(EXPERTS_PER_GROUP)]
    i2 = argmax4(v2)

    def sel(vals, idx):
        out = vals[0]
        for k in range(1, EXPERTS_PER_GROUP):
            out = jnp.where(idx == k, vals[k], out)
        return out

    w1 = sel(sc, i1)
    w2 = sel(sc, i2)
    tot = w1 + w2
    w1 = w1 / tot
    w2 = w2 / tot
    first_low = i1 < i2
    lo = jnp.where(first_low, i1, i2)
    hi = jnp.where(first_low, i2, i1)
    wa = jnp.where(first_low, w1, w2)
    wb = jnp.where(first_low, w2, w1)
    base = gi * EXPERTS_PER_GROUP
    pair = jnp.where(lo == 0, 0, jnp.where(lo == 1, 3, 5)) + (hi - lo - 1)
    cls = gi * N_PAIRS + pair
    return (base + lo).astype(F32), (base + hi).astype(F32), wa, wb, cls


def _mix_kernel(*refs, conv_w, n_in, n_shared):
    (h_ref, yc_ref, mod_ref, n1_ref, n2_ref, win_ref, sgug_ref, sguw_ref, sgub_ref,
     convw_ref, wbr_ref, wout_ref, rwt_ref, tri_ref, rb_ref) = refs[:n_in]
    h2_ref, hn2_ref, route_ref, counts_ref, ya_scr, count_scr = refs[n_in + n_shared:]
    tm = h_ref.shape[0]
    x = h_ref[...]
    xn = _rms(x, n1_ref[...]) * (1.0 + mod_ref[0, 1:2, :]) + mod_ref[0, 0:1, :]
    xb = xn.astype(BF16)

    def proj(c0, width):
        return jnp.dot(xb, win_ref[:, c0:c0 + width], preferred_element_type=F32)

    def gate(branch):
        return jax.nn.sigmoid(proj(GATE0 + branch * D_MODEL, D_MODEL))

    u = jax.nn.gelu(proj(0, W_MIX))
    v = jax.nn.gelu(proj(W_MIX, W_MIX))
    vc = v - jnp.mean(v, axis=-1, keepdims=True)
    v = vc * lax.rsqrt(jnp.mean(vc * vc, axis=-1, keepdims=True) + EPS) * sgug_ref[...]
    vb = v.astype(BF16)
    for ck in range(tm // CHUNK):
        r0 = ck * CHUNK
        for hh in range(SGU_HEADS):
            c0 = hh * SGU_HEAD_W
            mixed = jnp.dot(sguw_ref[hh], vb[r0:r0 + CHUNK, c0:c0 + SGU_HEAD_W],
                            preferred_element_type=F32) + sgub_ref[:, hh:hh + 1]
            ya_scr[r0:r0 + CHUNK, c0:c0 + SGU_HEAD_W] = (
                u[r0:r0 + CHUNK, c0:c0 + SGU_HEAD_W] * mixed).astype(BF16)
    merged = gate(0) * jnp.dot(ya_scr[...], wbr_ref[0], preferred_element_type=F32)

    m = proj(3 * W_MIX, W_MIX) * proj(4 * W_MIX, W_MIX)
    pos = lax.broadcasted_iota(jnp.int32, (tm, 1), 0) % conv_w
    prev = jnp.where(pos == 0, 0.0, pltpu.roll(m, 1, axis=0))
    nxt = jnp.where(pos == conv_w - 1, 0.0, pltpu.roll(m, tm - 1, axis=0))
    yb = proj(2 * W_MIX, W_MIX) * (
        convw_ref[0:1, :] * prev + convw_ref[1:2, :] * m + convw_ref[2:3, :] * nxt)
    merged = merged + gate(1) * _bdot(yb, wbr_ref[1])

    merged = merged + gate(2) * jnp.dot(yc_ref[...], wbr_ref[2], preferred_element_type=F32)

    h2 = x + mod_ref[0, 2:3, :] * _bdot(merged, wout_ref[...])
    h2_ref[...] = h2
    hn2 = _rms(h2, n2_ref[...]) * (1.0 + mod_ref[0, 4:5, :]) + mod_ref[0, 3:4, :]
    hn2_ref[...] = hn2.astype(BF16)
    logits_t = lax.dot_general(rwt_ref[...], hn2, _NT, precision=HIGHEST, preferred_element_type=F32)
    ea, eb, wa, wb, cls = _route(logits_t, rb_ref)

    @pl.when(pl.program_id(0) == 0)
    def _():
        count_scr[...] = jnp.zeros_like(count_scr)

    onehot = lax.broadcasted_iota(jnp.int32, (CLASS_ROWS, tm), 0) == cls
    within = jnp.dot(onehot.astype(BF16), tri_ref[...], preferred_element_type=F32)
    before = count_scr[:, 0:1]
    rank = jnp.sum(jnp.where(onehot, within + before, 0.0), axis=0, keepdims=True) - 1.0
    total = before + within[:, tm - 1:tm]
    count_scr[...] = jnp.broadcast_to(total, count_scr.shape)
    counts_ref[...] = jnp.broadcast_to(total, counts_ref.shape)
    zero = jnp.zeros_like(wa)
    route_ref[...] = jnp.concatenate([ea, eb, wa, wb, rank, cls.astype(F32), zero, zero], axis=0)


def _mix(h, yc, mod, lw, seq, conv_w, router_wt, router_b, total_rows, row0=0, shared=None):
    t = h.shape[0]
    tm = min(TM_TOK, seq)
    per_seq = seq // tm
    tok = pl.BlockSpec((tm, D_MODEL), lambda i: (i, 0))
    in_specs = [
        tok,
        pl.BlockSpec((tm, W_MIX), lambda i: (i, 0)),
        pl.BlockSpec((1, N_MOD, D_MODEL), lambda i: (i // per_seq, 0, 0)),
        _const_spec((1, D_MODEL)),
        _const_spec((1, D_MODEL)),
        _const_spec((D_MODEL, D_IN)),
        _const_spec((1, W_MIX)),
        _const_spec((SGU_HEADS, CHUNK, CHUNK)),
        _const_spec((CHUNK, SGU_HEADS)),
        _const_spec((3, W_MIX)),
        _const_spec((N_BRANCH, W_MIX, D_MODEL)),
        _const_spec((D_MODEL, D_MODEL)),
        _const_spec((N_EXPERTS, D_MODEL)),
        _const_spec((tm, tm)),
        pl.BlockSpec(memory_space=pltpu.SMEM),
    ]
    blk0 = row0 // tm
    out_specs = [tok, pl.BlockSpec((tm, D_MODEL), lambda i: (blk0 + i, 0)),
                 pl.BlockSpec((8, tm), lambda i: (0, blk0 + i)),
                 pl.BlockSpec((CLASS_ROWS, LANES), lambda i: (0, 0))]
    out_shape = [
        jax.ShapeDtypeStruct((t, D_MODEL), F32),
        jax.ShapeDtypeStruct((total_rows, D_MODEL), BF16),
        jax.ShapeDtypeStruct((8, total_rows), F32),
        jax.ShapeDtypeStruct((CLASS_ROWS, LANES), F32),
    ]
    tri = (jnp.arange(tm)[:, None] <= jnp.arange(tm)[None, :]).astype(BF16)
    args = [h, yc, mod, lw["norm1_g"], lw["norm2_g"], lw["w_in"], lw["sgu_norm_g"], lw["sgu_w"],
            lw["sgu_b_t"], lw["conv_w"], lw["w_branch"], lw["w_out"], router_wt, tri, router_b]
    n_in = len(args)
    aliases = {}
    if shared is not None:
        in_specs += [pl.BlockSpec(memory_space=pl.ANY)] * 2
        args += list(shared)
        aliases = {n_in: 1, n_in + 1: 2}
    h2, hn2, route, counts = pl.pallas_call(
        functools.partial(_mix_kernel, conv_w=conv_w, n_in=n_in, n_shared=len(aliases)),
        grid=(t // tm,),
        in_specs=in_specs,
        out_specs=out_specs,
        out_shape=out_shape,
        scratch_shapes=[pltpu.VMEM((tm, W_MIX), BF16), pltpu.VMEM((CLASS_ROWS, LANES), F32)],
        input_output_aliases=aliases,
        compiler_params=_params("arbitrary"),
        name="mix",
    )(*args)
    return h2, hn2, route, counts[:N_CLASSES, 0].astype(jnp.int32)


def _moe_kernel(ea_ref, eb_ref, valid_ref, x_ref, w_ref, ga_ref, ua_ref, da_ref,
                gb_ref, ub_ref, db_ref, o_ref):
    i = pl.program_id(0)

    @pl.when(valid_ref[i] > 0)
    def _():
        x = x_ref[...]

        def expert(g_ref, u_ref, d_ref):
            g = jnp.dot(x, g_ref[0], preferred_element_type=F32)
            act = g * jax.nn.sigmoid(g) * jnp.dot(x, u_ref[0], preferred_element_type=F32)
            return _bdot(act, d_ref[0])

        o_ref[...] = (w_ref[:, 1:2] * expert(ga_ref, ua_ref, da_ref)
                      + w_ref[:, 2:3] * expert(gb_ref, ub_ref, db_ref)).astype(BF16)

    @pl.when(valid_ref[i] == 0)
    def _():
        o_ref[...] = jnp.zeros_like(o_ref)


def _moe(hn2, route, part_rows, part_counts, w_g, w_u, w_d):
    t = hn2.shape[0]
    cls = route[5].astype(jnp.int32)
    rank = route[4].astype(jnp.int32)
    row0 = 0
    for rows, cnt in zip(part_rows[:-1], part_counts[:-1]):
        row0 += rows
        rank = rank + jnp.where(jnp.arange(t, dtype=jnp.int32) >= row0, cnt[cls], 0)
    counts = sum(part_counts[1:], part_counts[0])
    tm = TM_MOE
    nt = t // tm + N_CLASSES
    tp = nt * tm
    padded = ((counts + tm - 1) // tm) * tm
    ends = jnp.cumsum(padded)
    offs = ends - padded
    pos = offs[cls] + rank
    starts = jnp.arange(nt, dtype=jnp.int32) * tm
    tile_cls = jnp.sum((starts[:, None] >= ends[None, :]).astype(jnp.int32), axis=1)
    valid = (tile_cls < N_CLASSES).astype(jnp.int32)
    tile_cls = jnp.minimum(tile_cls, N_CLASSES - 1)
    pair_lo = jnp.array([0, 0, 0, 1, 1, 2], jnp.int32)
    pair_hi = jnp.array([1, 2, 3, 2, 3, 3], jnp.int32)
    tile_ea = (tile_cls // N_PAIRS) * EXPERTS_PER_GROUP + pair_lo[tile_cls % N_PAIRS]
    tile_eb = (tile_cls // N_PAIRS) * EXPERTS_PER_GROUP + pair_hi[tile_cls % N_PAIRS]
    slot = jnp.stack([jnp.arange(t, dtype=F32), route[2], route[3], jnp.zeros((t,), F32)], axis=1)
    spread = (jnp.arange(tp, dtype=jnp.int32) % t).astype(F32)
    empty = jnp.stack([spread] + [jnp.zeros((tp,), F32)] * 3, axis=1)
    ws = empty.at[pos].set(slot, mode="promise_in_bounds", unique_indices=True)
    xs = hn2.at[ws[:, 0].astype(jnp.int32)].get(mode="promise_in_bounds")

    wspec_in = lambda sel: pl.BlockSpec((1, D_MODEL, D_FF), sel)
    wspec_out = lambda sel: pl.BlockSpec((1, D_FF, D_MODEL), sel)
    sel_a = lambda i, a, b, v: (a[i], 0, 0)
    sel_b = lambda i, a, b, v: (b[i], 0, 0)
    grid_spec = pltpu.PrefetchScalarGridSpec(
        num_scalar_prefetch=3,
        grid=(nt,),
        in_specs=[
            pl.BlockSpec((tm, D_MODEL), lambda i, a, b, v: (i, 0)),
            pl.BlockSpec((tm, 4), lambda i, a, b, v: (i, 0)),
            wspec_in(sel_a), wspec_in(sel_a), wspec_out(sel_a),
            wspec_in(sel_b), wspec_in(sel_b), wspec_out(sel_b),
        ],
        out_specs=pl.BlockSpec((tm, D_MODEL), lambda i, a, b, v: (i, 0)),
    )
    out = pl.pallas_call(
        _moe_kernel,
        grid_spec=grid_spec,
        out_shape=jax.ShapeDtypeStruct((tp, D_MODEL), BF16),
        compiler_params=_params("arbitrary"),
        name="moe",
    )(tile_ea, tile_eb, valid, xs, ws, w_g, w_u, w_d, w_g, w_u, w_d)
    return out.at[pos].get(mode="promise_in_bounds", unique_indices=True)


def _final_kernel(h_ref, moe_ref, mod_ref, g_ref, o_ref):
    x = h_ref[...] + mod_ref[0, 5:6, :] * moe_ref[...].astype(F32)
    o_ref[...] = _rms(x, g_ref[...])


def _final(h, moe, mod, g, seq):
    t = h.shape[0]
    tm = TM_TOK
    per_seq = seq // tm
    tok = pl.BlockSpec((tm, D_MODEL), lambda i: (i, 0))
    return pl.pallas_call(
        _final_kernel,
        grid=(t // tm,),
        in_specs=[tok, tok, pl.BlockSpec((1, N_MOD, D_MODEL), lambda i: (i // per_seq, 0, 0)),
                  _const_spec((1, D_MODEL))],
        out_specs=tok,
        out_shape=jax.ShapeDtypeStruct((t, D_MODEL), F32),
        compiler_params=_params("arbitrary"),
        name="final",
    )(h, moe, mod, g.reshape(1, D_MODEL))


def kernel(x, c, ctx, c_ctx, w_mod, b_mod, norm1_g, norm2_g, w_in, sgu_norm_g, sgu_w, sgu_b,
           conv_w, s5_lam_re, s5_lam_im, s5_log_dt, s5_b_re, s5_b_im, s5_c_re, s5_c_im, s5_d,
           glu_w, glu_b, w_branch, w_out, router_w, router_b, exp_w_gate, exp_w_up, exp_w_down,
           final_norm_g):
    nb, seq, _ = x.shape
    ctx_len = ctx.shape[1]
    h = x.reshape(nb * seq, D_MODEL)
    hc = ctx.reshape(nb * ctx_len, D_MODEL)

    mod_rows = 24
    cvec = jnp.zeros((mod_rows, D_MODEL), F32).at[:nb].set(c).at[nb].set(c_ctx)
    mods = _adaln(cvec, w_mod, b_mod).reshape(DEPTH, mod_rows, N_MOD, D_MODEL)
    router_wt = router_w.T

    all_mats = jax.vmap(_s5_mats)(s5_lam_re, s5_lam_im, s5_log_dt, s5_b_re, s5_b_im,
                                  s5_c_re, s5_c_im, s5_d)
    n_lat, n_ctx = nb * seq, nb * ctx_len
    moe = None
    mod = mod_c = None
    for l in range(DEPTH):
        last = l == DEPTH - 1
        pmod, pmod_c = mod, mod_c
        mod = mods[l, :nb]
        mod_c = jnp.broadcast_to(mods[l, nb][None], (nb, N_MOD, D_MODEL))
        w_in_b = _cast_layer(w_in, l)
        w_s5 = w_in_b[:, S5_COL0:S5_COL0 + W_MIX]
        lw = {
            "norm1_g": norm1_g[l].reshape(1, D_MODEL),
            "norm2_g": norm2_g[l].reshape(1, D_MODEL),
            "w_in": w_in_b,
            "sgu_norm_g": sgu_norm_g[l].reshape(1, W_MIX),
            "sgu_w": sgu_w[l].astype(BF16),
            "sgu_b_t": sgu_b[l].T,
            "conv_w": conv_w[l],
            "w_branch": _cast_layer(w_branch, l),
            "w_out": _cast_layer(w_out, l),
        }
        glu_wt = glu_w[l].T.astype(BF16)
        glu_bc = glu_b[l].reshape(W_MIX, 1)
        mats = tuple(m[l] for m in all_mats)
        w_g = _cast_layer(exp_w_gate, l)
        w_u = _cast_layer(exp_w_up, l)
        w_d = _cast_layer(exp_w_down, l)

        hc, xg_c = _pre(hc, mod_c, norm1_g[l], w_s5, ctx_len, moe=moe, moe_row0=n_lat, pmod=pmod_c)
        y_c, fin_c = _s5(xg_c, mats, None, nb, want_y=not last)
        h, xg = _pre(h, mod, norm1_g[l], w_s5, seq, moe=moe, pmod=pmod)
        y, _ = _s5(xg, mats, fin_c, nb, want_y=True)
        yc = _s5out(y, glu_wt, glu_bc, nb)
        if last:
            h, hn2, route, cnt = _mix(h, yc, mod, lw, seq, GRID_W, router_wt, router_b, n_lat)
            moe = _moe(hn2, route, [n_lat], [cnt], w_g, w_u, w_d)
        else:
            n_all = n_lat + n_ctx
            h, hn2, route, cnt = _mix(h, yc, mod, lw, seq, GRID_W, router_wt, router_b, n_all)
            yc_c = _s5out(y_c, glu_wt, glu_bc, nb)
            hc, hn2, route, cnt_c = _mix(hc, yc_c, mod_c, lw, ctx_len, ctx_len, router_wt, router_b,
                                         n_all, row0=n_lat, shared=(hn2, route))
            moe = _moe(hn2, route, [n_lat, n_ctx], [cnt, cnt_c], w_g, w_u, w_d)
    out = _final(h, moe, mod, final_norm_g, seq)
    return out.reshape(nb, seq, D_MODEL)
```

```python
import functools

import jax
import jax.numpy as jnp
from jax import lax
from jax.experimental import pallas as pl
from jax.experimental.pallas import tpu as pltpu

F32 = jnp.float32
BF16 = jnp.bfloat16
HIGHEST = lax.Precision.HIGHEST

D_MODEL = 1024
DEPTH = 2
EPS = 1e-6
W_MIX = 512
N_BRANCH = 3
CHUNK = 128
SGU_HEADS = 4
SGU_HEAD_W = W_MIX // SGU_HEADS
GRID_W = 64
S5_H = 16
S5_G = W_MIX // S5_H
S5_P = 64
S5_LC = 16
S5_W = S5_LC * S5_H
S5_COL0 = 5 * W_MIX
GATE0 = 6 * W_MIX
D_IN = GATE0 + N_BRANCH * D_MODEL
N_EXPERTS = 16
EXPERTS_PER_GROUP = 4
N_GROUPS = 4
D_FF = 512
N_PAIRS = 6
N_CLASSES = N_GROUPS * N_PAIRS
CLASS_ROWS = 32
N_MOD = 6
LANES = 128

TM_TOK = 512
TM_MOE = 256
PRE_B_PER_DOT = 4
CAST_BLOCK_ELEMS = 1 << 19
VMEM_LIMIT = 56 * 1024 * 1024

_NT = (((1,), (1,)), ((), ()))
_TN = (((0,), (0,)), ((), ()))


def _rms(x, g):
    return x * lax.rsqrt(jnp.mean(x * x, axis=-1, keepdims=True) + EPS) * g


def _bdot(a, b):
    return jnp.dot(a.astype(BF16), b, preferred_element_type=F32)


def _const_spec(shape):
    nd = len(shape)
    return pl.BlockSpec(shape, lambda *_: (0,) * nd, pipeline_mode=pl.Buffered(1))


def _params(*sem):
    return pltpu.CompilerParams(dimension_semantics=sem, vmem_limit_bytes=VMEM_LIMIT)


def _cast_kernel(x_ref, o_ref):
    o_ref[...] = x_ref[0].astype(BF16)


def _cast_layer(a, l):
    cols = a.shape[-1]
    rows = a[0].size // cols
    tr = 1 << ((CAST_BLOCK_ELEMS // cols).bit_length() - 1)
    while rows % tr:
        tr //= 2
    out = pl.pallas_call(
        _cast_kernel,
        grid=(rows // tr,),
        in_specs=[pl.BlockSpec((1, tr, cols), lambda i: (l, i, 0))],
        out_specs=pl.BlockSpec((tr, cols), lambda i: (i, 0)),
        out_shape=jax.ShapeDtypeStruct((rows, cols), BF16),
        compiler_params=_params("arbitrary"),
        name="cast",
    )(a.reshape(a.shape[0], rows, cols))
    return out.reshape(a.shape[1:])


def _adaln_kernel(c_ref, w_ref, b_ref, o_ref):
    c = c_ref[...]
    s = c * jax.nn.sigmoid(c)
    o_ref[0] = jnp.dot(s, w_ref[0], precision=HIGHEST, preferred_element_type=F32) + b_ref[0]


def _adaln(cvec, w_mod, b_mod):
    rows = cvec.shape[0]
    tn = 1536
    n_mod = N_MOD * D_MODEL
    return pl.pallas_call(
        _adaln_kernel,
        grid=(DEPTH, n_mod // tn),
        in_specs=[
            pl.BlockSpec((rows, D_MODEL), lambda l, j: (0, 0)),
            pl.BlockSpec((1, D_MODEL, tn), lambda l, j: (l, 0, j)),
            pl.BlockSpec((1, 1, tn), lambda l, j: (l, 0, j)),
        ],
        out_specs=pl.BlockSpec((1, rows, tn), lambda l, j: (l, 0, j)),
        out_shape=jax.ShapeDtypeStruct((DEPTH, rows, n_mod), F32),
        compiler_params=_params("arbitrary", "arbitrary"),
        name="adaln",
    )(cvec, w_mod, b_mod.reshape(DEPTH, 1, n_mod))


def _pre_kernel(*refs, residual):
    if residual:
        h_ref, moe_ref, pmod_ref, mod_ref, g_ref, w_ref, hnew_ref, xo_ref = refs[:8]
    else:
        h_ref, mod_ref, g_ref, w_ref, xo_ref = refs[:5]
    cu_scr = refs[-W_MIX // LANES:]
    nb, tt, _ = h_ref.shape
    for b0 in range(0, nb, PRE_B_PER_DOT):
        bs = slice(b0, b0 + PRE_B_PER_DOT)
        x = h_ref[bs]
        if residual:
            x = x + pmod_ref[bs, 5:6, :] * moe_ref[bs].astype(F32)
            hnew_ref[bs] = x
        xn = _rms(x, g_ref[...]) * (1.0 + mod_ref[bs, 1:2, :]) + mod_ref[bs, 0:1, :]
        cu = _bdot(xn.reshape(PRE_B_PER_DOT * tt, D_MODEL), w_ref[...])
        for lb, scr in enumerate(cu_scr):
            scr[b0 * tt:(b0 + PRE_B_PER_DOT) * tt, :] = cu[:, lb * LANES:(lb + 1) * LANES]
    groups_per_block = LANES // S5_H
    for i in range(S5_LC):
        for lb, scr in enumerate(cu_scr):
            piece = jnp.concatenate(
                [scr[pl.ds(c * S5_LC + i, nb, stride=tt), :] for c in range(tt // S5_LC)], axis=0)
            xo_ref[lb * groups_per_block:(lb + 1) * groups_per_block, i * S5_H:(i + 1) * S5_H, :] = (
                piece.T.astype(BF16).reshape(groups_per_block, S5_H, LANES))


def _pre(h, mod, norm_g, w_s5, seq, moe=None, moe_row0=0, pmod=None):
    t = h.shape[0]
    nb = t // seq
    tt = (LANES // nb) * S5_LC
    view = lambda a: a.reshape(a.shape[0] // seq, seq, D_MODEL)
    tok = pl.BlockSpec((nb, tt, D_MODEL), lambda k: (0, k, 0))
    modspec = _const_spec((nb, N_MOD, D_MODEL))
    residual = moe is not None
    in_specs = [tok]
    args = [view(h)]
    if residual:
        moe_blk = moe_row0 // t
        in_specs += [pl.BlockSpec((nb, tt, D_MODEL), lambda k: (moe_blk, k, 0)), modspec]
        args += [view(moe), pmod]
    in_specs += [modspec, _const_spec((1, D_MODEL)), _const_spec((D_MODEL, W_MIX))]
    args += [mod, norm_g.reshape(1, D_MODEL), w_s5]
    xo_spec = pl.BlockSpec((S5_G, S5_W, LANES), lambda k: (0, 0, k))
    xo_shape = jax.ShapeDtypeStruct((S5_G, S5_W, t // S5_LC), BF16)
    if residual:
        out_specs = [tok, xo_spec]
        out_shape = [jax.ShapeDtypeStruct((nb, seq, D_MODEL), F32), xo_shape]
    else:
        out_specs = xo_spec
        out_shape = xo_shape
    out = pl.pallas_call(
        functools.partial(_pre_kernel, residual=residual),
        grid=(seq // tt,),
        in_specs=in_specs,
        out_specs=out_specs,
        out_shape=out_shape,
        scratch_shapes=[pltpu.VMEM((nb * tt, LANES), F32)] * (W_MIX // LANES),
        compiler_params=_params("arbitrary"),
        name="pre_res" if residual else "pre",
    )(*args)
    if residual:
        return out[0].reshape(t, D_MODEL), out[1]
    return h, out


def _s5_mats(lam_re, lam_im, log_dt, b_re, b_im, c_re, c_im, dvec):
    dt = jnp.exp(log_dt)[..., None]
    mag = jnp.exp(lam_re * dt)
    ab_re = mag * jnp.cos(lam_im * dt)
    ab_im = mag * jnp.sin(lam_im * dt)
    den = lam_re * lam_re + lam_im * lam_im
    nr = ab_re - 1.0
    f_re = (nr * lam_re + ab_im * lam_im) / den
    f_im = (ab_im * lam_re - nr * lam_im) / den
    bb_re = f_re[..., None] * b_re - f_im[..., None] * b_im
    bb_im = f_re[..., None] * b_im + f_im[..., None] * b_re
    k = jnp.arange(S5_LC + 1, dtype=F32)[:, None, None, None]
    pmag = jnp.exp(lam_re * dt * k)
    pw_re = pmag * jnp.cos(lam_im * dt * k)
    pw_im = pmag * jnp.sin(lam_im * dt * k)
    cp_re = c_re[None] * pw_re[:, :, :, None, :] - c_im[None] * pw_im[:, :, :, None, :]
    cp_im = c_re[None] * pw_im[:, :, :, None, :] + c_im[None] * pw_re[:, :, :, None, :]
    kk = jnp.einsum("kdghp,dgpj->kdghj",
                    jnp.concatenate([cp_re[:S5_LC], -cp_im[:S5_LC]], axis=-1),
                    jnp.concatenate([bb_re, bb_im], axis=-2))
    kf, kb = kk[:, 0], kk[:, 1]
    dmat = jnp.eye(S5_H, dtype=F32)[None] * dvec.reshape(S5_G, S5_H)[:, :, None]
    table = jnp.concatenate(
        [kf[:0:-1], (kf[0] + kb[0] + dmat)[None], kb[1:], jnp.zeros_like(kf[:1])], axis=0)
    table = jnp.transpose(table, (1, 2, 0, 3)).reshape(S5_G, S5_H, 2 * S5_W)
    tt = jnp.stack(
        [table[:, :, (S5_LC - 1 - j) * S5_H:(S5_LC - 1 - j) * S5_H + S5_W] for j in range(S5_LC)],
        axis=1).reshape(S5_G, S5_W, S5_W)

    rev = jnp.arange(S5_LC - 1, -1, -1)
    fwd = jnp.arange(S5_LC)

    def lanes4(f_re_part, b_re_part, f_im_part, b_im_part):
        return jnp.concatenate([f_re_part, b_re_part, f_im_part, b_im_part], axis=-1)

    def step_major(a):
        return jnp.transpose(a, (1, 0, 2))

    def chan_major(a):
        return jnp.transpose(a, (0, 2, 1))

    pr_f, pi_f = step_major(pw_re[rev, 0]), step_major(pw_im[rev, 0])
    pr_b, pi_b = step_major(pw_re[fwd, 1]), step_major(pw_im[fwd, 1])
    br_f, bi_f = chan_major(bb_re[0]), chan_major(bb_im[0])
    br_b, bi_b = chan_major(bb_re[1]), chan_major(bb_im[1])
    inj = (lanes4(pr_f, pr_b, pr_f, pr_b)[:, :, None, :] * lanes4(br_f, br_b, bi_f, bi_b)[:, None]
           + lanes4(-pi_f, -pi_b, pi_f, pi_b)[:, :, None, :] * lanes4(bi_f, bi_b, br_f, br_b)[:, None]
           ).reshape(S5_G, S5_W, 4 * S5_P)
    qr_f, qi_f = step_major(pw_re[fwd + 1, 0]), step_major(pw_im[fwd + 1, 0])
    qr_b, qi_b = step_major(pw_re[S5_LC - fwd, 1]), step_major(pw_im[S5_LC - fwd, 1])
    rt = (lanes4(qr_f, qr_b, -qi_f, -qi_b)[:, :, None, :]
          * lanes4(c_re[0], c_re[1], c_re[0], c_re[1])[:, None]
          + lanes4(-qi_f, -qi_b, -qr_f, -qr_b)[:, :, None, :]
          * lanes4(c_im[0], c_im[1], c_im[0], c_im[1])[:, None]).reshape(S5_G, S5_W, 4 * S5_P)
    a_re, a_im = pw_re[S5_LC], pw_im[S5_LC]
    pvec = jnp.concatenate([a_re[0], a_re[1]], axis=-1)
    qvec = jnp.concatenate([a_im[0], a_im[1]], axis=-1)
    pq = jnp.stack([pvec, qvec], axis=1)
    return tt.astype(BF16), inj.astype(BF16), rt.astype(BF16), pq


def _s5_kernel(*refs, n_chunks, n_batch, want_y):
    if want_y:
        x_ref, tt_ref, inj_ref, rt_ref, pq_ref, h0_ref, y_ref, fin_ref = refs[:8]
    else:
        x_ref, inj_ref, pq_ref, h0_ref, fin_ref = refs[:5]
    i_re_scr, i_im_scr, sf_re_scr, sf_im_scr, sb_re_scr, sb_im_scr = refs[-6:]
    half = 2 * S5_P
    x = x_ref[0]
    inj = lax.dot_general(x, inj_ref[0], _TN, preferred_element_type=F32)
    i_re_scr[...] = inj[:, 0:half]
    i_im_scr[...] = inj[:, half:]
    p = pq_ref[0, 0:1, :]
    q = pq_ref[0, 1:2, :]
    fwd_lanes = lax.broadcasted_iota(jnp.int32, (n_batch, half), 1) < S5_P
    h0 = h0_ref[0]
    s_re, s_im = h0[:, 0:half], h0[:, half:]
    for k in range(n_chunks):
        rows_f = slice(k * n_batch, (k + 1) * n_batch)
        rows_b = slice((n_chunks - 1 - k) * n_batch, (n_chunks - k) * n_batch)
        sf_re_scr[rows_f, :] = s_re
        sf_im_scr[rows_f, :] = s_im
        sb_re_scr[rows_b, :] = s_re
        sb_im_scr[rows_b, :] = s_im
        i_re = jnp.where(fwd_lanes, i_re_scr[rows_f, :], i_re_scr[rows_b, :])
        i_im = jnp.where(fwd_lanes, i_im_scr[rows_f, :], i_im_scr[rows_b, :])
        s_re, s_im = p * s_re - q * s_im + i_re, p * s_im + q * s_re + i_im
    fin_ref[0] = jnp.concatenate([s_re, s_im], axis=1)
    if want_y:
        fwd = lax.broadcasted_iota(jnp.int32, (1, half), 1) < S5_P
        s_prev = jnp.concatenate(
            [jnp.where(fwd, sf_re_scr[...], sb_re_scr[...]),
             jnp.where(fwd, sf_im_scr[...], sb_im_scr[...])], axis=1).astype(BF16)
        y = jnp.dot(tt_ref[0], x, preferred_element_type=F32)
        y = y + lax.dot_general(rt_ref[0], s_prev, _NT, preferred_element_type=F32)
        y_ref[0] = y


def _s5(xg, mats, h0, n_batch, want_y):
    tt, inj, rt, pq = mats
    n = xg.shape[2]
    nc = n // n_batch
    if h0 is None:
        h0 = jnp.zeros((S5_G, n_batch, S5_W), F32)
    grp = lambda shape: pl.BlockSpec((1,) + shape, lambda g: (g, 0, 0))
    sq = grp((S5_W, S5_W))
    if want_y:
        in_specs = [grp((S5_W, n)), sq, sq, sq]
        args = [xg, tt, inj, rt]
    else:
        in_specs = [grp((S5_W, n)), sq]
        args = [xg, inj]
    in_specs += [grp((2, 2 * S5_P)), grp((n_batch, S5_W))]
    args += [pq, h0]
    fin_spec = grp((n_batch, S5_W))
    fin_shape = jax.ShapeDtypeStruct((S5_G, n_batch, S5_W), F32)
    if want_y:
        out_specs = [grp((S5_W, n)), fin_spec]
        out_shape = [jax.ShapeDtypeStruct((S5_G, S5_W, n), F32), fin_shape]
    else:
        out_specs = fin_spec
        out_shape = fin_shape
    out = pl.pallas_call(
        functools.partial(_s5_kernel, n_chunks=nc, n_batch=n_batch, want_y=want_y),
        grid=(S5_G,),
        in_specs=in_specs,
        out_specs=out_specs,
        out_shape=out_shape,
        scratch_shapes=[pltpu.VMEM((n, 2 * S5_P), F32)] * 6,
        compiler_params=_params("arbitrary"),
        name="s5" if want_y else "s5_states",
    )(*args)
    if not want_y:
        return None, out
    return out[0], out[1]


def _s5out_kernel(y_ref, gwt_ref, gb_ref, o_ref, *tok_scr):
    nb, tt, _ = o_ref.shape
    for j in range(S5_LC):
        z = jax.nn.gelu(y_ref[:, j * S5_H:(j + 1) * S5_H, :].reshape(W_MIX, LANES))
        gate = jnp.dot(gwt_ref[...], z.astype(BF16), preferred_element_type=F32) + gb_ref[...]
        yc = z * jax.nn.sigmoid(gate)
        for lb, scr in enumerate(tok_scr):
            piece = yc[lb * LANES:(lb + 1) * LANES, :].T
            for c in range(tt // S5_LC):
                tok = c * S5_LC + j
                scr[tok * nb:(tok + 1) * nb, :] = piece[c * nb:(c + 1) * nb, :]
    for lb, scr in enumerate(tok_scr):
        for b in range(nb):
            o_ref[b, :, lb * LANES:(lb + 1) * LANES] = scr[pl.ds(b, tt, stride=nb), :].astype(BF16)


def _s5out(y, glu_wt, glu_b, n_batch):
    n = y.shape[2]
    seq = (n // n_batch) * S5_LC
    tt = (LANES // n_batch) * S5_LC
    out = pl.pallas_call(
        _s5out_kernel,
        grid=(seq // tt,),
        in_specs=[
            pl.BlockSpec((S5_G, S5_W, LANES), lambda k: (0, 0, k)),
            _const_spec((W_MIX, W_MIX)),
            _const_spec((W_MIX, 1)),
        ],
        out_specs=pl.BlockSpec((n_batch, tt, W_MIX), lambda k: (0, k, 0)),
        out_shape=jax.ShapeDtypeStruct((n_batch, seq, W_MIX), BF16),
        scratch_shapes=[pltpu.VMEM((n_batch * tt, LANES), F32)] * (W_MIX // LANES),
        compiler_params=_params("arbitrary"),
        name="s5out",
    )(y, glu_wt, glu_b)
    return out.reshape(n_batch * seq, W_MIX)


def _route(logits_t, rb_ref):
    s = [jax.nn.sigmoid(logits_t[e:e + 1, :]) for e in range(N_EXPERTS)]
    bz = [s[e] + rb_ref[e] for e in range(N_EXPERTS)]
    gs = []
    for g in range(N_GROUPS):
        v = bz[g * EXPERTS_PER_GROUP:(g + 1) * EXPERTS_PER_GROUP]
        best = None
        for i in range(EXPERTS_PER_GROUP):
            for j in range(i + 1, EXPERTS_PER_GROUP):
                pair = v[i] + v[j]
                best = pair if best is None else jnp.maximum(best, pair)
        gs.append(best)
    gbest = gs[0]
    gi = jnp.zeros_like(gbest, dtype=jnp.int32)
    for g in range(1, N_GROUPS):
        upd = gs[g] > gbest
        gi = jnp.where(upd, g, gi)
        gbest = jnp.where(upd, gs[g], gbest)

    def pick(vals, k):
        out = vals[k]
        for g in range(1, N_GROUPS):
            out = jnp.where(gi == g, vals[g * EXPERTS_PER_GROUP + k], out)
        return out

    v = [pick(bz, k) for k in range(EXPERTS_PER_GROUP)]
    sc = [pick(s, k) for k in range(EXPERTS_PER_GROUP)]

    def argmax4(vals):
        best, idx = vals[0], jnp.zeros_like(gi)
        for k in range(1, EXPERTS_PER_GROUP):
            upd = vals[k] > best
            idx = jnp.where(upd, k, idx)
            best = jnp.where(upd, vals[k], best)
        return idx

    i1 = argmax4(v)
    v2 = [jnp.where(i1 == k, -jnp.inf, v[k]) for k in range(EXPERTS_PER_GROUP)]
    i2 = argmax4(v2)

    def sel(vals, idx):
        out = vals[0]
        for k in range(1, EXPERTS_PER_GROUP):
            out = jnp.where(idx == k, vals[k], out)
        return out

    w1 = sel(sc, i1)
    w2 = sel(sc, i2)
    tot = w1 + w2
    w1 = w1 / tot
    w2 = w2 / tot
    first_low = i1 < i2
    lo = jnp.where(first_low, i1, i2)
    hi = jnp.where(first_low, i2, i1)
    wa = jnp.where(first_low, w1, w2)
    wb = jnp.where(first_low, w2, w1)
    base = gi * EXPERTS_PER_GROUP
    pair = jnp.where(lo == 0, 0, jnp.where(lo == 1, 3, 5)) + (hi - lo - 1)
    cls = gi * N_PAIRS + pair
    return (base + lo).astype(F32), (base + hi).astype(F32), wa, wb, cls


def _mix_kernel(*refs, conv_w, n_in, n_shared):
    (h_ref, yc_ref, mod_ref, n1_ref, n2_ref, win_ref, sgug_ref, sguw_ref, sgub_ref,
     convw_ref, wbr_ref, wout_ref, rwt_ref, tri_ref, rb_ref) = refs[:n_in]
    h2_ref, hn2_ref, route_ref, counts_ref, ya_scr, count_scr = refs[n_in + n_shared:]
    tm = h_ref.shape[0]
    x = h_ref[...]
    xn = _rms(x, n1_ref[...]) * (1.0 + mod_ref[0, 1:2, :]) + mod_ref[0, 0:1, :]
    xb = xn.astype(BF16)

    def proj(c0, width):
        return jnp.dot(xb, win_ref[:, c0:c0 + width], preferred_element_type=F32)

    def gate(branch):
        return jax.nn.sigmoid(proj(GATE0 + branch * D_MODEL, D_MODEL))

    u = jax.nn.gelu(proj(0, W_MIX))
    v = jax.nn.gelu(proj(W_MIX, W_MIX))
    vc = v - jnp.mean(v, axis=-1, keepdims=True)
    v = vc * lax.rsqrt(jnp.mean(vc * vc, axis=-1, keepdims=True) + EPS) * sgug_ref[...]
    vb = v.astype(BF16)
    for ck in range(tm // CHUNK):
        r0 = ck * CHUNK
        for hh in range(SGU_HEADS):
            c0 = hh * SGU_HEAD_W
            mixed = jnp.dot(sguw_ref[hh], vb[r0:r0 + CHUNK, c0:c0 + SGU_HEAD_W],
                            preferred_element_type=F32) + sgub_ref[:, hh:hh + 1]
            ya_scr[r0:r0 + CHUNK, c0:c0 + SGU_HEAD_W] = (
                u[r0:r0 + CHUNK, c0:c0 + SGU_HEAD_W] * mixed).astype(BF16)
    merged = gate(0) * jnp.dot(ya_scr[...], wbr_ref[0], preferred_element_type=F32)

    m = proj(3 * W_MIX, W_MIX) * proj(4 * W_MIX, W_MIX)
    pos = lax.broadcasted_iota(jnp.int32, (tm, 1), 0) % conv_w
    prev = jnp.where(pos == 0, 0.0, pltpu.roll(m, 1, axis=0))
    nxt = jnp.where(pos == conv_w - 1, 0.0, pltpu.roll(m, tm - 1, axis=0))
    yb = proj(2 * W_MIX, W_MIX) * (
        convw_ref[0:1, :] * prev + convw_ref[1:2, :] * m + convw_ref[2:3, :] * nxt)
    merged = merged + gate(1) * _bdot(yb, wbr_ref[1])

    merged = merged + gate(2) * jnp.dot(yc_ref[...], wbr_ref[2], preferred_element_type=F32)

    h2 = x + mod_ref[0, 2:3, :] * _bdot(merged, wout_ref[...])
    h2_ref[...] = h2
    hn2 = _rms(h2, n2_ref[...]) * (1.0 + mod_ref[0, 4:5, :]) + mod_ref[0, 3:4, :]
    hn2_ref[...] = hn2.astype(BF16)
    logits_t = lax.dot_general(rwt_ref[...], hn2, _NT, precision=HIGHEST, preferred_element_type=F32)
    ea, eb, wa, wb, cls = _route(logits_t, rb_ref)

    @pl.when(pl.program_id(0) == 0)
    def _():
        count_scr[...] = jnp.zeros_like(count_scr)

    onehot = lax.broadcasted_iota(jnp.int32, (CLASS_ROWS, tm), 0) == cls
    within = jnp.dot(onehot.astype(BF16), tri_ref[...], preferred_element_type=F32)
    before = count_scr[:, 0:1]
    rank = jnp.sum(jnp.where(onehot, within + before, 0.0), axis=0, keepdims=True) - 1.0
    total = before + within[:, tm - 1:tm]
    count_scr[...] = jnp.broadcast_to(total, count_scr.shape)
    counts_ref[...] = jnp.broadcast_to(total, counts_ref.shape)
    zero = jnp.zeros_like(wa)
    route_ref[...] = jnp.concatenate([ea, eb, wa, wb, rank, cls.astype(F32), zero, zero], axis=0)


def _mix(h, yc, mod, lw, seq, conv_w, router_wt, router_b, total_rows, row0=0, shared=None):
    t = h.shape[0]
    tm = min(TM_TOK, seq)
    per_seq = seq // tm
    tok = pl.BlockSpec((tm, D_MODEL), lambda i: (i, 0))
    in_specs = [
        tok,
        pl.BlockSpec((tm, W_MIX), lambda i: (i, 0)),
        pl.BlockSpec((1, N_MOD, D_MODEL), lambda i: (i // per_seq, 0, 0)),
        _const_spec((1, D_MODEL)),
        _const_spec((1, D_MODEL)),
        _const_spec((D_MODEL, D_IN)),
        _const_spec((1, W_MIX)),
        _const_spec((SGU_HEADS, CHUNK, CHUNK)),
        _const_spec((CHUNK, SGU_HEADS)),
        _const_spec((3, W_MIX)),
        _const_spec((N_BRANCH, W_MIX, D_MODEL)),
        _const_spec((D_MODEL, D_MODEL)),
        _const_spec((N_EXPERTS, D_MODEL)),
        _const_spec((tm, tm)),
        pl.BlockSpec(memory_space=pltpu.SMEM),
    ]
    blk0 = row0 // tm
    out_specs = [tok, pl.BlockSpec((tm, D_MODEL), lambda i: (blk0 + i, 0)),
                 pl.BlockSpec((8, tm), lambda i: (0, blk0 + i)),
                 pl.BlockSpec((CLASS_ROWS, LANES), lambda i: (0, 0))]
    out_shape = [
        jax.ShapeDtypeStruct((t, D_MODEL), F32),
        jax.ShapeDtypeStruct((total_rows, D_MODEL), BF16),
        jax.ShapeDtypeStruct((8, total_rows), F32),
        jax.ShapeDtypeStruct((CLASS_ROWS, LANES), F32),
    ]
    tri = (jnp.arange(tm)[:, None] <= jnp.arange(tm)[None, :]).astype(BF16)
    args = [h, yc, mod, lw["norm1_g"], lw["norm2_g"], lw["w_in"], lw["sgu_norm_g"], lw["sgu_w"],
            lw["sgu_b_t"], lw["conv_w"], lw["w_branch"], lw["w_out"], router_wt, tri, router_b]
    n_in = len(args)
    aliases = {}
    if shared is not None:
        in_specs += [pl.BlockSpec(memory_space=pl.ANY)] * 2
        args += list(shared)
        aliases = {n_in: 1, n_in + 1: 2}
    h2, hn2, route, counts = pl.pallas_call(
        functools.partial(_mix_kernel, conv_w=conv_w, n_in=n_in, n_shared=len(aliases)),
        grid=(t // tm,),
        in_specs=in_specs,
        out_specs=out_specs,
        out_shape=out_shape,
        scratch_shapes=[pltpu.VMEM((tm, W_MIX), BF16), pltpu.VMEM((CLASS_ROWS, LANES), F32)],
        input_output_aliases=aliases,
        compiler_params=_params("arbitrary"),
        name="mix",
    )(*args)
    return h2, hn2, route, counts[:N_CLASSES, 0].astype(jnp.int32)


def _moe_kernel(ea_ref, eb_ref, valid_ref, x_ref, w_ref, ga_ref, ua_ref, da_ref,
                gb_ref, ub_ref, db_ref, o_ref):
    i = pl.program_id(0)

    @pl.when(valid_ref[i] > 0)
    def _():
        x = x_ref[...]

        def expert(g_ref, u_ref, d_ref):
            g = jnp.dot(x, g_ref[0], preferred_element_type=F32)
            act = g * jax.nn.sigmoid(g) * jnp.dot(x, u_ref[0], preferred_element_type=F32)
            return _bdot(act, d_ref[0])

        o_ref[...] = (w_ref[:, 1:2] * expert(ga_ref, ua_ref, da_ref)
                      + w_ref[:, 2:3] * expert(gb_ref, ub_ref, db_ref)).astype(BF16)

    @pl.when(valid_ref[i] == 0)
    def _():
        o_ref[...] = jnp.zeros_like(o_ref)


def _moe(hn2, route, part_rows, part_counts, w_g, w_u, w_d):
    t = hn2.shape[0]
    cls = route[5].astype(jnp.int32)
    rank = route[4].astype(jnp.int32)
    row0 = 0
    for rows, cnt in zip(part_rows[:-1], part_counts[:-1]):
        row0 += rows
        rank = rank + jnp.where(jnp.arange(t, dtype=jnp.int32) >= row0, cnt[cls], 0)
    counts = sum(part_counts[1:], part_counts[0])
    tm = TM_MOE
    nt = t // tm + N_CLASSES
    tp = nt * tm
    padded = ((counts + tm - 1) // tm) * tm
    ends = jnp.cumsum(padded)
    offs = ends - padded
    pos = offs[cls] + rank
    starts = jnp.arange(nt, dtype=jnp.int32) * tm
    tile_cls = jnp.sum((starts[:, None] >= ends[None, :]).astype(jnp.int32), axis=1)
    valid = (tile_cls < N_CLASSES).astype(jnp.int32)
    tile_cls = jnp.minimum(tile_cls, N_CLASSES - 1)
    pair_lo = jnp.array([0, 0, 0, 1, 1, 2], jnp.int32)
    pair_hi = jnp.array([1, 2, 3, 2, 3, 3], jnp.int32)
    tile_ea = (tile_cls // N_PAIRS) * EXPERTS_PER_GROUP + pair_lo[tile_cls % N_PAIRS]
    tile_eb = (tile_cls // N_PAIRS) * EXPERTS_PER_GROUP + pair_hi[tile_cls % N_PAIRS]
    slot = jnp.stack([jnp.arange(t, dtype=F32), route[2], route[3], jnp.zeros((t,), F32)], axis=1)
    spread = (jnp.arange(tp, dtype=jnp.int32) % t).astype(F32)
    empty = jnp.stack([spread] + [jnp.zeros((tp,), F32)] * 3, axis=1)
    ws = empty.at[pos].set(slot, mode="promise_in_bounds", unique_indices=True)
    xs = hn2.at[ws[:, 0].astype(jnp.int32)].get(mode="promise_in_bounds")

    wspec_in = lambda sel: pl.BlockSpec((1, D_MODEL, D_FF), sel)
    wspec_out = lambda sel: pl.BlockSpec((1, D_FF, D_MODEL), sel)
    sel_a = lambda i, a, b, v: (a[i], 0, 0)
    sel_b = lambda i, a, b, v: (b[i], 0, 0)
    grid_spec = pltpu.PrefetchScalarGridSpec(
        num_scalar_prefetch=3,
        grid=(nt,),
        in_specs=[
            pl.BlockSpec((tm, D_MODEL), lambda i, a, b, v: (i, 0)),
            pl.BlockSpec((tm, 4), lambda i, a, b, v: (i, 0)),
            wspec_in(sel_a), wspec_in(sel_a), wspec_out(sel_a),
            wspec_in(sel_b), wspec_in(sel_b), wspec_out(sel_b),
        ],
        out_specs=pl.BlockSpec((tm, D_MODEL), lambda i, a, b, v: (i, 0)),
    )
    out = pl.pallas_call(
        _moe_kernel,
        grid_spec=grid_spec,
        out_shape=jax.ShapeDtypeStruct((tp, D_MODEL), BF16),
        compiler_params=_params("arbitrary"),
        name="moe",
    )(tile_ea, tile_eb, valid, xs, ws, w_g, w_u, w_d, w_g, w_u, w_d)
    return out.at[pos].get(mode="promise_in_bounds", unique_indices=True)


def _final_kernel(h_ref, moe_ref, mod_ref, g_ref, o_ref):
    x = h_ref[...] + mod_ref[0, 5:6, :] * moe_ref[...].astype(F32)
    o_ref[...] = _rms(x, g_ref[...])


def _final(h, moe, mod, g, seq):
    t = h.shape[0]
    tm = TM_TOK
    per_seq = seq // tm
    tok = pl.BlockSpec((tm, D_MODEL), lambda i: (i, 0))
    return pl.pallas_call(
        _final_kernel,
        grid=(t // tm,),
        in_specs=[tok, tok, pl.BlockSpec((1, N_MOD, D_MODEL), lambda i: (i // per_seq, 0, 0)),
                  _const_spec((1, D_MODEL))],
        out_specs=tok,
        out_shape=jax.ShapeDtypeStruct((t, D_MODEL), F32),
        compiler_params=_params("arbitrary"),
        name="final",
    )(h, moe, mod, g.reshape(1, D_MODEL))


def kernel(x, c, ctx, c_ctx, w_mod, b_mod, norm1_g, norm2_g, w_in, sgu_norm_g, sgu_w, sgu_b,
           conv_w, s5_lam_re, s5_lam_im, s5_log_dt, s5_b_re, s5_b_im, s5_c_re, s5_c_im, s5_d,
           glu_w, glu_b, w_branch, w_out, router_w, router_b, exp_w_gate, exp_w_up, exp_w_down,
           final_norm_g):
    nb, seq, _ = x.shape
    ctx_len = ctx.shape[1]
    h = x.reshape(nb * seq, D_MODEL)
    hc = ctx.reshape(nb * ctx_len, D_MODEL)

    mod_rows = 24
    cvec = jnp.zeros((mod_rows, D_MODEL), F32).at[:nb].set(c).at[nb].set(c_ctx)
    mods = _adaln(cvec, w_mod, b_mod).reshape(DEPTH, mod_rows, N_MOD, D_MODEL)
    router_wt = router_w.T

    all_mats = jax.vmap(_s5_mats)(s5_lam_re, s5_lam_im, s5_log_dt, s5_b_re, s5_b_im,
                                  s5_c_re, s5_c_im, s5_d)
    n_lat, n_ctx = nb * seq, nb * ctx_len
    moe = None
    mod = mod_c = None
    for l in range(DEPTH):
        last = l == DEPTH - 1
        pmod, pmod_c = mod, mod_c
        mod = mods[l, :nb]
        mod_c = jnp.broadcast_to(mods[l, nb][None], (nb, N_MOD, D_MODEL))
        w_in_b = _cast_layer(w_in, l)
        w_s5 = w_in_b[:, S5_COL0:S5_COL0 + W_MIX]
        lw = {
            "norm1_g": norm1_g[l].reshape(1, D_MODEL),
            "norm2_g": norm2_g[l].reshape(1, D_MODEL),
            "w_in": w_in_b,
            "sgu_norm_g": sgu_norm_g[l].reshape(1, W_MIX),
            "sgu_w": sgu_w[l].astype(BF16),
            "sgu_b_t": sgu_b[l].T,
            "conv_w": conv_w[l],
            "w_branch": _cast_layer(w_branch, l),
            "w_out": _cast_layer(w_out, l),
        }
        glu_wt = glu_w[l].T.astype(BF16)
        glu_bc = glu_b[l].reshape(W_MIX, 1)
        mats = tuple(m[l] for m in all_mats)
        w_g = _cast_layer(exp_w_gate, l)
        w_u = _cast_layer(exp_w_up, l)
        w_d = _cast_layer(exp_w_down, l)

        hc, xg_c = _pre(hc, mod_c, norm1_g[l], w_s5, ctx_len, moe=moe, moe_row0=n_lat, pmod=pmod_c)
        y_c, fin_c = _s5(xg_c, mats, None, nb, want_y=not last)
        h, xg = _pre(h, mod, norm1_g[l], w_s5, seq, moe=moe, pmod=pmod)
        y, _ = _s5(xg, mats, fin_c, nb, want_y=True)
        yc = _s5out(y, glu_wt, glu_bc, nb)
        if last:
            h, hn2, route, cnt = _mix(h, yc, mod, lw, seq, GRID_W, router_wt, router_b, n_lat)
            moe = _moe(hn2, route, [n_lat], [cnt], w_g, w_u, w_d)
        else:
            n_all = n_lat + n_ctx
            h, hn2, route, cnt = _mix(h, yc, mod, lw, seq, GRID_W, router_wt, router_b, n_all)
            yc_c = _s5out(y_c, glu_wt, glu_bc, nb)
            hc, hn2, route, cnt_c = _mix(hc, yc_c, mod_c, lw, ctx_len, ctx_len, router_wt, router_b,
                                         n_all, row0=n_lat, shared=(hn2, route))
            moe = _moe(hn2, route, [n_lat, n_ctx], [cnt, cnt_c], w_g, w_u, w_d)
    out = _final(h, moe, mod, final_norm_g, seq)
    return out.reshape(nb, seq, D_MODEL)
```

```python
import functools

import jax
import jax.numpy as jnp
from jax import lax
from jax.experimental import pallas as pl
from jax.experimental.pallas import tpu as pltpu

F32 = jnp.float32
BF16 = jnp.bfloat16
HIGHEST = lax.Precision.HIGHEST

D_MODEL = 1024
DEPTH = 2
EPS = 1e-6
W_MIX = 512
N_BRANCH = 3
CHUNK = 128
SGU_HEADS = 4
SGU_HEAD_W = W_MIX // SGU_HEADS
GRID_W = 64
S5_H = 16
S5_G = W_MIX // S5_H
S5_P = 64
S5_LC = 16
S5_W = S5_LC * S5_H
S5_COL0 = 5 * W_MIX
GATE0 = 6 * W_MIX
D_IN = GATE0 + N_BRANCH * D_MODEL
N_EXPERTS = 16
EXPERTS_PER_GROUP = 4
N_GROUPS = 4
D_FF = 512
N_PAIRS = 6
N_CLASSES = N_GROUPS * N_PAIRS
CLASS_ROWS = 32
N_MOD = 6
LANES = 128

TM_TOK = 512
TM_MOE = 256
PRE_B_PER_DOT = 4
CAST_BLOCK_ELEMS = 1 << 20
VMEM_LIMIT = 56 * 1024 * 1024

_NT = (((1,), (1,)), ((), ()))
_TN = (((0,), (0,)), ((), ()))


def _rms(x, g):
    return x * lax.rsqrt(jnp.mean(x * x, axis=-1, keepdims=True) + EPS) * g


def _bdot(a, b):
    return jnp.dot(a.astype(BF16), b, preferred_element_type=F32)


def _const_spec(shape):
    nd = len(shape)
    return pl.BlockSpec(shape, lambda *_: (0,) * nd, pipeline_mode=pl.Buffered(1))


def _params(*sem):
    return pltpu.CompilerParams(dimension_semantics=sem, vmem_limit_bytes=VMEM_LIMIT)


def _cast_kernel(x_ref, o_ref):
    o_ref[...] = x_ref[0].astype(BF16)


def _cast_layer(a, l):
    cols = a.shape[-1]
    rows = a[0].size // cols
    tr = 1 << ((CAST_BLOCK_ELEMS // cols).bit_length() - 1)
    while rows % tr:
        tr //= 2
    out = pl.pallas_call(
        _cast_kernel,
        grid=(rows // tr,),
        in_specs=[pl.BlockSpec((1, tr, cols), lambda i: (l, i, 0))],
        out_specs=pl.BlockSpec((tr, cols), lambda i: (i, 0)),
        out_shape=jax.ShapeDtypeStruct((rows, cols), BF16),
        compiler_params=_params("arbitrary"),
        name="cast",
    )(a.reshape(a.shape[0], rows, cols))
    return out.reshape(a.shape[1:])


def _adaln_kernel(c_ref, w_ref, b_ref, o_ref):
    c = c_ref[...]
    s = c * jax.nn.sigmoid(c)
    o_ref[0] = jnp.dot(s, w_ref[0], precision=HIGHEST, preferred_element_type=F32) + b_ref[0]


def _adaln(cvec, w_mod, b_mod):
    rows = cvec.shape[0]
    tn = 1536
    n_mod = N_MOD * D_MODEL
    return pl.pallas_call(
        _adaln_kernel,
        grid=(DEPTH, n_mod // tn),
        in_specs=[
            pl.BlockSpec((rows, D_MODEL), lambda l, j: (0, 0)),
            pl.BlockSpec((1, D_MODEL, tn), lambda l, j: (l, 0, j)),
            pl.BlockSpec((1, 1, tn), lambda l, j: (l, 0, j)),
        ],
        out_specs=pl.BlockSpec((1, rows, tn), lambda l, j: (l, 0, j)),
        out_shape=jax.ShapeDtypeStruct((DEPTH, rows, n_mod), F32),
        compiler_params=_params("arbitrary", "arbitrary"),
        name="adaln",
    )(cvec, w_mod, b_mod.reshape(DEPTH, 1, n_mod))


def _pre_kernel(*refs, residual):
    if residual:
        h_ref, moe_ref, pmod_ref, mod_ref, g_ref, w_ref, hnew_ref, xo_ref = refs[:8]
    else:
        h_ref, mod_ref, g_ref, w_ref, xo_ref = refs[:5]
    cu_scr = refs[-W_MIX // LANES:]
    nb, tt, _ = h_ref.shape
    for b0 in range(0, nb, PRE_B_PER_DOT):
        bs = slice(b0, b0 + PRE_B_PER_DOT)
        x = h_ref[bs]
        if residual:
            x = x + pmod_ref[bs, 5:6, :] * moe_ref[bs].astype(F32)
            hnew_ref[bs] = x
        xn = _rms(x, g_ref[...]) * (1.0 + mod_ref[bs, 1:2, :]) + mod_ref[bs, 0:1, :]
        cu = _bdot(xn.reshape(PRE_B_PER_DOT * tt, D_MODEL), w_ref[...])
        for lb, scr in enumerate(cu_scr):
            scr[b0 * tt:(b0 + PRE_B_PER_DOT) * tt, :] = cu[:, lb * LANES:(lb + 1) * LANES]
    groups_per_block = LANES // S5_H
    for i in range(S5_LC):
        for lb, scr in enumerate(cu_scr):
            piece = jnp.concatenate(
                [scr[pl.ds(c * S5_LC + i, nb, stride=tt), :] for c in range(tt // S5_LC)], axis=0)
            xo_ref[lb * groups_per_block:(lb + 1) * groups_per_block, i * S5_H:(i + 1) * S5_H, :] = (
                piece.T.astype(BF16).reshape(groups_per_block, S5_H, LANES))


def _pre(h, mod, norm_g, w_s5, seq, moe=None, moe_row0=0, pmod=None):
    t = h.shape[0]
    nb = t // seq
    tt = (LANES // nb) * S5_LC
    view = lambda a: a.reshape(a.shape[0] // seq, seq, D_MODEL)
    tok = pl.BlockSpec((nb, tt, D_MODEL), lambda k: (0, k, 0))
    modspec = _const_spec((nb, N_MOD, D_MODEL))
    residual = moe is not None
    in_specs = [tok]
    args = [view(h)]
    if residual:
        moe_blk = moe_row0 // t
        in_specs += [pl.BlockSpec((nb, tt, D_MODEL), lambda k: (moe_blk, k, 0)), modspec]
        args += [view(moe), pmod]
    in_specs += [modspec, _const_spec((1, D_MODEL)), _const_spec((D_MODEL, W_MIX))]
    args += [mod, norm_g.reshape(1, D_MODEL), w_s5]
    xo_spec = pl.BlockSpec((S5_G, S5_W, LANES), lambda k: (0, 0, k))
    xo_shape = jax.ShapeDtypeStruct((S5_G, S5_W, t // S5_LC), BF16)
    if residual:
        out_specs = [tok, xo_spec]
        out_shape = [jax.ShapeDtypeStruct((nb, seq, D_MODEL), F32), xo_shape]
    else:
        out_specs = xo_spec
        out_shape = xo_shape
    out = pl.pallas_call(
        functools.partial(_pre_kernel, residual=residual),
        grid=(seq // tt,),
        in_specs=in_specs,
        out_specs=out_specs,
        out_shape=out_shape,
        scratch_shapes=[pltpu.VMEM((nb * tt, LANES), F32)] * (W_MIX // LANES),
        compiler_params=_params("arbitrary"),
        name="pre_res" if residual else "pre",
    )(*args)
    if residual:
        return out[0].reshape(t, D_MODEL), out[1]
    return h, out


def _s5_mats(lam_re, lam_im, log_dt, b_re, b_im, c_re, c_im, dvec):
    dt = jnp.exp(log_dt)[..., None]
    mag = jnp.exp(lam_re * dt)
    ab_re = mag * jnp.cos(lam_im * dt)
    ab_im = mag * jnp.sin(lam_im * dt)
    den = lam_re * lam_re + lam_im * lam_im
    nr = ab_re - 1.0
    f_re = (nr * lam_re + ab_im * lam_im) / den
    f_im = (ab_im * lam_re - nr * lam_im) / den
    bb_re = f_re[..., None] * b_re - f_im[..., None] * b_im
    bb_im = f_re[..., None] * b_im + f_im[..., None] * b_re
    k = jnp.arange(S5_LC + 1, dtype=F32)[:, None, None, None]
    pmag = jnp.exp(lam_re * dt * k)
    pw_re = pmag * jnp.cos(lam_im * dt * k)
    pw_im = pmag * jnp.sin(lam_im * dt * k)
    cp_re = c_re[None] * pw_re[:, :, :, None, :] - c_im[None] * pw_im[:, :, :, None, :]
    cp_im = c_re[None] * pw_im[:, :, :, None, :] + c_im[None] * pw_re[:, :, :, None, :]
    kk = jnp.einsum("kdghp,dgpj->kdghj",
                    jnp.concatenate([cp_re[:S5_LC], -cp_im[:S5_LC]], axis=-1),
                    jnp.concatenate([bb_re, bb_im], axis=-2))
    kf, kb = kk[:, 0], kk[:, 1]
    dmat = jnp.eye(S5_H, dtype=F32)[None] * dvec.reshape(S5_G, S5_H)[:, :, None]
    table = jnp.concatenate(
        [kf[:0:-1], (kf[0] + kb[0] + dmat)[None], kb[1:], jnp.zeros_like(kf[:1])], axis=0)
    table = jnp.transpose(table, (1, 2, 0, 3)).reshape(S5_G, S5_H, 2 * S5_W)
    tt = jnp.stack(
        [table[:, :, (S5_LC - 1 - j) * S5_H:(S5_LC - 1 - j) * S5_H + S5_W] for j in range(S5_LC)],
        axis=1).reshape(S5_G, S5_W, S5_W)

    rev = jnp.arange(S5_LC - 1, -1, -1)
    fwd = jnp.arange(S5_LC)

    def lanes4(f_re_part, b_re_part, f_im_part, b_im_part):
        return jnp.concatenate([f_re_part, b_re_part, f_im_part, b_im_part], axis=-1)

    def step_major(a):
        return jnp.transpose(a, (1, 0, 2))

    def chan_major(a):
        return jnp.transpose(a, (0, 2, 1))

    pr_f, pi_f = step_major(pw_re[rev, 0]), step_major(pw_im[rev, 0])
    pr_b, pi_b = step_major(pw_re[fwd, 1]), step_major(pw_im[fwd, 1])
    br_f, bi_f = chan_major(bb_re[0]), chan_major(bb_im[0])
    br_b, bi_b = chan_major(bb_re[1]), chan_major(bb_im[1])
    inj = (lanes4(pr_f, pr_b, pr_f, pr_b)[:, :, None, :] * lanes4(br_f, br_b, bi_f, bi_b)[:, None]
           + lanes4(-pi_f, -pi_b, pi_f, pi_b)[:, :, None, :] * lanes4(bi_f, bi_b, br_f, br_b)[:, None]
           ).reshape(S5_G, S5_W, 4 * S5_P)
    qr_f, qi_f = step_major(pw_re[fwd + 1, 0]), step_major(pw_im[fwd + 1, 0])
    qr_b, qi_b = step_major(pw_re[S5_LC - fwd, 1]), step_major(pw_im[S5_LC - fwd, 1])
    rt = (lanes4(qr_f, qr_b, -qi_f, -qi_b)[:, :, None, :]
          * lanes4(c_re[0], c_re[1], c_re[0], c_re[1])[:, None]
          + lanes4(-qi_f, -qi_b, -qr_f, -qr_b)[:, :, None, :]
          * lanes4(c_im[0], c_im[1], c_im[0], c_im[1])[:, None]).reshape(S5_G, S5_W, 4 * S5_P)
    a_re, a_im = pw_re[S5_LC], pw_im[S5_LC]
    pvec = jnp.concatenate([a_re[0], a_re[1]], axis=-1)
    qvec = jnp.concatenate([a_im[0], a_im[1]], axis=-1)
    pq = jnp.stack([pvec, qvec], axis=1)
    return tt.astype(BF16), inj.astype(BF16), rt.astype(BF16), pq


def _s5_kernel(*refs, n_chunks, n_batch, want_y):
    if want_y:
        x_ref, tt_ref, inj_ref, rt_ref, pq_ref, h0_ref, y_ref, fin_ref = refs[:8]
    else:
        x_ref, inj_ref, pq_ref, h0_ref, fin_ref = refs[:5]
    i_re_scr, i_im_scr, sf_re_scr, sf_im_scr, sb_re_scr, sb_im_scr = refs[-6:]
    half = 2 * S5_P
    x = x_ref[0]
    inj = lax.dot_general(x, inj_ref[0], _TN, preferred_element_type=F32)
    i_re_scr[...] = inj[:, 0:half]
    i_im_scr[...] = inj[:, half:]
    p = pq_ref[0, 0:1, :]
    q = pq_ref[0, 1:2, :]
    fwd_lanes = lax.broadcasted_iota(jnp.int32, (n_batch, half), 1) < S5_P
    h0 = h0_ref[0]
    s_re, s_im = h0[:, 0:half], h0[:, half:]
    for k in range(n_chunks):
        rows_f = slice(k * n_batch, (k + 1) * n_batch)
        rows_b = slice((n_chunks - 1 - k) * n_batch, (n_chunks - k) * n_batch)
        sf_re_scr[rows_f, :] = s_re
        sf_im_scr[rows_f, :] = s_im
        sb_re_scr[rows_b, :] = s_re
        sb_im_scr[rows_b, :] = s_im
        i_re = jnp.where(fwd_lanes, i_re_scr[rows_f, :], i_re_scr[rows_b, :])
        i_im = jnp.where(fwd_lanes, i_im_scr[rows_f, :], i_im_scr[rows_b, :])
        s_re, s_im = p * s_re - q * s_im + i_re, p * s_im + q * s_re + i_im
    fin_ref[0] = jnp.concatenate([s_re, s_im], axis=1)
    if want_y:
        fwd = lax.broadcasted_iota(jnp.int32, (1, half), 1) < S5_P
        s_prev = jnp.concatenate(
            [jnp.where(fwd, sf_re_scr[...], sb_re_scr[...]),
             jnp.where(fwd, sf_im_scr[...], sb_im_scr[...])], axis=1).astype(BF16)
        y = jnp.dot(tt_ref[0], x, preferred_element_type=F32)
        y = y + lax.dot_general(rt_ref[0], s_prev, _NT, preferred_element_type=F32)
        y_ref[0] = y.astype(BF16)


def _s5(xg, mats, h0, n_batch, want_y):
    tt, inj, rt, pq = mats
    n = xg.shape[2]
    nc = n // n_batch
    if h0 is None:
        h0 = jnp.zeros((S5_G, n_batch, S5_W), F32)
    grp = lambda shape: pl.BlockSpec((1,) + shape, lambda g: (g, 0, 0))
    sq = grp((S5_W, S5_W))
    if want_y:
        in_specs = [grp((S5_W, n)), sq, sq, sq]
        args = [xg, tt, inj, rt]
    else:
        in_specs = [grp((S5_W, n)), sq]
        args = [xg, inj]
    in_specs += [grp((2, 2 * S5_P)), grp((n_batch, S5_W))]
    args += [pq, h0]
    fin_spec = grp((n_batch, S5_W))
    fin_shape = jax.ShapeDtypeStruct((S5_G, n_batch, S5_W), F32)
    if want_y:
        out_specs = [grp((S5_W, n)), fin_spec]
        out_shape = [jax.ShapeDtypeStruct((S5_G, S5_W, n), BF16), fin_shape]
    else:
        out_specs = fin_spec
        out_shape = fin_shape
    out = pl.pallas_call(
        functools.partial(_s5_kernel, n_chunks=nc, n_batch=n_batch, want_y=want_y),
        grid=(S5_G,),
        in_specs=in_specs,
        out_specs=out_specs,
        out_shape=out_shape,
        scratch_shapes=[pltpu.VMEM((n, 2 * S5_P), F32)] * 6,
        compiler_params=_params("arbitrary"),
        name="s5" if want_y else "s5_states",
    )(*args)
    if not want_y:
        return None, out
    return out[0], out[1]


def _s5out_kernel(y_ref, gwt_ref, gb_ref, o_ref, *tok_scr):
    nb, tt, _ = o_ref.shape
    for j in range(S5_LC):
        z = jax.nn.gelu(
            y_ref[:, j * S5_H:(j + 1) * S5_H, :].reshape(W_MIX, LANES).astype(F32))
        gate = jnp.dot(gwt_ref[...], z.astype(BF16), preferred_element_type=F32) + gb_ref[...]
        yc = z * jax.nn.sigmoid(gate)
        for lb, scr in enumerate(tok_scr):
            piece = yc[lb * LANES:(lb + 1) * LANES, :].T
            for c in range(tt // S5_LC):
                tok = c * S5_LC + j
                scr[tok * nb:(tok + 1) * nb, :] = piece[c * nb:(c + 1) * nb, :]
    for lb, scr in enumerate(tok_scr):
        for b in range(nb):
            o_ref[b, :, lb * LANES:(lb + 1) * LANES] = scr[pl.ds(b, tt, stride=nb), :].astype(BF16)


def _s5out(y, glu_wt, glu_b, n_batch):
    n = y.shape[2]
    seq = (n // n_batch) * S5_LC
    tt = (LANES // n_batch) * S5_LC
    out = pl.pallas_call(
        _s5out_kernel,
        grid=(seq // tt,),
        in_specs=[
            pl.BlockSpec((S5_G, S5_W, LANES), lambda k: (0, 0, k)),
            _const_spec((W_MIX, W_MIX)),
            _const_spec((W_MIX, 1)),
        ],
        out_specs=pl.BlockSpec((n_batch, tt, W_MIX), lambda k: (0, k, 0)),
        out_shape=jax.ShapeDtypeStruct((n_batch, seq, W_MIX), BF16),
        scratch_shapes=[pltpu.VMEM((n_batch * tt, LANES), F32)] * (W_MIX // LANES),
        compiler_params=_params("arbitrary"),
        name="s5out",
    )(y, glu_wt, glu_b)
    return out.reshape(n_batch * seq, W_MIX)


def _route(logits_t, rb_ref):
    s = [jax.nn.sigmoid(logits_t[e:e + 1, :]) for e in range(N_EXPERTS)]
    bz = [s[e] + rb_ref[e] for e in range(N_EXPERTS)]
    gs = []
    for g in range(N_GROUPS):
        v = bz[g * EXPERTS_PER_GROUP:(g + 1) * EXPERTS_PER_GROUP]
        best = None
        for i in range(EXPERTS_PER_GROUP):
            for j in range(i + 1, EXPERTS_PER_GROUP):
                pair = v[i] + v[j]
                best = pair if best is None else jnp.maximum(best, pair)
        gs.append(best)
    gbest = gs[0]
    gi = jnp.zeros_like(gbest, dtype=jnp.int32)
    for g in range(1, N_GROUPS):
        upd = gs[g] > gbest
        gi = jnp.where(upd, g, gi)
        gbest = jnp.where(upd, gs[g], gbest)

    def pick(vals, k):
        out = vals[k]
        for g in range(1, N_GROUPS):
            out = jnp.where(gi == g, vals[g * EXPERTS_PER_GROUP + k], out)
        return out

    v = [pick(bz, k) for k in range(EXPERTS_PER_GROUP)]
    sc = [pick(s, k) for k in range(EXPERTS_PER_GROUP)]

    def argmax4(vals):
        best, idx = vals[0], jnp.zeros_like(gi)
        for k in range(1, EXPERTS_PER_GROUP):
            upd = vals[k] > best
            idx = jnp.where(upd, k, idx)
            best = jnp.where(upd, vals[k], best)
        return idx

    i1 = argmax4(v)
    v2 = [jnp.where(i1 == k, -jnp.inf, v[k]) for k in range(EXPERTS_PER_GROUP)]
    i2 = argmax4(v2)

    def sel(vals, idx):
        out = vals[0]
        for k in range(1, EXPERTS_PER_GROUP):
            out = jnp.where(idx == k, vals[k], out)
        return out

    w1 = sel(sc, i1)
    w2 = sel(sc, i2)
    tot = w1 + w2
    w1 = w1 / tot
    w2 = w2 / tot
    first_low = i1 < i2
    lo = jnp.where(first_low, i1, i2)
    hi = jnp.where(first_low, i2, i1)
    wa = jnp.where(first_low, w1, w2)
    wb = jnp.where(first_low, w2, w1)
    base = gi * EXPERTS_PER_GROUP
    pair = jnp.where(lo == 0, 0, jnp.where(lo == 1, 3, 5)) + (hi - lo - 1)
    cls = gi * N_PAIRS + pair
    return (base + lo).astype(F32), (base + hi).astype(F32), wa, wb, cls


def _mix_kernel(*refs, conv_w, n_in, n_shared):
    (h_ref, yc_ref, mod_ref, n1_ref, n2_ref, win_ref, sgug_ref, sguw_ref, sgub_ref,
     convw_ref, wbr_ref, wout_ref, rwt_ref, tri_ref, rb_ref) = refs[:n_in]
    h2_ref, hn2_ref, route_ref, counts_ref, ya_scr, count_scr = refs[n_in + n_shared:]
    tm = h_ref.shape[0]
    x = h_ref[...]
    xn = _rms(x, n1_ref[...]) * (1.0 + mod_ref[0, 1:2, :]) + mod_ref[0, 0:1, :]
    xb = xn.astype(BF16)

    def proj(c0, width):
        return jnp.dot(xb, win_ref[:, c0:c0 + width], preferred_element_type=F32)

    def gate(branch):
        return jax.nn.sigmoid(proj(GATE0 + branch * D_MODEL, D_MODEL))

    u = jax.nn.gelu(proj(0, W_MIX))
    v = jax.nn.gelu(proj(W_MIX, W_MIX))
    vc = v - jnp.mean(v, axis=-1, keepdims=True)
    v = vc * lax.rsqrt(jnp.mean(vc * vc, axis=-1, keepdims=True) + EPS) * sgug_ref[...]
    vb = v.astype(BF16)
    for ck in range(tm // CHUNK):
        r0 = ck * CHUNK
        for hh in range(SGU_HEADS):
            c0 = hh * SGU_HEAD_W
            mixed = jnp.dot(sguw_ref[hh], vb[r0:r0 + CHUNK, c0:c0 + SGU_HEAD_W],
                            preferred_element_type=F32) + sgub_ref[:, hh:hh + 1]
            ya_scr[r0:r0 + CHUNK, c0:c0 + SGU_HEAD_W] = (
                u[r0:r0 + CHUNK, c0:c0 + SGU_HEAD_W] * mixed).astype(BF16)
    merged = gate(0) * jnp.dot(ya_scr[...], wbr_ref[0], preferred_element_type=F32)

    m = proj(3 * W_MIX, W_MIX) * proj(4 * W_MIX, W_MIX)
    pos = lax.broadcasted_iota(jnp.int32, (tm, 1), 0) % conv_w
    prev = jnp.where(pos == 0, 0.0, pltpu.roll(m, 1, axis=0))
    nxt = jnp.where(pos == conv_w - 1, 0.0, pltpu.roll(m, tm - 1, axis=0))
    yb = proj(2 * W_MIX, W_MIX) * (
        convw_ref[0:1, :] * prev + convw_ref[1:2, :] * m + convw_ref[2:3, :] * nxt)
    merged = merged + gate(1) * _bdot(yb, wbr_ref[1])

    merged = merged + gate(2) * jnp.dot(yc_ref[...], wbr_ref[2], preferred_element_type=F32)

    h2 = x + mod_ref[0, 2:3, :] * _bdot(merged, wout_ref[...])
    h2_ref[...] = h2
    hn2 = _rms(h2, n2_ref[...]) * (1.0 + mod_ref[0, 4:5, :]) + mod_ref[0, 3:4, :]
    hn2_ref[...] = hn2.astype(BF16)
    logits_t = lax.dot_general(rwt_ref[...], hn2, _NT, precision=HIGHEST, preferred_element_type=F32)
    ea, eb, wa, wb, cls = _route(logits_t, rb_ref)

    @pl.when(pl.program_id(0) == 0)
    def _():
        count_scr[...] = jnp.zeros_like(count_scr)

    onehot = lax.broadcasted_iota(jnp.int32, (CLASS_ROWS, tm), 0) == cls
    within = jnp.dot(onehot.astype(BF16), tri_ref[...], preferred_element_type=F32)
    before = count_scr[:, 0:1]
    rank = jnp.sum(jnp.where(onehot, within + before, 0.0), axis=0, keepdims=True) - 1.0
    total = before + within[:, tm - 1:tm]
    count_scr[...] = jnp.broadcast_to(total, count_scr.shape)
    counts_ref[...] = jnp.broadcast_to(total, counts_ref.shape)
    zero = jnp.zeros_like(wa)
    route_ref[...] = jnp.concatenate([ea, eb, wa, wb, rank, cls.astype(F32), zero, zero], axis=0)


def _mix(h, yc, mod, lw, seq, conv_w, router_wt, router_b, total_rows, row0=0, shared=None):
    t = h.shape[0]
    tm = min(TM_TOK, seq)
    per_seq = seq // tm
    tok = pl.BlockSpec((tm, D_MODEL), lambda i: (i, 0))
    in_specs = [
        tok,
        pl.BlockSpec((tm, W_MIX), lambda i: (i, 0)),
        pl.BlockSpec((1, N_MOD, D_MODEL), lambda i: (i // per_seq, 0, 0)),
        _const_spec((1, D_MODEL)),
        _const_spec((1, D_MODEL)),
        _const_spec((D_MODEL, D_IN)),
        _const_spec((1, W_MIX)),
        _const_spec((SGU_HEADS, CHUNK, CHUNK)),
        _const_spec((CHUNK, SGU_HEADS)),
        _const_spec((3, W_MIX)),
        _const_spec((N_BRANCH, W_MIX, D_MODEL)),
        _const_spec((D_MODEL, D_MODEL)),
        _const_spec((N_EXPERTS, D_MODEL)),
        _const_spec((tm, tm)),
        pl.BlockSpec(memory_space=pltpu.SMEM),
    ]
    blk0 = row0 // tm
    out_specs = [tok, pl.BlockSpec((tm, D_MODEL), lambda i: (blk0 + i, 0)),
                 pl.BlockSpec((8, tm), lambda i: (0, blk0 + i)),
                 pl.BlockSpec((CLASS_ROWS, LANES), lambda i: (0, 0))]
    out_shape = [
        jax.ShapeDtypeStruct((t, D_MODEL), F32),
        jax.ShapeDtypeStruct((total_rows, D_MODEL), BF16),
        jax.ShapeDtypeStruct((8, total_rows), F32),
        jax.ShapeDtypeStruct((CLASS_ROWS, LANES), F32),
    ]
    tri = (jnp.arange(tm)[:, None] <= jnp.arange(tm)[None, :]).astype(BF16)
    args = [h, yc, mod, lw["norm1_g"], lw["norm2_g"], lw["w_in"], lw["sgu_norm_g"], lw["sgu_w"],
            lw["sgu_b_t"], lw["conv_w"], lw["w_branch"], lw["w_out"], router_wt, tri, router_b]
    n_in = len(args)
    aliases = {}
    if shared is not None:
        in_specs += [pl.BlockSpec(memory_space=pl.ANY)] * 2
        args += list(shared)
        aliases = {n_in: 1, n_in + 1: 2}
    h2, hn2, route, counts = pl.pallas_call(
        functools.partial(_mix_kernel, conv_w=conv_w, n_in=n_in, n_shared=len(aliases)),
        grid=(t // tm,),
        in_specs=in_specs,
        out_specs=out_specs,
        out_shape=out_shape,
        scratch_shapes=[pltpu.VMEM((tm, W_MIX), BF16), pltpu.VMEM((CLASS_ROWS, LANES), F32)],
        input_output_aliases=aliases,
        compiler_params=_params("arbitrary"),
        name="mix",
    )(*args)
    return h2, hn2, route, counts[:N_CLASSES, 0].astype(jnp.int32)


def _moe_kernel(ea_ref, eb_ref, valid_ref, x_ref, w_ref, ga_ref, ua_ref, da_ref,
                gb_ref, ub_ref, db_ref, o_ref):
    i = pl.program_id(0)

    @pl.when(valid_ref[i] > 0)
    def _():
        x = x_ref[...]

        def expert(g_ref, u_ref, d_ref):
            g = jnp.dot(x, g_ref[0], preferred_element_type=F32)
            act = g * jax.nn.sigmoid(g) * jnp.dot(x, u_ref[0], preferred_element_type=F32)
            return _bdot(act, d_ref[0])

        o_ref[...] = (w_ref[:, 1:2] * expert(ga_ref, ua_ref, da_ref)
                      + w_ref[:, 2:3] * expert(gb_ref, ub_ref, db_ref)).astype(BF16)

    @pl.when(valid_ref[i] == 0)
    def _():
        o_ref[...] = jnp.zeros_like(o_ref)


def _moe(hn2, route, part_rows, part_counts, w_g, w_u, w_d):
    t = hn2.shape[0]
    cls = route[5].astype(jnp.int32)
    rank = route[4].astype(jnp.int32)
    row0 = 0
    for rows, cnt in zip(part_rows[:-1], part_counts[:-1]):
        row0 += rows
        rank = rank + jnp.where(jnp.arange(t, dtype=jnp.int32) >= row0, cnt[cls], 0)
    counts = sum(part_counts[1:], part_counts[0])
    tm = TM_MOE
    nt = t // tm + N_CLASSES
    tp = nt * tm
    padded = ((counts + tm - 1) // tm) * tm
    ends = jnp.cumsum(padded)
    offs = ends - padded
    pos = offs[cls] + rank
    starts = jnp.arange(nt, dtype=jnp.int32) * tm
    tile_cls = jnp.sum((starts[:, None] >= ends[None, :]).astype(jnp.int32), axis=1)
    valid = (tile_cls < N_CLASSES).astype(jnp.int32)
    tile_cls = jnp.minimum(tile_cls, N_CLASSES - 1)
    pair_lo = jnp.array([0, 0, 0, 1, 1, 2], jnp.int32)
    pair_hi = jnp.array([1, 2, 3, 2, 3, 3], jnp.int32)
    tile_ea = (tile_cls // N_PAIRS) * EXPERTS_PER_GROUP + pair_lo[tile_cls % N_PAIRS]
    tile_eb = (tile_cls // N_PAIRS) * EXPERTS_PER_GROUP + pair_hi[tile_cls % N_PAIRS]
    slot = jnp.stack([jnp.arange(t, dtype=F32), route[2], route[3], jnp.zeros((t,), F32)], axis=1)
    spread = (jnp.arange(tp, dtype=jnp.int32) % t).astype(F32)
    empty = jnp.stack([spread] + [jnp.zeros((tp,), F32)] * 3, axis=1)
    ws = empty.at[pos].set(slot, mode="promise_in_bounds", unique_indices=True)
    xs = hn2.at[ws[:, 0].astype(jnp.int32)].get(mode="promise_in_bounds")

    wspec_in = lambda sel: pl.BlockSpec((1, D_MODEL, D_FF), sel)
    wspec_out = lambda sel: pl.BlockSpec((1, D_FF, D_MODEL), sel)
    sel_a = lambda i, a, b, v: (a[i], 0, 0)
    sel_b = lambda i, a, b, v: (b[i], 0, 0)
    grid_spec = pltpu.PrefetchScalarGridSpec(
        num_scalar_prefetch=3,
        grid=(nt,),
        in_specs=[
            pl.BlockSpec((tm, D_MODEL), lambda i, a, b, v: (i, 0)),
            pl.BlockSpec((tm, 4), lambda i, a, b, v: (i, 0)),
            wspec_in(sel_a), wspec_in(sel_a), wspec_out(sel_a),
            wspec_in(sel_b), wspec_in(sel_b), wspec_out(sel_b),
        ],
        out_specs=pl.BlockSpec((tm, D_MODEL), lambda i, a, b, v: (i, 0)),
    )
    out = pl.pallas_call(
        _moe_kernel,
        grid_spec=grid_spec,
        out_shape=jax.ShapeDtypeStruct((tp, D_MODEL), BF16),
        compiler_params=_params("arbitrary"),
        name="moe",
    )(tile_ea, tile_eb, valid, xs, ws, w_g, w_u, w_d, w_g, w_u, w_d)
    return out.at[pos].get(mode="promise_in_bounds", unique_indices=True)


def _final_kernel(h_ref, moe_ref, mod_ref, g_ref, o_ref):
    x = h_ref[...] + mod_ref[0, 5:6, :] * moe_ref[...].astype(F32)
    o_ref[...] = _rms(x, g_ref[...])


def _final(h, moe, mod, g, seq):
    t = h.shape[0]
    tm = TM_TOK
    per_seq = seq // tm
    tok = pl.BlockSpec((tm, D_MODEL), lambda i: (i, 0))
    return pl.pallas_call(
        _final_kernel,
        grid=(t // tm,),
        in_specs=[tok, tok, pl.BlockSpec((1, N_MOD, D_MODEL), lambda i: (i // per_seq, 0, 0)),
                  _const_spec((1, D_MODEL))],
        out_specs=tok,
        out_shape=jax.ShapeDtypeStruct((t, D_MODEL), F32),
        compiler_params=_params("arbitrary"),
        name="final",
    )(h, moe, mod, g.reshape(1, D_MODEL))


def kernel(x, c, ctx, c_ctx, w_mod, b_mod, norm1_g, norm2_g, w_in, sgu_norm_g, sgu_w, sgu_b,
           conv_w, s5_lam_re, s5_lam_im, s5_log_dt, s5_b_re, s5_b_im, s5_c_re, s5_c_im, s5_d,
           glu_w, glu_b, w_branch, w_out, router_w, router_b, exp_w_gate, exp_w_up, exp_w_down,
           final_norm_g):
    nb, seq, _ = x.shape
    ctx_len = ctx.shape[1]
    h = x.reshape(nb * seq, D_MODEL)
    hc = ctx.reshape(nb * ctx_len, D_MODEL)

    mod_rows = 24
    cvec = jnp.zeros((mod_rows, D_MODEL), F32).at[:nb].set(c).at[nb].set(c_ctx)
    mods = _adaln(cvec, w_mod, b_mod).reshape(DEPTH, mod_rows, N_MOD, D_MODEL)
    router_wt = router_w.T

    all_mats = jax.vmap(_s5_mats)(s5_lam_re, s5_lam_im, s5_log_dt, s5_b_re, s5_b_im,
                                  s5_c_re, s5_c_im, s5_d)
    n_lat, n_ctx = nb * seq, nb * ctx_len
    moe = None
    mod = mod_c = None
    for l in range(DEPTH):
        last = l == DEPTH - 1
        pmod, pmod_c = mod, mod_c
        mod = mods[l, :nb]
        mod_c = jnp.broadcast_to(mods[l, nb][None], (nb, N_MOD, D_MODEL))
        w_in_b = _cast_layer(w_in, l)
        w_s5 = w_in_b[:, S5_COL0:S5_COL0 + W_MIX]
        lw = {
            "norm1_g": norm1_g[l].reshape(1, D_MODEL),
            "norm2_g": norm2_g[l].reshape(1, D_MODEL),
            "w_in": w_in_b,
            "sgu_norm_g": sgu_norm_g[l].reshape(1, W_MIX),
            "sgu_w": sgu_w[l].astype(BF16),
            "sgu_b_t": sgu_b[l].T,
            "conv_w": conv_w[l],
            "w_branch": _cast_layer(w_branch, l),
            "w_out": _cast_layer(w_out, l),
        }
        glu_wt = glu_w[l].T.astype(BF16)
        glu_bc = glu_b[l].reshape(W_MIX, 1)
        mats = tuple(m[l] for m in all_mats)
        w_g = _cast_layer(exp_w_gate, l)
        w_u = _cast_layer(exp_w_up, l)
        w_d = _cast_layer(exp_w_down, l)

        hc, xg_c = _pre(hc, mod_c, norm1_g[l], w_s5, ctx_len, moe=moe, moe_row0=n_lat, pmod=pmod_c)
        y_c, fin_c = _s5(xg_c, mats, None, nb, want_y=not last)
        h, xg = _pre(h, mod, norm1_g[l], w_s5, seq, moe=moe, pmod=pmod)
        y, _ = _s5(xg, mats, fin_c, nb, want_y=True)
        yc = _s5out(y, glu_wt, glu_bc, nb)
        if last:
            h, hn2, route, cnt = _mix(h, yc, mod, lw, seq, GRID_W, router_wt, router_b, n_lat)
            moe = _moe(hn2, route, [n_lat], [cnt], w_g, w_u, w_d)
        else:
            n_all = n_lat + n_ctx
            h, hn2, route, cnt = _mix(h, yc, mod, lw, seq, GRID_W, router_wt, router_b, n_all)
            yc_c = _s5out(y_c, glu_wt, glu_bc, nb)
            hc, hn2, route, cnt_c = _mix(hc, yc_c, mod_c, lw, ctx_len, ctx_len, router_wt, router_b,
                                         n_all, row0=n_lat, shared=(hn2, route))
            moe = _moe(hn2, route, [n_lat, n_ctx], [cnt, cnt_c], w_g, w_u, w_d)
    out = _final(h, moe, mod, final_norm_g, seq)
    return out.reshape(nb, seq, D_MODEL)
```

```python
import functools

import jax
import jax.numpy as jnp
from jax import lax
from jax.experimental import pallas as pl
from jax.experimental.pallas import tpu as pltpu

F32 = jnp.float32
BF16 = jnp.bfloat16
HIGHEST = lax.Precision.HIGHEST

D_MODEL = 1024
DEPTH = 2
EPS = 1e-6
W_MIX = 512
N_BRANCH = 3
CHUNK = 128
SGU_HEADS = 4
SGU_HEAD_W = W_MIX // SGU_HEADS
GRID_W = 64
S5_H = 16
S5_G = W_MIX // S5_H
S5_P = 64
S5_LC = 16
S5_W = S5_LC * S5_H
S5_COL0 = 5 * W_MIX
GATE0 = 6 * W_MIX
D_IN = GATE0 + N_BRANCH * D_MODEL
N_EXPERTS = 16
EXPERTS_PER_GROUP = 4
N_GROUPS = 4
D_FF = 512
N_PAIRS = 6
N_CLASSES = N_GROUPS * N_PAIRS
CLASS_ROWS = 32
N_MOD = 6
LANES = 128

TM_TOK = 512
TM_MOE = 256
PRE_B_PER_DOT = 4
CAST_BLOCK_ELEMS = 1 << 20
VMEM_LIMIT = 56 * 1024 * 1024

_NT = (((1,), (1,)), ((), ()))
_TN = (((0,), (0,)), ((), ()))


def _rms(x, g):
    return x * lax.rsqrt(jnp.mean(x * x, axis=-1, keepdims=True) + EPS) * g


def _bdot(a, b):
    return jnp.dot(a.astype(BF16), b, preferred_element_type=F32)


def _const_spec(shape):
    nd = len(shape)
    return pl.BlockSpec(shape, lambda *_: (0,) * nd, pipeline_mode=pl.Buffered(1))


def _params(*sem):
    return pltpu.CompilerParams(dimension_semantics=sem, vmem_limit_bytes=VMEM_LIMIT)


def _cast_kernel(x_ref, o_ref):
    o_ref[...] = x_ref[0].astype(BF16)


def _cast_layer(a, l):
    cols = a.shape[-1]
    rows = a[0].size // cols
    tr = 1 << ((CAST_BLOCK_ELEMS // cols).bit_length() - 1)
    while rows % tr:
        tr //= 2
    out = pl.pallas_call(
        _cast_kernel,
        grid=(rows // tr,),
        in_specs=[pl.BlockSpec((1, tr, cols), lambda i: (l, i, 0))],
        out_specs=pl.BlockSpec((tr, cols), lambda i: (i, 0)),
        out_shape=jax.ShapeDtypeStruct((rows, cols), BF16),
        compiler_params=_params("arbitrary"),
        name="cast",
    )(a.reshape(a.shape[0], rows, cols))
    return out.reshape(a.shape[1:])


def _adaln_kernel(c_ref, w_ref, b_ref, o_ref):
    c = c_ref[...]
    s = c * jax.nn.sigmoid(c)
    o_ref[0] = jnp.dot(s, w_ref[0], precision=HIGHEST, preferred_element_type=F32) + b_ref[0]


def _adaln(cvec, w_mod, b_mod):
    rows = cvec.shape[0]
    tn = 1536
    n_mod = N_MOD * D_MODEL
    return pl.pallas_call(
        _adaln_kernel,
        grid=(DEPTH, n_mod // tn),
        in_specs=[
            pl.BlockSpec((rows, D_MODEL), lambda l, j: (0, 0)),
            pl.BlockSpec((1, D_MODEL, tn), lambda l, j: (l, 0, j)),
            pl.BlockSpec((1, 1, tn), lambda l, j: (l, 0, j)),
        ],
        out_specs=pl.BlockSpec((1, rows, tn), lambda l, j: (l, 0, j)),
        out_shape=jax.ShapeDtypeStruct((DEPTH, rows, n_mod), F32),
        compiler_params=_params("arbitrary", "arbitrary"),
        name="adaln",
    )(cvec, w_mod, b_mod.reshape(DEPTH, 1, n_mod))


def _pre_kernel(*refs, residual):
    if residual:
        h_ref, moe_ref, pmod_ref, mod_ref, g_ref, w_ref, hnew_ref, xo_ref = refs[:8]
    else:
        h_ref, mod_ref, g_ref, w_ref, xo_ref = refs[:5]
    cu_scr = refs[-W_MIX // LANES:]
    nb, tt, _ = h_ref.shape
    for b0 in range(0, nb, PRE_B_PER_DOT):
        bs = slice(b0, b0 + PRE_B_PER_DOT)
        x = h_ref[bs]
        if residual:
            x = x + pmod_ref[bs, 5:6, :] * moe_ref[bs].astype(F32)
            hnew_ref[bs] = x
        xn = _rms(x, g_ref[...]) * (1.0 + mod_ref[bs, 1:2, :]) + mod_ref[bs, 0:1, :]
        cu = _bdot(xn.reshape(PRE_B_PER_DOT * tt, D_MODEL), w_ref[...])
        for lb, scr in enumerate(cu_scr):
            scr[b0 * tt:(b0 + PRE_B_PER_DOT) * tt, :] = cu[:, lb * LANES:(lb + 1) * LANES]
    groups_per_block = LANES // S5_H
    for i in range(S5_LC):
        for lb, scr in enumerate(cu_scr):
            piece = jnp.concatenate(
                [scr[pl.ds(c * S5_LC + i, nb, stride=tt), :] for c in range(tt // S5_LC)], axis=0)
            xo_ref[lb * groups_per_block:(lb + 1) * groups_per_block, i * S5_H:(i + 1) * S5_H, :] = (
                piece.T.astype(BF16).reshape(groups_per_block, S5_H, LANES))


def _pre(h, mod, norm_g, w_s5, seq, moe=None, moe_row0=0, pmod=None):
    t = h.shape[0]
    nb = t // seq
    tt = (LANES // nb) * S5_LC
    view = lambda a: a.reshape(a.shape[0] // seq, seq, D_MODEL)
    tok = pl.BlockSpec((nb, tt, D_MODEL), lambda k: (0, k, 0))
    modspec = _const_spec((nb, N_MOD, D_MODEL))
    residual = moe is not None
    in_specs = [tok]
    args = [view(h)]
    if residual:
        moe_blk = moe_row0 // t
        in_specs += [pl.BlockSpec((nb, tt, D_MODEL), lambda k: (moe_blk, k, 0)), modspec]
        args += [view(moe), pmod]
    in_specs += [modspec, _const_spec((1, D_MODEL)), _const_spec((D_MODEL, W_MIX))]
    args += [mod, norm_g.reshape(1, D_MODEL), w_s5]
    xo_spec = pl.BlockSpec((S5_G, S5_W, LANES), lambda k: (0, 0, k))
    xo_shape = jax.ShapeDtypeStruct((S5_G, S5_W, t // S5_LC), BF16)
    if residual:
        out_specs = [tok, xo_spec]
        out_shape = [jax.ShapeDtypeStruct((nb, seq, D_MODEL), F32), xo_shape]
    else:
        out_specs = xo_spec
        out_shape = xo_shape
    out = pl.pallas_call(
        functools.partial(_pre_kernel, residual=residual),
        grid=(seq // tt,),
        in_specs=in_specs,
        out_specs=out_specs,
        out_shape=out_shape,
        scratch_shapes=[pltpu.VMEM((nb * tt, LANES), F32)] * (W_MIX // LANES),
        compiler_params=_params("arbitrary"),
        name="pre_res" if residual else "pre",
    )(*args)
    if residual:
        return out[0].reshape(t, D_MODEL), out[1]
    return h, out


def _s5_mats(lam_re, lam_im, log_dt, b_re, b_im, c_re, c_im, dvec):
    dt = jnp.exp(log_dt)[..., None]
    mag = jnp.exp(lam_re * dt)
    ab_re = mag * jnp.cos(lam_im * dt)
    ab_im = mag * jnp.sin(lam_im * dt)
    den = lam_re * lam_re + lam_im * lam_im
    nr = ab_re - 1.0
    f_re = (nr * lam_re + ab_im * lam_im) / den
    f_im = (ab_im * lam_re - nr * lam_im) / den
    bb_re = f_re[..., None] * b_re - f_im[..., None] * b_im
    bb_im = f_re[..., None] * b_im + f_im[..., None] * b_re
    k = jnp.arange(S5_LC + 1, dtype=F32)[:, None, None, None]
    pmag = jnp.exp(lam_re * dt * k)
    pw_re = pmag * jnp.cos(lam_im * dt * k)
    pw_im = pmag * jnp.sin(lam_im * dt * k)
    cp_re = c_re[None] * pw_re[:, :, :, None, :] - c_im[None] * pw_im[:, :, :, None, :]
    cp_im = c_re[None] * pw_im[:, :, :, None, :] + c_im[None] * pw_re[:, :, :, None, :]
    kk = jnp.einsum("kdghp,dgpj->kdghj",
                    jnp.concatenate([cp_re[:S5_LC], -cp_im[:S5_LC]], axis=-1),
                    jnp.concatenate([bb_re, bb_im], axis=-2))
    kf, kb = kk[:, 0], kk[:, 1]
    dmat = jnp.eye(S5_H, dtype=F32)[None] * dvec.reshape(S5_G, S5_H)[:, :, None]
    table = jnp.concatenate(
        [kf[:0:-1], (kf[0] + kb[0] + dmat)[None], kb[1:], jnp.zeros_like(kf[:1])], axis=0)
    table = jnp.transpose(table, (1, 2, 0, 3)).reshape(S5_G, S5_H, 2 * S5_W)
    tt = jnp.stack(
        [table[:, :, (S5_LC - 1 - j) * S5_H:(S5_LC - 1 - j) * S5_H + S5_W] for j in range(S5_LC)],
        axis=1).reshape(S5_G, S5_W, S5_W)

    rev = jnp.arange(S5_LC - 1, -1, -1)
    fwd = jnp.arange(S5_LC)

    def lanes4(f_re_part, b_re_part, f_im_part, b_im_part):
        return jnp.concatenate([f_re_part, b_re_part, f_im_part, b_im_part], axis=-1)

    def step_major(a):
        return jnp.transpose(a, (1, 0, 2))

    def chan_major(a):
        return jnp.transpose(a, (0, 2, 1))

    pr_f, pi_f = step_major(pw_re[rev, 0]), step_major(pw_im[rev, 0])
    pr_b, pi_b = step_major(pw_re[fwd, 1]), step_major(pw_im[fwd, 1])
    br_f, bi_f = chan_major(bb_re[0]), chan_major(bb_im[0])
    br_b, bi_b = chan_major(bb_re[1]), chan_major(bb_im[1])
    inj = (lanes4(pr_f, pr_b, pr_f, pr_b)[:, :, None, :] * lanes4(br_f, br_b, bi_f, bi_b)[:, None]
           + lanes4(-pi_f, -pi_b, pi_f, pi_b)[:, :, None, :] * lanes4(bi_f, bi_b, br_f, br_b)[:, None]
           ).reshape(S5_G, S5_W, 4 * S5_P)
    qr_f, qi_f = step_major(pw_re[fwd + 1, 0]), step_major(pw_im[fwd + 1, 0])
    qr_b, qi_b = step_major(pw_re[S5_LC - fwd, 1]), step_major(pw_im[S5_LC - fwd, 1])
    rt = (lanes4(qr_f, qr_b, -qi_f, -qi_b)[:, :, None, :]
          * lanes4(c_re[0], c_re[1], c_re[0], c_re[1])[:, None]
          + lanes4(-qi_f, -qi_b, -qr_f, -qr_b)[:, :, None, :]
          * lanes4(c_im[0], c_im[1], c_im[0], c_im[1])[:, None]).reshape(S5_G, S5_W, 4 * S5_P)
    a_re, a_im = pw_re[S5_LC], pw_im[S5_LC]
    pvec = jnp.concatenate([a_re[0], a_re[1]], axis=-1)
    qvec = jnp.concatenate([a_im[0], a_im[1]], axis=-1)
    pq = jnp.stack([pvec, qvec], axis=1)
    return tt.astype(BF16), inj.astype(BF16), rt.astype(BF16), pq


def _s5_kernel(*refs, n_chunks, n_batch, want_y):
    if want_y:
        x_ref, tt_ref, inj_ref, rt_ref, pq_ref, h0_ref, y_ref, fin_ref = refs[:8]
    else:
        x_ref, inj_ref, pq_ref, h0_ref, fin_ref = refs[:5]
    i_re_scr, i_im_scr, sf_re_scr, sf_im_scr, sb_re_scr, sb_im_scr = refs[-6:]
    half = 2 * S5_P
    x = x_ref[0]
    inj = lax.dot_general(x, inj_ref[0], _TN, preferred_element_type=F32)
    i_re_scr[...] = inj[:, 0:half]
    i_im_scr[...] = inj[:, half:]
    p = pq_ref[0, 0:1, :]
    q = pq_ref[0, 1:2, :]
    fwd_lanes = lax.broadcasted_iota(jnp.int32, (n_batch, half), 1) < S5_P
    h0 = h0_ref[0]
    s_re, s_im = h0[:, 0:half], h0[:, half:]
    for k in range(n_chunks):
        rows_f = slice(k * n_batch, (k + 1) * n_batch)
        rows_b = slice((n_chunks - 1 - k) * n_batch, (n_chunks - k) * n_batch)
        sf_re_scr[rows_f, :] = s_re
        sf_im_scr[rows_f, :] = s_im
        sb_re_scr[rows_b, :] = s_re
        sb_im_scr[rows_b, :] = s_im
        i_re = jnp.where(fwd_lanes, i_re_scr[rows_f, :], i_re_scr[rows_b, :])
        i_im = jnp.where(fwd_lanes, i_im_scr[rows_f, :], i_im_scr[rows_b, :])
        s_re, s_im = p * s_re - q * s_im + i_re, p * s_im + q * s_re + i_im
    fin_ref[0] = jnp.concatenate([s_re, s_im], axis=1)
    if want_y:
        fwd = lax.broadcasted_iota(jnp.int32, (1, half), 1) < S5_P
        s_prev = jnp.concatenate(
            [jnp.where(fwd, sf_re_scr[...], sb_re_scr[...]),
             jnp.where(fwd, sf_im_scr[...], sb_im_scr[...])], axis=1).astype(BF16)
        y = jnp.dot(tt_ref[0], x, preferred_element_type=F32)
        y = y + lax.dot_general(rt_ref[0], s_prev, _NT, preferred_element_type=F32)
        y_ref[0] = y.astype(BF16)


def _s5(xg, mats, h0, n_batch, want_y):
    tt, inj, rt, pq = mats
    n = xg.shape[2]
    nc = n // n_batch
    if h0 is None:
        h0 = jnp.zeros((S5_G, n_batch, S5_W), F32)
    grp = lambda shape: pl.BlockSpec((1,) + shape, lambda g: (g, 0, 0))
    sq = grp((S5_W, S5_W))
    if want_y:
        in_specs = [grp((S5_W, n)), sq, sq, sq]
        args = [xg, tt, inj, rt]
    else:
        in_specs = [grp((S5_W, n)), sq]
        args = [xg, inj]
    in_specs += [grp((2, 2 * S5_P)), grp((n_batch, S5_W))]
    args += [pq, h0]
    fin_spec = grp((n_batch, S5_W))
    fin_shape = jax.ShapeDtypeStruct((S5_G, n_batch, S5_W), F32)
    if want_y:
        out_specs = [grp((S5_W, n)), fin_spec]
        out_shape = [jax.ShapeDtypeStruct((S5_G, S5_W, n), BF16), fin_shape]
    else:
        out_specs = fin_spec
        out_shape = fin_shape
    out = pl.pallas_call(
        functools.partial(_s5_kernel, n_chunks=nc, n_batch=n_batch, want_y=want_y),
        grid=(S5_G,),
        in_specs=in_specs,
        out_specs=out_specs,
        out_shape=out_shape,
        scratch_shapes=[pltpu.VMEM((n, 2 * S5_P), F32)] * 6,
        compiler_params=_params("arbitrary"),
        name="s5" if want_y else "s5_states",
    )(*args)
    if not want_y:
        return None, out
    return out[0], out[1]


def _s5out_kernel(y_ref, gwt_ref, gb_ref, o_ref, *tok_scr):
    nb, tt, _ = o_ref.shape
    for j in range(S5_LC):
        z = jax.nn.gelu(
            y_ref[:, j * S5_H:(j + 1) * S5_H, :].reshape(W_MIX, LANES).astype(F32))
        gate = jnp.dot(gwt_ref[...], z.astype(BF16), preferred_element_type=F32) + gb_ref[...]
        yc = z * jax.nn.sigmoid(gate)
        for lb, scr in enumerate(tok_scr):
            piece = yc[lb * LANES:(lb + 1) * LANES, :].T
            for c in range(tt // S5_LC):
                tok = c * S5_LC + j
                scr[tok * nb:(tok + 1) * nb, :] = piece[c * nb:(c + 1) * nb, :]
    for lb, scr in enumerate(tok_scr):
        for b in range(nb):
            o_ref[b, :, lb * LANES:(lb + 1) * LANES] = scr[pl.ds(b, tt, stride=nb), :].astype(BF16)


def _s5out(y, glu_wt, glu_b, n_batch):
    n = y.shape[2]
    seq = (n // n_batch) * S5_LC
    tt = (LANES // n_batch) * S5_LC
    out = pl.pallas_call(
        _s5out_kernel,
        grid=(seq // tt,),
        in_specs=[
            pl.BlockSpec((S5_G, S5_W, LANES), lambda k: (0, 0, k)),
            _const_spec((W_MIX, W_MIX)),
            _const_spec((W_MIX, 1)),
        ],
        out_specs=pl.BlockSpec((n_batch, tt, W_MIX), lambda k: (0, k, 0)),
        out_shape=jax.ShapeDtypeStruct((n_batch, seq, W_MIX), BF16),
        scratch_shapes=[pltpu.VMEM((n_batch * tt, LANES), F32)] * (W_MIX // LANES),
        compiler_params=_params("arbitrary"),
        name="s5out",
    )(y, glu_wt, glu_b)
    return out.reshape(n_batch * seq, W_MIX)


def _route(logits_t, rb_ref):
    s = [jax.nn.sigmoid(logits_t[e:e + 1, :]) for e in range(N_EXPERTS)]
    bz = [s[e] + rb_ref[e] for e in range(N_EXPERTS)]
    gs = []
    for g in range(N_GROUPS):
        v = bz[g * EXPERTS_PER_GROUP:(g + 1) * EXPERTS_PER_GROUP]
        best = None
        for i in range(EXPERTS_PER_GROUP):
            for j in range(i + 1, EXPERTS_PER_GROUP):
                pair = v[i] + v[j]
                best = pair if best is None else jnp.maximum(best, pair)
        gs.append(best)
    gbest = gs[0]
    gi = jnp.zeros_like(gbest, dtype=jnp.int32)
    for g in range(1, N_GROUPS):
        upd = gs[g] > gbest
        gi = jnp.where(upd, g, gi)
        gbest = jnp.where(upd, gs[g], gbest)

    def pick(vals, k):
        out = vals[k]
        for g in range(1, N_GROUPS):
            out = jnp.where(gi == g, vals[g * EXPERTS_PER_GROUP + k], out)
        return out

    v = [pick(bz, k) for k in range(EXPERTS_PER_GROUP)]
    sc = [pick(s, k) for k in range(EXPERTS_PER_GROUP)]

    def argmax4(vals):
        best, idx = vals[0], jnp.zeros_like(gi)
        for k in range(1, EXPERTS_PER_GROUP):
            upd = vals[k] > best
            idx = jnp.where(upd, k, idx)
            best = jnp.where(upd, vals[k], best)
        return idx

    i1 = argmax4(v)
    v2 = [jnp.where(i1 == k, -jnp.inf, v[k]) for k in range(EXPERTS_PER_GROUP)]
    i2 = argmax4(v2)

    def sel(vals, idx):
        out = vals[0]
        for k in range(1, EXPERTS_PER_GROUP):
            out = jnp.where(idx == k, vals[k], out)
        return out

    w1 = sel(sc, i1)
    w2 = sel(sc, i2)
    tot = w1 + w2
    w1 = w1 / tot
    w2 = w2 / tot
    first_low = i1 < i2
    lo = jnp.where(first_low, i1, i2)
    hi = jnp.where(first_low, i2, i1)
    wa = jnp.where(first_low, w1, w2)
    wb = jnp.where(first_low, w2, w1)
    base = gi * EXPERTS_PER_GROUP
    pair = jnp.where(lo == 0, 0, jnp.where(lo == 1, 3, 5)) + (hi - lo - 1)
    cls = gi * N_PAIRS + pair
    return (base + lo).astype(F32), (base + hi).astype(F32), wa, wb, cls


def _mix_kernel(*refs, conv_w, n_in, n_shared):
    (h_ref, yc_ref, mod_ref, n1_ref, n2_ref, win_ref, sgug_ref, sguw_ref, sgub_ref,
     convw_ref, wbr_ref, wout_ref, rwt_ref, tri_ref, rb_ref) = refs[:n_in]
    h2_ref, hn2_ref, route_ref, counts_ref, ya_scr, count_scr = refs[n_in + n_shared:]
    tm = h_ref.shape[0]
    x = h_ref[...]
    xn = _rms(x, n1_ref[...]) * (1.0 + mod_ref[0, 1:2, :]) + mod_ref[0, 0:1, :]
    xb = xn.astype(BF16)

    def proj(c0, width):
        return jnp.dot(xb, win_ref[:, c0:c0 + width], preferred_element_type=F32)

    def gate(branch):
        return jax.nn.sigmoid(proj(GATE0 + branch * D_MODEL, D_MODEL))

    u = jax.nn.gelu(proj(0, W_MIX))
    v = jax.nn.gelu(proj(W_MIX, W_MIX))
    vc = v - jnp.mean(v, axis=-1, keepdims=True)
    v = vc * lax.rsqrt(jnp.mean(vc * vc, axis=-1, keepdims=True) + EPS) * sgug_ref[...]
    vb = v.astype(BF16)
    for ck in range(tm // CHUNK):
        r0 = ck * CHUNK
        for hh in range(SGU_HEADS):
            c0 = hh * SGU_HEAD_W
            mixed = jnp.dot(sguw_ref[hh], vb[r0:r0 + CHUNK, c0:c0 + SGU_HEAD_W],
                            preferred_element_type=F32) + sgub_ref[:, hh:hh + 1]
            ya_scr[r0:r0 + CHUNK, c0:c0 + SGU_HEAD_W] = (
                u[r0:r0 + CHUNK, c0:c0 + SGU_HEAD_W] * mixed).astype(BF16)
    merged = gate(0) * jnp.dot(ya_scr[...], wbr_ref[0], preferred_element_type=F32)

    m = proj(3 * W_MIX, W_MIX) * proj(4 * W_MIX, W_MIX)
    pos = lax.broadcasted_iota(jnp.int32, (tm, 1), 0) % conv_w
    prev = jnp.where(pos == 0, 0.0, pltpu.roll(m, 1, axis=0))
    nxt = jnp.where(pos == conv_w - 1, 0.0, pltpu.roll(m, tm - 1, axis=0))
    yb = proj(2 * W_MIX, W_MIX) * (
        convw_ref[0:1, :] * prev + convw_ref[1:2, :] * m + convw_ref[2:3, :] * nxt)
    merged = merged + gate(1) * _bdot(yb, wbr_ref[1])

    merged = merged + gate(2) * jnp.dot(yc_ref[...], wbr_ref[2], preferred_element_type=F32)

    h2 = x + mod_ref[0, 2:3, :] * _bdot(merged, wout_ref[...])
    h2_ref[...] = h2
    hn2 = _rms(h2, n2_ref[...]) * (1.0 + mod_ref[0, 4:5, :]) + mod_ref[0, 3:4, :]
    hn2_ref[...] = hn2.astype(BF16)
    logits_t = lax.dot_general(rwt_ref[...], hn2, _NT, precision=HIGHEST, preferred_element_type=F32)
    ea, eb, wa, wb, cls = _route(logits_t, rb_ref)

    @pl.when(pl.program_id(0) == 0)
    def _():
        count_scr[...] = jnp.zeros_like(count_scr)

    onehot = lax.broadcasted_iota(jnp.int32, (CLASS_ROWS, tm), 0) == cls
    within = jnp.dot(onehot.astype(BF16), tri_ref[...], preferred_element_type=F32)
    before = count_scr[:, 0:1]
    rank = jnp.sum(jnp.where(onehot, within + before, 0.0), axis=0, keepdims=True) - 1.0
    total = before + within[:, tm - 1:tm]
    count_scr[...] = jnp.broadcast_to(total, count_scr.shape)
    counts_ref[...] = jnp.broadcast_to(total, counts_ref.shape)
    zero = jnp.zeros_like(wa)
    route_ref[...] = jnp.concatenate([ea, eb, wa, wb, rank, cls.astype(F32), zero, zero], axis=0)


def _mix(h, yc, mod, lw, seq, conv_w, router_wt, router_b, total_rows, row0=0, shared=None,
         mod_rows_equal=False):
    t = h.shape[0]
    tm = TM_TOK if mod_rows_equal else min(TM_TOK, seq)
    tok = pl.BlockSpec((tm, D_MODEL), lambda i: (i, 0))
    in_specs = [
        tok,
        pl.BlockSpec((tm, W_MIX), lambda i: (i, 0)),
        pl.BlockSpec((1, N_MOD, D_MODEL), lambda i: ((i * tm) // seq, 0, 0)),
        _const_spec((1, D_MODEL)),
        _const_spec((1, D_MODEL)),
        _const_spec((D_MODEL, D_IN)),
        _const_spec((1, W_MIX)),
        _const_spec((SGU_HEADS, CHUNK, CHUNK)),
        _const_spec((CHUNK, SGU_HEADS)),
        _const_spec((3, W_MIX)),
        _const_spec((N_BRANCH, W_MIX, D_MODEL)),
        _const_spec((D_MODEL, D_MODEL)),
        _const_spec((N_EXPERTS, D_MODEL)),
        _const_spec((tm, tm)),
        pl.BlockSpec(memory_space=pltpu.SMEM),
    ]
    blk0 = row0 // tm
    out_specs = [tok, pl.BlockSpec((tm, D_MODEL), lambda i: (blk0 + i, 0)),
                 pl.BlockSpec((8, tm), lambda i: (0, blk0 + i)),
                 pl.BlockSpec((CLASS_ROWS, LANES), lambda i: (0, 0))]
    out_shape = [
        jax.ShapeDtypeStruct((t, D_MODEL), F32),
        jax.ShapeDtypeStruct((total_rows, D_MODEL), BF16),
        jax.ShapeDtypeStruct((8, total_rows), F32),
        jax.ShapeDtypeStruct((CLASS_ROWS, LANES), F32),
    ]
    tri = (jnp.arange(tm)[:, None] <= jnp.arange(tm)[None, :]).astype(BF16)
    args = [h, yc, mod, lw["norm1_g"], lw["norm2_g"], lw["w_in"], lw["sgu_norm_g"], lw["sgu_w"],
            lw["sgu_b_t"], lw["conv_w"], lw["w_branch"], lw["w_out"], router_wt, tri, router_b]
    n_in = len(args)
    aliases = {}
    if shared is not None:
        in_specs += [pl.BlockSpec(memory_space=pl.ANY)] * 2
        args += list(shared)
        aliases = {n_in: 1, n_in + 1: 2}
    h2, hn2, route, counts = pl.pallas_call(
        functools.partial(_mix_kernel, conv_w=conv_w, n_in=n_in, n_shared=len(aliases)),
        grid=(t // tm,),
        in_specs=in_specs,
        out_specs=out_specs,
        out_shape=out_shape,
        scratch_shapes=[pltpu.VMEM((tm, W_MIX), BF16), pltpu.VMEM((CLASS_ROWS, LANES), F32)],
        input_output_aliases=aliases,
        compiler_params=_params("arbitrary"),
        name="mix",
    )(*args)
    return h2, hn2, route, counts[:N_CLASSES, 0].astype(jnp.int32)


def _moe_kernel(ea_ref, eb_ref, valid_ref, x_ref, w_ref, ga_ref, ua_ref, da_ref,
                gb_ref, ub_ref, db_ref, o_ref):
    i = pl.program_id(0)

    @pl.when(valid_ref[i] > 0)
    def _():
        x = x_ref[...]

        def expert(g_ref, u_ref, d_ref):
            g = jnp.dot(x, g_ref[0], preferred_element_type=F32)
            act = g * jax.nn.sigmoid(g) * jnp.dot(x, u_ref[0], preferred_element_type=F32)
            return _bdot(act, d_ref[0])

        o_ref[...] = (w_ref[:, 1:2] * expert(ga_ref, ua_ref, da_ref)
                      + w_ref[:, 2:3] * expert(gb_ref, ub_ref, db_ref)).astype(BF16)

    @pl.when(valid_ref[i] == 0)
    def _():
        o_ref[...] = jnp.zeros_like(o_ref)


def _moe(hn2, route, part_rows, part_counts, w_g, w_u, w_d):
    t = hn2.shape[0]
    cls = route[5].astype(jnp.int32)
    rank = route[4].astype(jnp.int32)
    row0 = 0
    for rows, cnt in zip(part_rows[:-1], part_counts[:-1]):
        row0 += rows
        rank = rank + jnp.where(jnp.arange(t, dtype=jnp.int32) >= row0, cnt[cls], 0)
    counts = sum(part_counts[1:], part_counts[0])
    tm = TM_MOE
    nt = t // tm + N_CLASSES
    tp = nt * tm
    padded = ((counts + tm - 1) // tm) * tm
    ends = jnp.cumsum(padded)
    offs = ends - padded
    pos = offs[cls] + rank
    starts = jnp.arange(nt, dtype=jnp.int32) * tm
    tile_cls = jnp.sum((starts[:, None] >= ends[None, :]).astype(jnp.int32), axis=1)
    valid = (tile_cls < N_CLASSES).astype(jnp.int32)
    tile_cls = jnp.minimum(tile_cls, N_CLASSES - 1)
    pair_lo = jnp.array([0, 0, 0, 1, 1, 2], jnp.int32)
    pair_hi = jnp.array([1, 2, 3, 2, 3, 3], jnp.int32)
    tile_ea = (tile_cls // N_PAIRS) * EXPERTS_PER_GROUP + pair_lo[tile_cls % N_PAIRS]
    tile_eb = (tile_cls // N_PAIRS) * EXPERTS_PER_GROUP + pair_hi[tile_cls % N_PAIRS]
    slot = jnp.stack([jnp.arange(t, dtype=F32), route[2], route[3], jnp.zeros((t,), F32)], axis=1)
    spread = (jnp.arange(tp, dtype=jnp.int32) % t).astype(F32)
    empty = jnp.stack([spread] + [jnp.zeros((tp,), F32)] * 3, axis=1)
    ws = empty.at[pos].set(slot, mode="promise_in_bounds", unique_indices=True)
    xs = hn2.at[ws[:, 0].astype(jnp.int32)].get(mode="promise_in_bounds")

    wspec_in = lambda sel: pl.BlockSpec((1, D_MODEL, D_FF), sel)
    wspec_out = lambda sel: pl.BlockSpec((1, D_FF, D_MODEL), sel)
    sel_a = lambda i, a, b, v: (a[i], 0, 0)
    sel_b = lambda i, a, b, v: (b[i], 0, 0)
    grid_spec = pltpu.PrefetchScalarGridSpec(
        num_scalar_prefetch=3,
        grid=(nt,),
        in_specs=[
            pl.BlockSpec((tm, D_MODEL), lambda i, a, b, v: (i, 0)),
            pl.BlockSpec((tm, 4), lambda i, a, b, v: (i, 0)),
            wspec_in(sel_a), wspec_in(sel_a), wspec_out(sel_a),
            wspec_in(sel_b), wspec_in(sel_b), wspec_out(sel_b),
        ],
        out_specs=pl.BlockSpec((tm, D_MODEL), lambda i, a, b, v: (i, 0)),
    )
    out = pl.pallas_call(
        _moe_kernel,
        grid_spec=grid_spec,
        out_shape=jax.ShapeDtypeStruct((tp, D_MODEL), BF16),
        compiler_params=_params("arbitrary"),
        name="moe",
    )(tile_ea, tile_eb, valid, xs, ws, w_g, w_u, w_d, w_g, w_u, w_d)
    return out.at[pos].get(mode="promise_in_bounds", unique_indices=True)


def _final_kernel(h_ref, moe_ref, mod_ref, g_ref, o_ref):
    x = h_ref[...] + mod_ref[0, 5:6, :] * moe_ref[...].astype(F32)
    o_ref[...] = _rms(x, g_ref[...])


def _final(h, moe, mod, g, seq):
    t = h.shape[0]
    tm = TM_TOK
    per_seq = seq // tm
    tok = pl.BlockSpec((tm, D_MODEL), lambda i: (i, 0))
    return pl.pallas_call(
        _final_kernel,
        grid=(t // tm,),
        in_specs=[tok, tok, pl.BlockSpec((1, N_MOD, D_MODEL), lambda i: (i // per_seq, 0, 0)),
                  _const_spec((1, D_MODEL))],
        out_specs=tok,
        out_shape=jax.ShapeDtypeStruct((t, D_MODEL), F32),
        compiler_params=_params("arbitrary"),
        name="final",
    )(h, moe, mod, g.reshape(1, D_MODEL))


def kernel(x, c, ctx, c_ctx, w_mod, b_mod, norm1_g, norm2_g, w_in, sgu_norm_g, sgu_w, sgu_b,
           conv_w, s5_lam_re, s5_lam_im, s5_log_dt, s5_b_re, s5_b_im, s5_c_re, s5_c_im, s5_d,
           glu_w, glu_b, w_branch, w_out, router_w, router_b, exp_w_gate, exp_w_up, exp_w_down,
           final_norm_g):
    nb, seq, _ = x.shape
    ctx_len = ctx.shape[1]
    h = x.reshape(nb * seq, D_MODEL)
    hc = ctx.reshape(nb * ctx_len, D_MODEL)

    mod_rows = 24
    cvec = jnp.zeros((mod_rows, D_MODEL), F32).at[:nb].set(c).at[nb].set(c_ctx)
    mods = _adaln(cvec, w_mod, b_mod).reshape(DEPTH, mod_rows, N_MOD, D_MODEL)
    router_wt = router_w.T

    all_mats = jax.vmap(_s5_mats)(s5_lam_re, s5_lam_im, s5_log_dt, s5_b_re, s5_b_im,
                                  s5_c_re, s5_c_im, s5_d)
    n_lat, n_ctx = nb * seq, nb * ctx_len
    moe = None
    mod = mod_c = None
    for l in range(DEPTH):
        last = l == DEPTH - 1
        pmod, pmod_c = mod, mod_c
        mod = mods[l, :nb]
        mod_c = jnp.broadcast_to(mods[l, nb][None], (nb, N_MOD, D_MODEL))
        w_in_b = _cast_layer(w_in, l)
        w_s5 = w_in_b[:, S5_COL0:S5_COL0 + W_MIX]
        lw = {
            "norm1_g": norm1_g[l].reshape(1, D_MODEL),
            "norm2_g": norm2_g[l].reshape(1, D_MODEL),
            "w_in": w_in_b,
            "sgu_norm_g": sgu_norm_g[l].reshape(1, W_MIX),
            "sgu_w": sgu_w[l].astype(BF16),
            "sgu_b_t": sgu_b[l].T,
            "conv_w": conv_w[l],
            "w_branch": _cast_layer(w_branch, l),
            "w_out": _cast_layer(w_out, l),
        }
        glu_wt = glu_w[l].T.astype(BF16)
        glu_bc = glu_b[l].reshape(W_MIX, 1)
        mats = tuple(m[l] for m in all_mats)
        w_g = _cast_layer(exp_w_gate, l)
        w_u = _cast_layer(exp_w_up, l)
        w_d = _cast_layer(exp_w_down, l)

        hc, xg_c = _pre(hc, mod_c, norm1_g[l], w_s5, ctx_len, moe=moe, moe_row0=n_lat, pmod=pmod_c)
        y_c, fin_c = _s5(xg_c, mats, None, nb, want_y=not last)
        h, xg = _pre(h, mod, norm1_g[l], w_s5, seq, moe=moe, pmod=pmod)
        y, _ = _s5(xg, mats, fin_c, nb, want_y=True)
        yc = _s5out(y, glu_wt, glu_bc, nb)
        if last:
            h, hn2, route, cnt = _mix(h, yc, mod, lw, seq, GRID_W, router_wt, router_b, n_lat)
            moe = _moe(hn2, route, [n_lat], [cnt], w_g, w_u, w_d)
        else:
            n_all = n_lat + n_ctx
            h, hn2, route, cnt = _mix(h, yc, mod, lw, seq, GRID_W, router_wt, router_b, n_all)
            yc_c = _s5out(y_c, glu_wt, glu_bc, nb)
            hc, hn2, route, cnt_c = _mix(hc, yc_c, mod_c, lw, ctx_len, ctx_len, router_wt, router_b,
                                         n_all, row0=n_lat, shared=(hn2, route), mod_rows_equal=True)
            moe = _moe(hn2, route, [n_lat, n_ctx], [cnt, cnt_c], w_g, w_u, w_d)
    out = _final(h, moe, mod, final_norm_g, seq)
    return out.reshape(nb, seq, D_MODEL)
```

```python
import functools

import jax
import jax.numpy as jnp
from jax import lax
from jax.experimental import pallas as pl
from jax.experimental.pallas import tpu as pltpu

F32 = jnp.float32
BF16 = jnp.bfloat16
HIGHEST = lax.Precision.HIGHEST

D_MODEL = 1024
DEPTH = 2
EPS = 1e-6
W_MIX = 512
N_BRANCH = 3
CHUNK = 128
SGU_HEADS = 4
SGU_HEAD_W = W_MIX // SGU_HEADS
GRID_W = 64
S5_H = 16
S5_G = W_MIX // S5_H
S5_P = 64
S5_LC = 16
S5_W = S5_LC * S5_H
S5_COL0 = 5 * W_MIX
GATE0 = 6 * W_MIX
D_IN = GATE0 + N_BRANCH * D_MODEL
N_EXPERTS = 16
EXPERTS_PER_GROUP = 4
N_GROUPS = 4
D_FF = 512
N_PAIRS = 6
N_CLASSES = N_GROUPS * N_PAIRS
CLASS_ROWS = 32
N_MOD = 6
LANES = 128

TM_TOK = 512
TM_MOE = 256
MERGE_COLS = 512
PRE_B_PER_DOT = 4
CAST_BLOCK_ELEMS = 1 << 20
VMEM_LIMIT = 56 * 1024 * 1024

_NT = (((1,), (1,)), ((), ()))
_TN = (((0,), (0,)), ((), ()))


def _rms(x, g):
    return x * lax.rsqrt(jnp.mean(x * x, axis=-1, keepdims=True) + EPS) * g


def _bdot(a, b):
    return jnp.dot(a.astype(BF16), b, preferred_element_type=F32)


def _const_spec(shape):
    nd = len(shape)
    return pl.BlockSpec(shape, lambda *_: (0,) * nd, pipeline_mode=pl.Buffered(1))


def _params(*sem):
    return pltpu.CompilerParams(dimension_semantics=sem, vmem_limit_bytes=VMEM_LIMIT)


def _cast_kernel(x_ref, o_ref):
    o_ref[...] = x_ref[0].astype(BF16)


def _cast_layer(a, l):
    cols = a.shape[-1]
    rows = a[0].size // cols
    tr = 1 << ((CAST_BLOCK_ELEMS // cols).bit_length() - 1)
    while rows % tr:
        tr //= 2
    out = pl.pallas_call(
        _cast_kernel,
        grid=(rows // tr,),
        in_specs=[pl.BlockSpec((1, tr, cols), lambda i: (l, i, 0))],
        out_specs=pl.BlockSpec((tr, cols), lambda i: (i, 0)),
        out_shape=jax.ShapeDtypeStruct((rows, cols), BF16),
        compiler_params=_params("arbitrary"),
        name="cast",
    )(a.reshape(a.shape[0], rows, cols))
    return out.reshape(a.shape[1:])


def _adaln_kernel(c_ref, w_ref, b_ref, o_ref):
    c = c_ref[...]
    s = c * jax.nn.sigmoid(c)
    o_ref[0] = jnp.dot(s, w_ref[0], precision=HIGHEST, preferred_element_type=F32) + b_ref[0]


def _adaln(cvec, w_mod, b_mod):
    rows = cvec.shape[0]
    tn = 1536
    n_mod = N_MOD * D_MODEL
    return pl.pallas_call(
        _adaln_kernel,
        grid=(DEPTH, n_mod // tn),
        in_specs=[
            pl.BlockSpec((rows, D_MODEL), lambda l, j: (0, 0)),
            pl.BlockSpec((1, D_MODEL, tn), lambda l, j: (l, 0, j)),
            pl.BlockSpec((1, 1, tn), lambda l, j: (l, 0, j)),
        ],
        out_specs=pl.BlockSpec((1, rows, tn), lambda l, j: (l, 0, j)),
        out_shape=jax.ShapeDtypeStruct((DEPTH, rows, n_mod), F32),
        compiler_params=_params("arbitrary", "arbitrary"),
        name="adaln",
    )(cvec, w_mod, b_mod.reshape(DEPTH, 1, n_mod))


def _pre_kernel(*refs, residual):
    if residual:
        h_ref, moe_ref, pmod_ref, mod_ref, g_ref, w_ref, hnew_ref, xo_ref = refs[:8]
    else:
        h_ref, mod_ref, g_ref, w_ref, xo_ref = refs[:5]
    cu_scr = refs[-W_MIX // LANES:]
    nb, tt, _ = h_ref.shape
    for b0 in range(0, nb, PRE_B_PER_DOT):
        bs = slice(b0, b0 + PRE_B_PER_DOT)
        x = h_ref[bs]
        if residual:
            x = x + pmod_ref[bs, 5:6, :] * moe_ref[bs].astype(F32)
            hnew_ref[bs] = x
        xn = _rms(x, g_ref[...]) * (1.0 + mod_ref[bs, 1:2, :]) + mod_ref[bs, 0:1, :]
        cu = _bdot(xn.reshape(PRE_B_PER_DOT * tt, D_MODEL), w_ref[...])
        for lb, scr in enumerate(cu_scr):
            scr[b0 * tt:(b0 + PRE_B_PER_DOT) * tt, :] = cu[:, lb * LANES:(lb + 1) * LANES]
    groups_per_block = LANES // S5_H
    for i in range(S5_LC):
        for lb, scr in enumerate(cu_scr):
            piece = jnp.concatenate(
                [scr[pl.ds(c * S5_LC + i, nb, stride=tt), :] for c in range(tt // S5_LC)], axis=0)
            xo_ref[lb * groups_per_block:(lb + 1) * groups_per_block, i * S5_H:(i + 1) * S5_H, :] = (
                piece.T.astype(BF16).reshape(groups_per_block, S5_H, LANES))


def _pre(h, mod, norm_g, w_s5, seq, moe=None, moe_row0=0, pmod=None):
    t = h.shape[0]
    nb = t // seq
    tt = (LANES // nb) * S5_LC
    view = lambda a: a.reshape(a.shape[0] // seq, seq, D_MODEL)
    tok = pl.BlockSpec((nb, tt, D_MODEL), lambda k: (0, k, 0))
    modspec = _const_spec((nb, N_MOD, D_MODEL))
    residual = moe is not None
    in_specs = [tok]
    args = [view(h)]
    if residual:
        moe_blk = moe_row0 // t
        in_specs += [pl.BlockSpec((nb, tt, D_MODEL), lambda k: (moe_blk, k, 0)), modspec]
        args += [view(moe), pmod]
    in_specs += [modspec, _const_spec((1, D_MODEL)), _const_spec((D_MODEL, W_MIX))]
    args += [mod, norm_g.reshape(1, D_MODEL), w_s5]
    xo_spec = pl.BlockSpec((S5_G, S5_W, LANES), lambda k: (0, 0, k))
    xo_shape = jax.ShapeDtypeStruct((S5_G, S5_W, t // S5_LC), BF16)
    if residual:
        out_specs = [tok, xo_spec]
        out_shape = [jax.ShapeDtypeStruct((nb, seq, D_MODEL), F32), xo_shape]
    else:
        out_specs = xo_spec
        out_shape = xo_shape
    out = pl.pallas_call(
        functools.partial(_pre_kernel, residual=residual),
        grid=(seq // tt,),
        in_specs=in_specs,
        out_specs=out_specs,
        out_shape=out_shape,
        scratch_shapes=[pltpu.VMEM((nb * tt, LANES), F32)] * (W_MIX // LANES),
        compiler_params=_params("arbitrary"),
        name="pre_res" if residual else "pre",
    )(*args)
    if residual:
        return out[0].reshape(t, D_MODEL), out[1]
    return h, out


def _s5_mats(lam_re, lam_im, log_dt, b_re, b_im, c_re, c_im, dvec):
    dt = jnp.exp(log_dt)[..., None]
    mag = jnp.exp(lam_re * dt)
    ab_re = mag * jnp.cos(lam_im * dt)
    ab_im = mag * jnp.sin(lam_im * dt)
    den = lam_re * lam_re + lam_im * lam_im
    nr = ab_re - 1.0
    f_re = (nr * lam_re + ab_im * lam_im) / den
    f_im = (ab_im * lam_re - nr * lam_im) / den
    bb_re = f_re[..., None] * b_re - f_im[..., None] * b_im
    bb_im = f_re[..., None] * b_im + f_im[..., None] * b_re
    k = jnp.arange(S5_LC + 1, dtype=F32)[:, None, None, None]
    pmag = jnp.exp(lam_re * dt * k)
    pw_re = pmag * jnp.cos(lam_im * dt * k)
    pw_im = pmag * jnp.sin(lam_im * dt * k)
    cp_re = c_re[None] * pw_re[:, :, :, None, :] - c_im[None] * pw_im[:, :, :, None, :]
    cp_im = c_re[None] * pw_im[:, :, :, None, :] + c_im[None] * pw_re[:, :, :, None, :]
    kk = jnp.einsum("kdghp,dgpj->kdghj",
                    jnp.concatenate([cp_re[:S5_LC], -cp_im[:S5_LC]], axis=-1),
                    jnp.concatenate([bb_re, bb_im], axis=-2))
    kf, kb = kk[:, 0], kk[:, 1]
    dmat = jnp.eye(S5_H, dtype=F32)[None] * dvec.reshape(S5_G, S5_H)[:, :, None]
    table = jnp.concatenate(
        [kf[:0:-1], (kf[0] + kb[0] + dmat)[None], kb[1:], jnp.zeros_like(kf[:1])], axis=0)
    table = jnp.transpose(table, (1, 2, 0, 3)).reshape(S5_G, S5_H, 2 * S5_W)
    tt = jnp.stack(
        [table[:, :, (S5_LC - 1 - j) * S5_H:(S5_LC - 1 - j) * S5_H + S5_W] for j in range(S5_LC)],
        axis=1).reshape(S5_G, S5_W, S5_W)

    rev = jnp.arange(S5_LC - 1, -1, -1)
    fwd = jnp.arange(S5_LC)

    def lanes4(f_re_part, b_re_part, f_im_part, b_im_part):
        return jnp.concatenate([f_re_part, b_re_part, f_im_part, b_im_part], axis=-1)

    def step_major(a):
        return jnp.transpose(a, (1, 0, 2))

    def chan_major(a):
        return jnp.transpose(a, (0, 2, 1))

    pr_f, pi_f = step_major(pw_re[rev, 0]), step_major(pw_im[rev, 0])
    pr_b, pi_b = step_major(pw_re[fwd, 1]), step_major(pw_im[fwd, 1])
    br_f, bi_f = chan_major(bb_re[0]), chan_major(bb_im[0])
    br_b, bi_b = chan_major(bb_re[1]), chan_major(bb_im[1])
    inj = (lanes4(pr_f, pr_b, pr_f, pr_b)[:, :, None, :] * lanes4(br_f, br_b, bi_f, bi_b)[:, None]
           + lanes4(-pi_f, -pi_b, pi_f, pi_b)[:, :, None, :] * lanes4(bi_f, bi_b, br_f, br_b)[:, None]
           ).reshape(S5_G, S5_W, 4 * S5_P)
    qr_f, qi_f = step_major(pw_re[fwd + 1, 0]), step_major(pw_im[fwd + 1, 0])
    qr_b, qi_b = step_major(pw_re[S5_LC - fwd, 1]), step_major(pw_im[S5_LC - fwd, 1])
    rt = (lanes4(qr_f, qr_b, -qi_f, -qi_b)[:, :, None, :]
          * lanes4(c_re[0], c_re[1], c_re[0], c_re[1])[:, None]
          + lanes4(-qi_f, -qi_b, -qr_f, -qr_b)[:, :, None, :]
          * lanes4(c_im[0], c_im[1], c_im[0], c_im[1])[:, None]).reshape(S5_G, S5_W, 4 * S5_P)
    a_re, a_im = pw_re[S5_LC], pw_im[S5_LC]
    pvec = jnp.concatenate([a_re[0], a_re[1]], axis=-1)
    qvec = jnp.concatenate([a_im[0], a_im[1]], axis=-1)
    pq = jnp.stack([pvec, qvec], axis=1)
    return tt.astype(BF16), inj.astype(BF16), rt.astype(BF16), pq


def _s5_kernel(*refs, n_chunks, n_batch, want_y):
    if want_y:
        x_ref, tt_ref, inj_ref, rt_ref, pq_ref, h0_ref, y_ref, fin_ref = refs[:8]
    else:
        x_ref, inj_ref, pq_ref, h0_ref, fin_ref = refs[:5]
    i_re_scr, i_im_scr, sf_re_scr, sf_im_scr, sb_re_scr, sb_im_scr = refs[-6:]
    half = 2 * S5_P
    x = x_ref[0]
    inj = lax.dot_general(x, inj_ref[0], _TN, preferred_element_type=F32)
    i_re_scr[...] = inj[:, 0:half]
    i_im_scr[...] = inj[:, half:]
    p = pq_ref[0, 0:1, :]
    q = pq_ref[0, 1:2, :]
    fwd_lanes = lax.broadcasted_iota(jnp.int32, (n_batch, half), 1) < S5_P
    h0 = h0_ref[0]
    s_re, s_im = h0[:, 0:half], h0[:, half:]
    for k in range(n_chunks):
        rows_f = slice(k * n_batch, (k + 1) * n_batch)
        rows_b = slice((n_chunks - 1 - k) * n_batch, (n_chunks - k) * n_batch)
        sf_re_scr[rows_f, :] = s_re
        sf_im_scr[rows_f, :] = s_im
        sb_re_scr[rows_b, :] = s_re
        sb_im_scr[rows_b, :] = s_im
        i_re = jnp.where(fwd_lanes, i_re_scr[rows_f, :], i_re_scr[rows_b, :])
        i_im = jnp.where(fwd_lanes, i_im_scr[rows_f, :], i_im_scr[rows_b, :])
        s_re, s_im = p * s_re - q * s_im + i_re, p * s_im + q * s_re + i_im
    fin_ref[0] = jnp.concatenate([s_re, s_im], axis=1)
    if want_y:
        fwd = lax.broadcasted_iota(jnp.int32, (1, half), 1) < S5_P
        s_prev = jnp.concatenate(
            [jnp.where(fwd, sf_re_scr[...], sb_re_scr[...]),
             jnp.where(fwd, sf_im_scr[...], sb_im_scr[...])], axis=1).astype(BF16)
        y = jnp.dot(tt_ref[0], x, preferred_element_type=F32)
        y = y + lax.dot_general(rt_ref[0], s_prev, _NT, preferred_element_type=F32)
        y_ref[0] = y.astype(BF16)


def _s5(xg, mats, h0, n_batch, want_y):
    tt, inj, rt, pq = mats
    n = xg.shape[2]
    nc = n // n_batch
    if h0 is None:
        h0 = jnp.zeros((S5_G, n_batch, S5_W), F32)
    grp = lambda shape: pl.BlockSpec((1,) + shape, lambda g: (g, 0, 0))
    sq = grp((S5_W, S5_W))
    if want_y:
        in_specs = [grp((S5_W, n)), sq, sq, sq]
        args = [xg, tt, inj, rt]
    else:
        in_specs = [grp((S5_W, n)), sq]
        args = [xg, inj]
    in_specs += [grp((2, 2 * S5_P)), grp((n_batch, S5_W))]
    args += [pq, h0]
    fin_spec = grp((n_batch, S5_W))
    fin_shape = jax.ShapeDtypeStruct((S5_G, n_batch, S5_W), F32)
    if want_y:
        out_specs = [grp((S5_W, n)), fin_spec]
        out_shape = [jax.ShapeDtypeStruct((S5_G, S5_W, n), BF16), fin_shape]
    else:
        out_specs = fin_spec
        out_shape = fin_shape
    out = pl.pallas_call(
        functools.partial(_s5_kernel, n_chunks=nc, n_batch=n_batch, want_y=want_y),
        grid=(S5_G,),
        in_specs=in_specs,
        out_specs=out_specs,
        out_shape=out_shape,
        scratch_shapes=[pltpu.VMEM((n, 2 * S5_P), F32)] * 6,
        compiler_params=_params("arbitrary"),
        name="s5" if want_y else "s5_states",
    )(*args)
    if not want_y:
        return None, out
    return out[0], out[1]


def _s5out_kernel(y_ref, gwt_ref, gb_ref, o_ref, *tok_scr):
    nb, tt, _ = o_ref.shape
    for j in range(S5_LC):
        z = jax.nn.gelu(
            y_ref[:, j * S5_H:(j + 1) * S5_H, :].reshape(W_MIX, LANES).astype(F32))
        gate = jnp.dot(gwt_ref[...], z.astype(BF16), preferred_element_type=F32) + gb_ref[...]
        yc = z * jax.nn.sigmoid(gate)
        for lb, scr in enumerate(tok_scr):
            piece = yc[lb * LANES:(lb + 1) * LANES, :].T
            for c in range(tt // S5_LC):
                tok = c * S5_LC + j
                scr[tok * nb:(tok + 1) * nb, :] = piece[c * nb:(c + 1) * nb, :]
    for lb, scr in enumerate(tok_scr):
        for b in range(nb):
            o_ref[b, :, lb * LANES:(lb + 1) * LANES] = scr[pl.ds(b, tt, stride=nb), :].astype(BF16)


def _s5out(y, glu_wt, glu_b, n_batch):
    n = y.shape[2]
    seq = (n // n_batch) * S5_LC
    tt = (LANES // n_batch) * S5_LC
    out = pl.pallas_call(
        _s5out_kernel,
        grid=(seq // tt,),
        in_specs=[
            pl.BlockSpec((S5_G, S5_W, LANES), lambda k: (0, 0, k)),
            _const_spec((W_MIX, W_MIX)),
            _const_spec((W_MIX, 1)),
        ],
        out_specs=pl.BlockSpec((n_batch, tt, W_MIX), lambda k: (0, k, 0)),
        out_shape=jax.ShapeDtypeStruct((n_batch, seq, W_MIX), BF16),
        scratch_shapes=[pltpu.VMEM((n_batch * tt, LANES), F32)] * (W_MIX // LANES),
        compiler_params=_params("arbitrary"),
        name="s5out",
    )(y, glu_wt, glu_b)
    return out.reshape(n_batch * seq, W_MIX)


def _route(logits_t, rb_ref):
    s = [jax.nn.sigmoid(logits_t[e:e + 1, :]) for e in range(N_EXPERTS)]
    bz = [s[e] + rb_ref[e] for e in range(N_EXPERTS)]
    gs = []
    for g in range(N_GROUPS):
        v = bz[g * EXPERTS_PER_GROUP:(g + 1) * EXPERTS_PER_GROUP]
        best = None
        for i in range(EXPERTS_PER_GROUP):
            for j in range(i + 1, EXPERTS_PER_GROUP):
                pair = v[i] + v[j]
                best = pair if best is None else jnp.maximum(best, pair)
        gs.append(best)
    gbest = gs[0]
    gi = jnp.zeros_like(gbest, dtype=jnp.int32)
    for g in range(1, N_GROUPS):
        upd = gs[g] > gbest
        gi = jnp.where(upd, g, gi)
        gbest = jnp.where(upd, gs[g], gbest)

    def pick(vals, k):
        out = vals[k]
        for g in range(1, N_GROUPS):
            out = jnp.where(gi == g, vals[g * EXPERTS_PER_GROUP + k], out)
        return out

    v = [pick(bz, k) for k in range(EXPERTS_PER_GROUP)]
    sc = [pick(s, k) for k in range(EXPERTS_PER_GROUP)]

    def argmax4(vals):
        best, idx = vals[0], jnp.zeros_like(gi)
        for k in range(1, EXPERTS_PER_GROUP):
            upd = vals[k] > best
            idx = jnp.where(upd, k, idx)
            best = jnp.where(upd, vals[k], best)
        return idx

    i1 = argmax4(v)
    v2 = [jnp.where(i1 == k, -jnp.inf, v[k]) for k in range(EXPERTS_PER_GROUP)]
    i2 = argmax4(v2)

    def sel(vals, idx):
        out = vals[0]
        for k in range(1, EXPERTS_PER_GROUP):
            out = jnp.where(idx == k, vals[k], out)
        return out

    w1 = sel(sc, i1)
    w2 = sel(sc, i2)
    tot = w1 + w2
    w1 = w1 / tot
    w2 = w2 / tot
    first_low = i1 < i2
    lo = jnp.where(first_low, i1, i2)
    hi = jnp.where(first_low, i2, i1)
    wa = jnp.where(first_low, w1, w2)
    wb = jnp.where(first_low, w2, w1)
    base = gi * EXPERTS_PER_GROUP
    pair = jnp.where(lo == 0, 0, jnp.where(lo == 1, 3, 5)) + (hi - lo - 1)
    cls = gi * N_PAIRS + pair
    return (base + lo).astype(F32), (base + hi).astype(F32), wa, wb, cls


def _mix_kernel(*refs, conv_w, n_in, n_shared):
    (h_ref, yc_ref, mod_ref, n1_ref, n2_ref, win_ref, sgug_ref, sguw_ref, sgub_ref,
     convw_ref, wbr_ref, wout_ref, rwt_ref, tri_ref, rb_ref) = refs[:n_in]
    h2_ref, hn2_ref, route_ref, counts_ref, ya_scr, count_scr = refs[n_in + n_shared:]
    tm = h_ref.shape[0]
    x = h_ref[...]
    xn = _rms(x, n1_ref[...]) * (1.0 + mod_ref[0, 1:2, :]) + mod_ref[0, 0:1, :]
    xb = xn.astype(BF16)

    def proj(c0, width):
        return jnp.dot(xb, win_ref[:, c0:c0 + width], preferred_element_type=F32)

    def gate(branch, c0, width):
        return jax.nn.sigmoid(proj(GATE0 + branch * D_MODEL + c0, width))

    u = jax.nn.gelu(proj(0, W_MIX))
    v = jax.nn.gelu(proj(W_MIX, W_MIX))
    vc = v - jnp.mean(v, axis=-1, keepdims=True)
    v = vc * lax.rsqrt(jnp.mean(vc * vc, axis=-1, keepdims=True) + EPS) * sgug_ref[...]
    vb = v.astype(BF16)
    for ck in range(tm // CHUNK):
        r0 = ck * CHUNK
        for hh in range(SGU_HEADS):
            c0 = hh * SGU_HEAD_W
            mixed = jnp.dot(sguw_ref[hh], vb[r0:r0 + CHUNK, c0:c0 + SGU_HEAD_W],
                            preferred_element_type=F32) + sgub_ref[:, hh:hh + 1]
            ya_scr[r0:r0 + CHUNK, c0:c0 + SGU_HEAD_W] = (
                u[r0:r0 + CHUNK, c0:c0 + SGU_HEAD_W] * mixed).astype(BF16)

    m = proj(3 * W_MIX, W_MIX) * proj(4 * W_MIX, W_MIX)
    pos = lax.broadcasted_iota(jnp.int32, (tm, 1), 0) % conv_w
    prev = jnp.where(pos == 0, 0.0, pltpu.roll(m, 1, axis=0))
    nxt = jnp.where(pos == conv_w - 1, 0.0, pltpu.roll(m, tm - 1, axis=0))
    yb = proj(2 * W_MIX, W_MIX) * (
        convw_ref[0:1, :] * prev + convw_ref[1:2, :] * m + convw_ref[2:3, :] * nxt)
    ybb = yb.astype(BF16)

    mix = None
    for c0 in range(0, D_MODEL, MERGE_COLS):
        cols = slice(c0, c0 + MERGE_COLS)
        merged = (
            gate(0, c0, MERGE_COLS) * jnp.dot(ya_scr[...], wbr_ref[0, :, cols], preferred_element_type=F32)
            + gate(1, c0, MERGE_COLS) * jnp.dot(ybb, wbr_ref[1, :, cols], preferred_element_type=F32)
            + gate(2, c0, MERGE_COLS) * jnp.dot(yc_ref[...], wbr_ref[2, :, cols], preferred_element_type=F32))
        part = _bdot(merged, wout_ref[cols, :])
        mix = part if mix is None else mix + part

    h2 = x + mod_ref[0, 2:3, :] * mix
    h2_ref[...] = h2
    hn2 = _rms(h2, n2_ref[...]) * (1.0 + mod_ref[0, 4:5, :]) + mod_ref[0, 3:4, :]
    hn2_ref[...] = hn2.astype(BF16)
    logits_t = lax.dot_general(rwt_ref[...], hn2, _NT, precision=HIGHEST, preferred_element_type=F32)
    ea, eb, wa, wb, cls = _route(logits_t, rb_ref)

    @pl.when(pl.program_id(0) == 0)
    def _():
        count_scr[...] = jnp.zeros_like(count_scr)

    onehot = lax.broadcasted_iota(jnp.int32, (CLASS_ROWS, tm), 0) == cls
    within = jnp.dot(onehot.astype(BF16), tri_ref[...], preferred_element_type=F32)
    before = count_scr[:, 0:1]
    rank = jnp.sum(jnp.where(onehot, within + before, 0.0), axis=0, keepdims=True) - 1.0
    total = before + within[:, tm - 1:tm]
    count_scr[...] = jnp.broadcast_to(total, count_scr.shape)
    counts_ref[...] = jnp.broadcast_to(total, counts_ref.shape)
    zero = jnp.zeros_like(wa)
    route_ref[...] = jnp.concatenate([ea, eb, wa, wb, rank, cls.astype(F32), zero, zero], axis=0)


def _mix(h, yc, mod, lw, seq, conv_w, router_wt, router_b, total_rows, row0=0, shared=None,
         mod_rows_equal=False):
    t = h.shape[0]
    tm = TM_TOK if mod_rows_equal else min(TM_TOK, seq)
    tok = pl.BlockSpec((tm, D_MODEL), lambda i: (i, 0))
    in_specs = [
        tok,
        pl.BlockSpec((tm, W_MIX), lambda i: (i, 0)),
        pl.BlockSpec((1, N_MOD, D_MODEL), lambda i: ((i * tm) // seq, 0, 0)),
        _const_spec((1, D_MODEL)),
        _const_spec((1, D_MODEL)),
        _const_spec((D_MODEL, D_IN)),
        _const_spec((1, W_MIX)),
        _const_spec((SGU_HEADS, CHUNK, CHUNK)),
        _const_spec((CHUNK, SGU_HEADS)),
        _const_spec((3, W_MIX)),
        _const_spec((N_BRANCH, W_MIX, D_MODEL)),
        _const_spec((D_MODEL, D_MODEL)),
        _const_spec((N_EXPERTS, D_MODEL)),
        _const_spec((tm, tm)),
        pl.BlockSpec(memory_space=pltpu.SMEM),
    ]
    blk0 = row0 // tm
    out_specs = [tok, pl.BlockSpec((tm, D_MODEL), lambda i: (blk0 + i, 0)),
                 pl.BlockSpec((8, tm), lambda i: (0, blk0 + i)),
                 pl.BlockSpec((CLASS_ROWS, LANES), lambda i: (0, 0))]
    out_shape = [
        jax.ShapeDtypeStruct((t, D_MODEL), F32),
        jax.ShapeDtypeStruct((total_rows, D_MODEL), BF16),
        jax.ShapeDtypeStruct((8, total_rows), F32),
        jax.ShapeDtypeStruct((CLASS_ROWS, LANES), F32),
    ]
    tri = (jnp.arange(tm)[:, None] <= jnp.arange(tm)[None, :]).astype(BF16)
    args = [h, yc, mod, lw["norm1_g"], lw["norm2_g"], lw["w_in"], lw["sgu_norm_g"], lw["sgu_w"],
            lw["sgu_b_t"], lw["conv_w"], lw["w_branch"], lw["w_out"], router_wt, tri, router_b]
    n_in = len(args)
    aliases = {}
    if shared is not None:
        in_specs += [pl.BlockSpec(memory_space=pl.ANY)] * 2
        args += list(shared)
        aliases = {n_in: 1, n_in + 1: 2}
    h2, hn2, route, counts = pl.pallas_call(
        functools.partial(_mix_kernel, conv_w=conv_w, n_in=n_in, n_shared=len(aliases)),
        grid=(t // tm,),
        in_specs=in_specs,
        out_specs=out_specs,
        out_shape=out_shape,
        scratch_shapes=[pltpu.VMEM((tm, W_MIX), BF16), pltpu.VMEM((CLASS_ROWS, LANES), F32)],
        input_output_aliases=aliases,
        compiler_params=_params("arbitrary"),
        name="mix",
    )(*args)
    return h2, hn2, route, counts[:N_CLASSES, 0].astype(jnp.int32)


def _moe_kernel(ea_ref, eb_ref, valid_ref, x_ref, w_ref, ga_ref, ua_ref, da_ref,
                gb_ref, ub_ref, db_ref, o_ref):
    i = pl.program_id(0)

    @pl.when(valid_ref[i] > 0)
    def _():
        x = x_ref[...]

        def expert(g_ref, u_ref, d_ref):
            g = jnp.dot(x, g_ref[0], preferred_element_type=F32)
            act = g * jax.nn.sigmoid(g) * jnp.dot(x, u_ref[0], preferred_element_type=F32)
            return _bdot(act, d_ref[0])

        o_ref[...] = (w_ref[:, 1:2] * expert(ga_ref, ua_ref, da_ref)
                      + w_ref[:, 2:3] * expert(gb_ref, ub_ref, db_ref)).astype(BF16)

    @pl.when(valid_ref[i] == 0)
    def _():
        o_ref[...] = jnp.zeros_like(o_ref)


def _moe(hn2, route, part_rows, part_counts, w_g, w_u, w_d):
    t = hn2.shape[0]
    cls = route[5].astype(jnp.int32)
    rank = route[4].astype(jnp.int32)
    row0 = 0
    for rows, cnt in zip(part_rows[:-1], part_counts[:-1]):
        row0 += rows
        rank = rank + jnp.where(jnp.arange(t, dtype=jnp.int32) >= row0, cnt[cls], 0)
    counts = sum(part_counts[1:], part_counts[0])
    tm = TM_MOE
    nt = t // tm + N_CLASSES
    tp = nt * tm
    padded = ((counts + tm - 1) // tm) * tm
    ends = jnp.cumsum(padded)
    offs = ends - padded
    pos = offs[cls] + rank
    starts = jnp.arange(nt, dtype=jnp.int32) * tm
    tile_cls = jnp.sum((starts[:, None] >= ends[None, :]).astype(jnp.int32), axis=1)
    valid = (tile_cls < N_CLASSES).astype(jnp.int32)
    tile_cls = jnp.minimum(tile_cls, N_CLASSES - 1)
    pair_lo = jnp.array([0, 0, 0, 1, 1, 2], jnp.int32)
    pair_hi = jnp.array([1, 2, 3, 2, 3, 3], jnp.int32)
    tile_ea = (tile_cls // N_PAIRS) * EXPERTS_PER_GROUP + pair_lo[tile_cls % N_PAIRS]
    tile_eb = (tile_cls // N_PAIRS) * EXPERTS_PER_GROUP + pair_hi[tile_cls % N_PAIRS]
    slot = jnp.stack([jnp.arange(t, dtype=F32), route[2], route[3], jnp.zeros((t,), F32)], axis=1)
    spread = (jnp.arange(tp, dtype=jnp.int32) % t).astype(F32)
    empty = jnp.stack([spread] + [jnp.zeros((tp,), F32)] * 3, axis=1)
    ws = empty.at[pos].set(slot, mode="promise_in_bounds", unique_indices=True)
    xs = hn2.at[ws[:, 0].astype(jnp.int32)].get(mode="promise_in_bounds")

    wspec_in = lambda sel: pl.BlockSpec((1, D_MODEL, D_FF), sel)
    wspec_out = lambda sel: pl.BlockSpec((1, D_FF, D_MODEL), sel)
    sel_a = lambda i, a, b, v: (a[i], 0, 0)
    sel_b = lambda i, a, b, v: (b[i], 0, 0)
    grid_spec = pltpu.PrefetchScalarGridSpec(
        num_scalar_prefetch=3,
        grid=(nt,),
        in_specs=[
            pl.BlockSpec((tm, D_MODEL), lambda i, a, b, v: (i, 0)),
            pl.BlockSpec((tm, 4), lambda i, a, b, v: (i, 0)),
            wspec_in(sel_a), wspec_in(sel_a), wspec_out(sel_a),
            wspec_in(sel_b), wspec_in(sel_b), wspec_out(sel_b),
        ],
        out_specs=pl.BlockSpec((tm, D_MODEL), lambda i, a, b, v: (i, 0)),
    )
    out = pl.pallas_call(
        _moe_kernel,
        grid_spec=grid_spec,
        out_shape=jax.ShapeDtypeStruct((tp, D_MODEL), BF16),
        compiler_params=_params("arbitrary"),
        name="moe",
    )(tile_ea, tile_eb, valid, xs, ws, w_g, w_u, w_d, w_g, w_u, w_d)
    return out.at[pos].get(mode="promise_in_bounds", unique_indices=True)


def _final_kernel(h_ref, moe_ref, mod_ref, g_ref, o_ref):
    x = h_ref[...] + mod_ref[0, 5:6, :] * moe_ref[...].astype(F32)
    o_ref[...] = _rms(x, g_ref[...])


def _final(h, moe, mod, g, seq):
    t = h.shape[0]
    tm = TM_TOK
    per_seq = seq // tm
    tok = pl.BlockSpec((tm, D_MODEL), lambda i: (i, 0))
    return pl.pallas_call(
        _final_kernel,
        grid=(t // tm,),
        in_specs=[tok, tok, pl.BlockSpec((1, N_MOD, D_MODEL), lambda i: (i // per_seq, 0, 0)),
                  _const_spec((1, D_MODEL))],
        out_specs=tok,
        out_shape=jax.ShapeDtypeStruct((t, D_MODEL), F32),
        compiler_params=_params("arbitrary"),
        name="final",
    )(h, moe, mod, g.reshape(1, D_MODEL))


def kernel(x, c, ctx, c_ctx, w_mod, b_mod, norm1_g, norm2_g, w_in, sgu_norm_g, sgu_w, sgu_b,
           conv_w, s5_lam_re, s5_lam_im, s5_log_dt, s5_b_re, s5_b_im, s5_c_re, s5_c_im, s5_d,
           glu_w, glu_b, w_branch, w_out, router_w, router_b, exp_w_gate, exp_w_up, exp_w_down,
           final_norm_g):
    nb, seq, _ = x.shape
    ctx_len = ctx.shape[1]
    h = x.reshape(nb * seq, D_MODEL)
    hc = ctx.reshape(nb * ctx_len, D_MODEL)

    mod_rows = 24
    cvec = jnp.zeros((mod_rows, D_MODEL), F32).at[:nb].set(c).at[nb].set(c_ctx)
    mods = _adaln(cvec, w_mod, b_mod).reshape(DEPTH, mod_rows, N_MOD, D_MODEL)
    router_wt = router_w.T

    all_mats = jax.vmap(_s5_mats)(s5_lam_re, s5_lam_im, s5_log_dt, s5_b_re, s5_b_im,
                                  s5_c_re, s5_c_im, s5_d)
    n_lat, n_ctx = nb * seq, nb * ctx_len
    moe = None
    mod = mod_c = None
    for l in range(DEPTH):
        last = l == DEPTH - 1
        pmod, pmod_c = mod, mod_c
        mod = mods[l, :nb]
        mod_c = jnp.broadcast_to(mods[l, nb][None], (nb, N_MOD, D_MODEL))
        w_in_b = _cast_layer(w_in, l)
        w_s5 = w_in_b[:, S5_COL0:S5_COL0 + W_MIX]
        lw = {
            "norm1_g": norm1_g[l].reshape(1, D_MODEL),
            "norm2_g": norm2_g[l].reshape(1, D_MODEL),
            "w_in": w_in_b,
            "sgu_norm_g": sgu_norm_g[l].reshape(1, W_MIX),
            "sgu_w": sgu_w[l].astype(BF16),
            "sgu_b_t": sgu_b[l].T,
            "conv_w": conv_w[l],
            "w_branch": _cast_layer(w_branch, l),
            "w_out": _cast_layer(w_out, l),
        }
        glu_wt = glu_w[l].T.astype(BF16)
        glu_bc = glu_b[l].reshape(W_MIX, 1)
        mats = tuple(m[l] for m in all_mats)
        w_g = _cast_layer(exp_w_gate, l)
        w_u = _cast_layer(exp_w_up, l)
        w_d = _cast_layer(exp_w_down, l)

        hc, xg_c = _pre(hc, mod_c, norm1_g[l], w_s5, ctx_len, moe=moe, moe_row0=n_lat, pmod=pmod_c)
        y_c, fin_c = _s5(xg_c, mats, None, nb, want_y=not last)
        h, xg = _pre(h, mod, norm1_g[l], w_s5, seq, moe=moe, pmod=pmod)
        y, _ = _s5(xg, mats, fin_c, nb, want_y=True)
        yc = _s5out(y, glu_wt, glu_bc, nb)
        if last:
            h, hn2, route, cnt = _mix(h, yc, mod, lw, seq, GRID_W, router_wt, router_b, n_lat)
            moe = _moe(hn2, route, [n_lat], [cnt], w_g, w_u, w_d)
        else:
            n_all = n_lat + n_ctx
            h, hn2, route, cnt = _mix(h, yc, mod, lw, seq, GRID_W, router_wt, router_b, n_all)
            yc_c = _s5out(y_c, glu_wt, glu_bc, nb)
            hc, hn2, route, cnt_c = _mix(hc, yc_c, mod_c, lw, ctx_len, ctx_len, router_wt, router_b,
                                         n_all, row0=n_lat, shared=(hn2, route), mod_rows_equal=True)
            moe = _moe(hn2, route, [n_lat, n_ctx], [cnt, cnt_c], w_g, w_u, w_d)
    out = _final(h, moe, mod, final_norm_g, seq)
    return out.reshape(nb, seq, D_MODEL)
```
